```python
import jax, jax.numpy as jnp
from jax import lax
import numpy as np

D_MODEL = 2048
BATCH = 1
SEQ = 16384
DEPTH = 1
DEC_BATCH = 16
DEC_SEQ = 64
PAST_LEN = 1024

CHUNK = 64
SSM_HEADS = 16
SSM_HEAD_DIM = 64
SSM_INNER = SSM_HEADS * SSM_HEAD_DIM
SSM_GROUPS = 2
SSM_STATE = 128
CONV_W = 4
CONV_CH = SSM_INNER + 2 * SSM_GROUPS * SSM_STATE
HEADS_PER_GROUP = SSM_HEADS // SSM_GROUPS
ATT_HEADS = 16
ATT_HEAD_DIM = 64
ATT_INNER = ATT_HEADS * ATT_HEAD_DIM
LEFT_CHUNKS = 8
ATT_PAST = LEFT_CHUNKS * CHUNK
BAND = ATT_PAST + CHUNK
REL_CLIP = 128
MIX_WIDTH = SSM_INNER + ATT_INNER
SPLITS = [SSM_INNER,
          SSM_INNER + CONV_CH,
          SSM_INNER + CONV_CH + SSM_HEADS,
          SSM_INNER + CONV_CH + SSM_HEADS + ATT_INNER,
          SSM_INNER + CONV_CH + SSM_HEADS + 2 * ATT_INNER]
IN_PROJ = SPLITS[-1] + ATT_INNER
N_EXPERTS = 64
TOP_K = 8
N_EXPERT_GROUPS = 8
TOPK_GROUPS = 4
EXPERT_DIM = 512
SHARED_DIM = 512
ROUTED_SCALE = 2.5
DISPATCH_BLOCK = 128
EPS = 1e-6

kernel_name = "hymba_ssd_chunkband_moe_stream_step"


def rms_norm(x, g):
    xf = x.astype(jnp.float32)
    y = xf * lax.rsqrt(jnp.mean(xf * xf, axis=-1, keepdims=True) + EPS)
    return (y * g.astype(jnp.float32)).astype(x.dtype)


def swiglu(x, w1, w3, w2):
    return (jax.nn.silu(x @ w1) * (x @ w3)) @ w2


def causal_conv(xbc, prefix, w, b):
    L = xbc.shape[1]
    xp = jnp.concatenate([prefix.astype(xbc.dtype), xbc], axis=1)
    out = b + xp[:, 0:L] * w[0]
    for k in range(1, CONV_W):
        out = out + xp[:, k:k + L] * w[k]
    return jax.nn.silu(out), xp[:, -(CONV_W - 1):]


def ssd_scan(xs, dt, A, bm, cm, h0):
    Bsz, L = xs.shape[:2]
    nc = -(-L // CHUNK)
    pad = nc * CHUNK - L
    f32 = jnp.float32
    padl = lambda t: jnp.pad(t.astype(f32), [(0, 0), (0, pad)] + [(0, 0)] * (t.ndim - 2))
    x = padl(xs).reshape(Bsz, nc, CHUNK, SSM_HEADS, SSM_HEAD_DIM)
    d = padl(dt).reshape(Bsz, nc, CHUNK, SSM_HEADS)
    bh = jnp.repeat(padl(bm), HEADS_PER_GROUP, axis=2).reshape(Bsz, nc, CHUNK, SSM_HEADS, SSM_STATE)
    ch = jnp.repeat(padl(cm), HEADS_PER_GROUP, axis=2).reshape(Bsz, nc, CHUNK, SSM_HEADS, SSM_STATE)
    acs = jnp.cumsum(d * A.astype(f32), axis=2)
    xdt = x * d[..., None]
    seg = jnp.moveaxis(acs, 3, 2)
    causal = jnp.tril(jnp.ones((CHUNK, CHUNK), bool))
    lmat = jnp.exp(jnp.where(causal, seg[..., :, None] - seg[..., None, :], -jnp.inf))
    cb = jnp.einsum('bcihn,bcjhn->bchij', ch, bh)
    y_diag = jnp.einsum('bchij,bcjhp->bcihp', cb * lmat, xdt)
    decay_to_end = jnp.exp(acs[:, :, -1:, :] - acs)
    chunk_states = jnp.einsum('bcjhn,bcjh,bcjhp->bchpn', bh, decay_to_end, xdt)
    chunk_decay = jnp.exp(acs[:, :, -1, :])

    def step(h, inp):
        st, dec = inp
        return h * dec[:, :, None, None] + st, h

    h_final, h_starts = lax.scan(step, h0.astype(f32),
                                 (jnp.moveaxis(chunk_states, 1, 0), jnp.moveaxis(chunk_decay, 1, 0)))
    h_starts = jnp.moveaxis(h_starts, 0, 1)
    y_off = jnp.einsum('bcihn,bchpn,bcih->bcihp', ch, h_starts, jnp.exp(acs))
    y = (y_diag + y_off).reshape(Bsz, nc * CHUNK, SSM_HEADS, SSM_HEAD_DIM)[:, :L]
    return y.astype(xs.dtype), h_final


def rel_bias_matrix(table, n_q, n_k, n_past):
    rel = (n_past + jnp.arange(n_q))[:, None] - jnp.arange(n_k)[None, :]
    return table[:, jnp.clip(rel, -REL_CLIP, REL_CLIP) + REL_CLIP]


def band_attend(q, k, v, bias, valid):
    s = jnp.einsum('bqhd,bkhd->bhqk', q, k).astype(jnp.float32) * (ATT_HEAD_DIM ** -0.5)
    s = s + bias.astype(jnp.float32)
    if valid is not None:
        s = jnp.where(valid, s, -jnp.inf)
    p = jax.nn.softmax(s, axis=-1).astype(v.dtype)
    return jnp.einsum('bhqk,bkhd->bqhd', p, v)


def chunk_band_attention(q, k, v, table):
    Bsz, L, H, Dh = q.shape
    nc = L // CHUNK
    padk = ((0, 0), (ATT_PAST, 0), (0, 0), (0, 0))
    kp, vp = jnp.pad(k, padk), jnp.pad(v, padk)
    bias = rel_bias_matrix(table, CHUNK, BAND, ATT_PAST)
    qc = jnp.moveaxis(q.reshape(Bsz, nc, CHUNK, H, Dh), 1, 0)
    key_off = jnp.arange(BAND) - ATT_PAST

    def one(args):
        c, qb = args
        kb = lax.dynamic_slice_in_dim(kp, c * CHUNK, BAND, axis=1)
        vb = lax.dynamic_slice_in_dim(vp, c * CHUNK, BAND, axis=1)
        return band_attend(qb, kb, vb, bias, (c * CHUNK + key_off) >= 0)

    out = lax.map(one, (jnp.arange(nc), qc))
    return jnp.moveaxis(out, 0, 1).reshape(Bsz, L, H * Dh)


def cached_chunk_attention(q, k, v, k_cache, v_cache, table):
    Bsz, L, H, Dh = q.shape
    n_past = k_cache.shape[1]
    kk = jnp.concatenate([k_cache.astype(k.dtype), k], axis=1)
    vv = jnp.concatenate([v_cache.astype(v.dtype), v], axis=1)
    bias = rel_bias_matrix(table, L, n_past + L, n_past)
    return band_attend(q, kk, vv, bias, None).reshape(Bsz, L, H * Dh)


def token_mixer(h, conv_prefix, h0, k_cache, v_cache, w):
    Bsz, L, _ = h.shape
    proj = h @ w['w_in']
    z, xbc, dt_raw, q, k, v = jnp.split(proj, SPLITS, axis=-1)
    xbc, conv_state = causal_conv(xbc, conv_prefix, w['conv_w'], w['conv_b'])
    xs, bm, cm = jnp.split(xbc, [SSM_INNER, SSM_INNER + SSM_GROUPS * SSM_STATE], axis=-1)
    xs = xs.reshape(Bsz, L, SSM_HEADS, SSM_HEAD_DIM)
    bm = bm.reshape(Bsz, L, SSM_GROUPS, SSM_STATE)
    cm = cm.reshape(Bsz, L, SSM_GROUPS, SSM_STATE)
    dt = jax.nn.softplus(dt_raw.astype(jnp.float32) + w['dt_bias'].astype(jnp.float32))
    A = -jnp.exp(w['a_log'].astype(jnp.float32))
    y, ssm_state = ssd_scan(xs, dt, A, bm, cm, h0)
    y = (y + w['d_skip'][:, None] * xs).reshape(Bsz, L, SSM_INNER)
    y = rms_norm(y * jax.nn.silu(z), w['gn_w'])
    q = q.reshape(Bsz, L, ATT_HEADS, ATT_HEAD_DIM)
    k = k.reshape(Bsz, L, ATT_HEADS, ATT_HEAD_DIM)
    v = v.reshape(Bsz, L, ATT_HEADS, ATT_HEAD_DIM)
    if k_cache is None:
        att = chunk_band_attention(q, k, v, w['rel_bias'])
        keep = min(ATT_PAST, L)
        k_new, v_new = k[:, L - keep:], v[:, L - keep:]
    else:
        att = cached_chunk_attention(q, k, v, k_cache, v_cache, w['rel_bias'])
        k_new, v_new = k, v
    out = jnp.concatenate([y, att], axis=-1) @ w['w_out']
    return out, conv_state, ssm_state, k_new, v_new


def routed_experts(hf, topi, topw, w1, w3, w2):
    T, D = hf.shape
    A = T * TOP_K
    e_flat = topi.reshape(A)
    tok_flat = jnp.repeat(jnp.arange(T, dtype=jnp.int32), TOP_K)
    w_flat = topw.reshape(A)
    order = jnp.argsort(e_flat)
    e_sorted = e_flat[order]
    counts = jnp.bincount(e_flat, length=N_EXPERTS)
    padded = (counts + DISPATCH_BLOCK - 1) // DISPATCH_BLOCK * DISPATCH_BLOCK
    start = jnp.cumsum(counts) - counts
    pend = jnp.cumsum(padded)
    pstart = pend - padded
    dest = pstart[e_sorted] + (jnp.arange(A) - start[e_sorted])
    n_blocks = -(-(A + N_EXPERTS * (DISPATCH_BLOCK - 1)) // DISPATCH_BLOCK)
    n_slots = n_blocks * DISPATCH_BLOCK
    slot_tok = jnp.zeros((n_slots,), jnp.int32).at[dest].set(tok_flat[order])
    slot_w = jnp.zeros((n_slots,), hf.dtype).at[dest].set(w_flat[order])
    block_e = jnp.minimum(jnp.searchsorted(pend, jnp.arange(n_blocks) * DISPATCH_BLOCK, side='right'),
                          N_EXPERTS - 1)

    def body(y, blk):
        tok, wt, e = blk
        o = swiglu(hf[tok], w1[e], w3[e], w2[e])
        return y.at[tok].add(o * wt[:, None]), None

    y, _ = lax.scan(body, jnp.zeros_like(hf),
                    (slot_tok.reshape(n_blocks, DISPATCH_BLOCK), slot_w.reshape(n_blocks, DISPATCH_BLOCK), block_e))
    return y


def moe_ffn(h, w):
    Bsz, L, D = h.shape
    hf = h.reshape(Bsz * L, D)
    T = hf.shape[0]
    scores = jax.nn.sigmoid((hf @ w['w_router']).astype(jnp.float32))
    sel = scores + w['router_bias'].astype(jnp.float32)
    grp = sel.reshape(T, N_EXPERT_GROUPS, N_EXPERTS // N_EXPERT_GROUPS)
    grp_score = jnp.sum(lax.top_k(grp, 2)[0], axis=-1)
    _, gidx = lax.top_k(grp_score, TOPK_GROUPS)
    gmask = jnp.any(gidx[..., None] == jnp.arange(N_EXPERT_GROUPS), axis=1)
    emask = jnp.repeat(gmask, N_EXPERTS // N_EXPERT_GROUPS, axis=1)
    _, topi = lax.top_k(jnp.where(emask, sel, -jnp.inf), TOP_K)
    topw = jnp.take_along_axis(scores, topi, axis=1)
    topw = topw / jnp.sum(topw, axis=-1, keepdims=True) * ROUTED_SCALE
    routed = routed_experts(hf, topi, topw.astype(h.dtype), w['w1'], w['w3'], w['w2'])
    shared = swiglu(hf, w['ws1'], w['ws3'], w['ws2'])
    return (routed + shared).reshape(Bsz, L, D)


def layer(x, c, conv_prefix, h0, k_cache, v_cache, w):
    ada = (jax.nn.silu(c) @ w['w_ada'] + w['b_ada'])[:, None, :]
    shift_m, scale_m, gate_m, shift_f, scale_f, gate_f = jnp.split(ada, 6, axis=-1)
    hm = rms_norm(x, w['norm_pre_mix']) * (1 + scale_m) + shift_m
    mix, conv_state, ssm_state, k_new, v_new = token_mixer(hm, conv_prefix, h0, k_cache, v_cache, w)
    x = x + gate_m * rms_norm(mix, w['norm_post_mix'])
    hf = rms_norm(x, w['norm_pre_ffn']) * (1 + scale_f) + shift_f
    x = x + gate_f * rms_norm(moe_ffn(hf, w), w['norm_post_ffn'])
    return x, conv_state, ssm_state, k_new, v_new


def setup_inputs(seed: int = 0) -> dict:
    key = jax.random.key(seed)
    ks = jax.random.split(key, 32)
    f32 = jnp.float32
    nrm = lambda k, shape, s: jax.random.normal(k, shape, f32) * s
    att_cache = min(ATT_PAST, PAST_LEN)
    dt0 = jnp.exp(jax.random.uniform(ks[10], (DEPTH, SSM_HEADS), f32, np.log(1e-3), np.log(1e-1)))
    return {
        "x_prompt": nrm(ks[0], (BATCH, SEQ, D_MODEL), 1.0),
        "x_sample": nrm(ks[1], (DEC_BATCH, DEC_SEQ, D_MODEL), 1.0),
        "cache_conv": nrm(ks[2], (DEPTH, DEC_BATCH, CONV_W - 1, CONV_CH), 1.0),
        "state_ssm": nrm(ks[3], (DEPTH, DEC_BATCH, SSM_HEADS, SSM_HEAD_DIM, SSM_STATE), 0.1),
        "cache_k": nrm(ks[4], (DEPTH, DEC_BATCH, att_cache, ATT_HEADS, ATT_HEAD_DIM), 1.0),
        "cache_v": nrm(ks[5], (DEPTH, DEC_BATCH, att_cache, ATT_HEADS, ATT_HEAD_DIM), 1.0),
        "c_prompt": nrm(ks[6], (BATCH, D_MODEL), 1.0),
        "c_sample": nrm(ks[7], (DEC_BATCH, D_MODEL), 1.0),
        "w_ada": nrm(ks[8], (DEPTH, D_MODEL, 6 * D_MODEL), 0.5 * D_MODEL ** -0.5),
        "b_ada": nrm(ks[9], (DEPTH, 6 * D_MODEL), 0.02),
        "norm_pre_mix": 1.0 + nrm(ks[11], (DEPTH, D_MODEL), 0.05),
        "norm_post_mix": 1.0 + nrm(ks[12], (DEPTH, D_MODEL), 0.05),
        "norm_pre_ffn": 1.0 + nrm(ks[13], (DEPTH, D_MODEL), 0.05),
        "norm_post_ffn": 1.0 + nrm(ks[14], (DEPTH, D_MODEL), 0.05),
        "w_in": nrm(ks[15], (DEPTH, D_MODEL, IN_PROJ), D_MODEL ** -0.5),
        "conv_w": nrm(ks[16], (DEPTH, CONV_W, CONV_CH), CONV_W ** -0.5),
        "conv_b": nrm(ks[17], (DEPTH, CONV_CH), 0.02),
        "dt_bias": dt0 + jnp.log(-jnp.expm1(-dt0)),
        "a_log": jnp.log(jax.random.uniform(ks[18], (DEPTH, SSM_HEADS), f32, 1.0, 16.0)),
        "d_skip": 1.0 + nrm(ks[19], (DEPTH, SSM_HEADS), 0.1),
        "gn_w": 1.0 + nrm(ks[20], (DEPTH, SSM_INNER), 0.05),
        "rel_bias": nrm(ks[21], (DEPTH, ATT_HEADS, 2 * REL_CLIP + 1), 0.5),
        "w_out": nrm(ks[22], (DEPTH, MIX_WIDTH, D_MODEL), MIX_WIDTH ** -0.5),
        "w_router": nrm(ks[23], (DEPTH, D_MODEL, N_EXPERTS), D_MODEL ** -0.5),
        "router_bias": nrm(ks[24], (DEPTH, N_EXPERTS), 0.01),
        "w1": nrm(ks[25], (DEPTH, N_EXPERTS, D_MODEL, EXPERT_DIM), D_MODEL ** -0.5),
        "w3": nrm(ks[26], (DEPTH, N_EXPERTS, D_MODEL, EXPERT_DIM), D_MODEL ** -0.5),
        "w2": nrm(ks[27], (DEPTH, N_EXPERTS, EXPERT_DIM, D_MODEL), EXPERT_DIM ** -0.5),
        "ws1": nrm(ks[28], (DEPTH, D_MODEL, SHARED_DIM), D_MODEL ** -0.5),
        "ws3": nrm(ks[29], (DEPTH, D_MODEL, SHARED_DIM), D_MODEL ** -0.5),
        "ws2": nrm(ks[30], (DEPTH, SHARED_DIM, D_MODEL), SHARED_DIM ** -0.5),
    }


def reference(x_prompt, x_sample, cache_conv, state_ssm, cache_k, cache_v, c_prompt, c_sample,
              w_ada, b_ada, norm_pre_mix, norm_post_mix, norm_pre_ffn, norm_post_ffn,
              w_in, conv_w, conv_b, dt_bias, a_log, d_skip, gn_w, rel_bias, w_out,
              w_router, router_bias, w1, w3, w2, ws1, ws3, ws2):
    weights = dict(w_ada=w_ada, b_ada=b_ada, norm_pre_mix=norm_pre_mix, norm_post_mix=norm_post_mix,
                   norm_pre_ffn=norm_pre_ffn, norm_post_ffn=norm_post_ffn, w_in=w_in, conv_w=conv_w,
                   conv_b=conv_b, dt_bias=dt_bias, a_log=a_log, d_skip=d_skip, gn_w=gn_w,
                   rel_bias=rel_bias, w_out=w_out, w_router=w_router, router_bias=router_bias,
                   w1=w1, w3=w3, w2=w2, ws1=ws1, ws3=ws3, ws2=ws2)
    bp = x_prompt.shape[0]
    xp, xs = x_prompt, x_sample
    cp, sp, kp, vp = [], [], [], []
    cs, ss, ksm, vsm = [], [], [], []
    for l in range(DEPTH):
        w = {name: arr[l] for name, arr in weights.items()}
        conv0 = jnp.zeros((bp, CONV_W - 1, CONV_CH), x_prompt.dtype)
        h0 = jnp.zeros((bp, SSM_HEADS, SSM_HEAD_DIM, SSM_STATE), jnp.float32)
        xp, c1, s1, k1, v1 = layer(xp, c_prompt, conv0, h0, None, None, w)
        xs, c2, s2, k2, v2 = layer(xs, c_sample, cache_conv[l], state_ssm[l], cache_k[l], cache_v[l], w)
        cp.append(c1); sp.append(s1); kp.append(k1); vp.append(v1)
        cs.append(c2); ss.append(s2); ksm.append(k2); vsm.append(v2)
    conv_prompt, ssm_prompt = jnp.stack(cp), jnp.stack(sp)
    k_prompt, v_prompt = jnp.stack(kp), jnp.stack(vp)
    conv_sample, ssm_sample = jnp.stack(cs), jnp.stack(ss)
    k_sample, v_sample = jnp.stack(ksm), jnp.stack(vsm)
    return (xp, xs, conv_prompt, ssm_prompt, k_prompt, v_prompt, conv_sample, ssm_sample, k_sample, v_sample)
```

```python
import functools

import numpy as np
import jax
import jax.numpy as jnp
from jax import lax
from jax.experimental import pallas as pl
from jax.experimental.pallas import tpu as pltpu

F32 = jnp.float32
BF16 = jnp.bfloat16
HIGHEST = lax.Precision.HIGHEST

CHUNK = 64
SSM_HEAD_DIM = 64
SSM_GROUPS = 2
SSM_STATE = 128
CONV_W = 4
ATT_HEAD_DIM = 64
LEFT_CHUNKS = 8
ATT_PAST = LEFT_CHUNKS * CHUNK
REL_CLIP = 128
TOP_K = 8
N_EXPERT_GROUPS = 8
TOPK_GROUPS = 4
ROUTED_SCALE = 2.5
EPS = 1e-6
NEG_BIG = -1e30

LANES = 128
PAIR = 2 * ATT_HEAD_DIM
QPAIR = 2 * CHUNK
KWIN = ATT_PAST + QPAIR
VMEM_LIMIT = 56 * 1024 * 1024


def _cparams(sem):
    return pltpu.CompilerParams(dimension_semantics=sem, vmem_limit_bytes=VMEM_LIMIT)


def _pick(n, cands):
    for c in cands:
        if n % c == 0:
            return c
    raise ValueError(f"no tile for {n} in {cands}")


def _silu(x):
    return x * jax.nn.sigmoid(x)


def _rms(x, g):
    ms = jnp.mean(x * x, axis=-1, keepdims=True)
    return x * lax.rsqrt(ms + EPS) * g


def _bdot(a, b):
    return jnp.dot(a, b, preferred_element_type=F32)


def _ada_kernel(c_ref, w_ref, b_ref, o_ref):
    a = _silu(c_ref[...])
    o_ref[...] = jnp.dot(a, w_ref[...], precision=HIGHEST, preferred_element_type=F32) + b_ref[...]


def _ada(c_pad, w_ada, b_ada):
    m, d = c_pad.shape
    n = w_ada.shape[1]
    tn = _pick(n, (1024, 512, 256, 128))
    return pl.pallas_call(
        _ada_kernel,
        grid=(n // tn,),
        in_specs=[pl.BlockSpec((m, d), lambda j: (0, 0)),
                  pl.BlockSpec((d, tn), lambda j: (0, j)),
                  pl.BlockSpec((1, tn), lambda j: (0, j))],
        out_specs=pl.BlockSpec((m, tn), lambda j: (0, j)),
        out_shape=jax.ShapeDtypeStruct((m, n), F32),
        compiler_params=_cparams(("arbitrary",)),
        name="ada",
    )(c_pad, w_ada, b_ada.reshape(1, n))


def _inproj_kernel(x_ref, sc_ref, sh_ref, g_ref, w_ref, wdt_ref, o_ref, dt_ref, hm_ref):
    @pl.when(pl.program_id(1) == 0)
    def _():
        x = x_ref[...]
        tm, d = x.shape
        y = _rms(x, g_ref[...]).reshape(tm // CHUNK, CHUNK, d)
        h = (y * (1.0 + sc_ref[...]) + sh_ref[...]).reshape(tm, d).astype(BF16)
        hm_ref[...] = h
        dt_ref[...] = _bdot(h, wdt_ref[...])

    o_ref[...] = _bdot(hm_ref[...], w_ref[...])


def _inproj(x_all, scale, shift, g, w_main, w_dt):
    t, d = x_all.shape
    n = w_main.shape[1]
    tm = _pick(t, (1024, 512, 256, 128, 64))
    tn = _pick(n, (512, 256, 128))
    nc = tm // CHUNK
    return pl.pallas_call(
        _inproj_kernel,
        grid=(t // tm, n // tn),
        in_specs=[pl.BlockSpec((tm, d), lambda i, j: (i, 0)),
                  pl.BlockSpec((nc, 1, d), lambda i, j: (i, 0, 0)),
                  pl.BlockSpec((nc, 1, d), lambda i, j: (i, 0, 0)),
                  pl.BlockSpec((1, d), lambda i, j: (0, 0)),
                  pl.BlockSpec((d, tn), lambda i, j: (0, j)),
                  pl.BlockSpec((d, LANES), lambda i, j: (0, 0))],
        out_specs=[pl.BlockSpec((tm, tn), lambda i, j: (i, j)),
                   pl.BlockSpec((tm, LANES), lambda i, j: (i, 0))],
        out_shape=[jax.ShapeDtypeStruct((t, n), F32),
                   jax.ShapeDtypeStruct((t, LANES), F32)],
        scratch_shapes=[pltpu.VMEM((tm, d), BF16)],
        compiler_params=_cparams(("arbitrary", "arbitrary")),
        name="inproj",
    )(x_all, scale, shift, g, w_main, w_dt)


def _ssd_kernel(seq_ref, first_ref,
                z_ref, xs_ref, bc_ref, dt_ref, prex_ref, prebc_ref, h0_ref,
                cwx_ref, cwbc_ref, cbx_ref, cbbc_ref, dtb_ref, aneg_ref, dsk_ref, gnw_ref,
                e_ref, tri_ref,
                y_ref, st_out_ref,
                xpx_scr, xpbc_scr, st_scr):
    del seq_ref
    c = pl.program_id(0)
    inner = xs_ref.shape[1]
    gw = inner // SSM_GROUPS
    n = SSM_STATE
    pad = 8

    @pl.when(first_ref[c] == 1)
    def _():
        xpx_scr[0:pad, :] = prex_ref[0]
        xpbc_scr[0:pad, :] = prebc_ref[0]
        st_scr[...] = h0_ref[0]

    xpx_scr[pad:pad + CHUNK, :] = xs_ref[...]
    xpbc_scr[pad:pad + CHUNK, :] = bc_ref[...]

    def conv(xp, w_ref, b_ref):
        base = pad - (CONV_W - 1)
        acc = b_ref[...] + xp[base:base + CHUNK, :] * w_ref[0:1, :]
        for k in range(1, CONV_W):
            acc = acc + xp[base + k:base + k + CHUNK, :] * w_ref[k:k + 1, :]
        return _silu(acc)

    xs = conv(xpx_scr, cwx_ref, cbx_ref)
    bc = conv(xpbc_scr, cwbc_ref, cbbc_ref)
    xpx_scr[0:pad, :] = xpx_scr[CHUNK:CHUNK + pad, :]
    xpbc_scr[0:pad, :] = xpbc_scr[CHUNK:CHUNK + pad, :]

    dtv = dt_ref[...] + dtb_ref[...]
    dt = jnp.maximum(dtv, 0.0) + jnp.log(1.0 + jnp.exp(-jnp.abs(dtv)))
    da = dt * aneg_ref[...]
    acs = jnp.dot(tri_ref[...], da, precision=HIGHEST, preferred_element_type=F32)
    full = jnp.dot(jnp.concatenate([dt, acs], axis=0), e_ref[...],
                   precision=HIGHEST, preferred_element_type=F32)
    dtf = full[0:CHUNK]
    af = full[CHUNK:2 * CHUNK]

    row = lax.broadcasted_iota(jnp.int32, (CHUNK, inner), 0)
    lj = lax.broadcasted_iota(jnp.int32, (CHUNK, inner), 1) & (SSM_HEAD_DIM - 1)
    aj = jnp.sum(jnp.where(row == lj, af, 0.0), axis=0, keepdims=True)
    lmat = jnp.exp(jnp.where(row >= lj, af - aj, NEG_BIG))
    alast = af[CHUNK - 1:CHUNK, :]
    xdt = xs * dtf
    xw = xdt * jnp.exp(alast - af)
    cdec = jnp.exp(alast)
    eaf = jnp.exp(af)

    lane = lax.broadcasted_iota(jnp.int32, (CHUNK, PAIR), 1)
    st = st_scr[...]
    ydiag, yoff, stn = [], [], []
    for g in range(SSM_GROUPS):
        bg = bc[:, g * n:(g + 1) * n].astype(BF16)
        cg = bc[:, (SSM_GROUPS + g) * n:(SSM_GROUPS + g + 1) * n].astype(BF16)
        bb = jnp.concatenate([bg, bg], axis=0)
        cbb = lax.dot_general(cg, bb, (((1,), (1,)), ((), ())), preferred_element_type=F32)
        stg = st[:, g * gw:(g + 1) * gw]
        yoff.append(_bdot(cg, stg.astype(BF16)))
        for p in range(gw // PAIR):
            lo = g * gw + p * PAIR
            m = (cbb * lmat[:, lo:lo + PAIR]).astype(BF16)
            xd = xdt[:, lo:lo + PAIR]
            w = jnp.concatenate([jnp.where(lane < SSM_HEAD_DIM, xd, 0.0),
                                 jnp.where(lane >= SSM_HEAD_DIM, xd, 0.0)], axis=0).astype(BF16)
            ydiag.append(_bdot(m, w))
        upd = lax.dot_general(bg, xw[:, g * gw:(g + 1) * gw].astype(BF16),
                              (((0,), (0,)), ((), ())), preferred_element_type=F32)
        stn.append(cdec[:, g * gw:(g + 1) * gw] * stg + upd)

    y = jnp.concatenate(ydiag, axis=1) + jnp.concatenate(yoff, axis=1) * eaf + dsk_ref[...] * xs
    y = y * _silu(z_ref[...])
    y_ref[...] = _rms(y, gnw_ref[...])
    st_new = jnp.concatenate(stn, axis=1)
    st_scr[...] = st_new
    st_out_ref[0] = st_new


def _ssd(proj, dt_raw, seq_of_chunk, first_of_chunk, pre_x, pre_bc, h0t, consts, inner):
    t = proj.shape[0]
    nch = t // CHUNK
    nseq = h0t.shape[0]
    bcw = pre_bc.shape[-1]
    cx = inner // inner
    bc_blk = (5 * inner) // bcw
    cmap = lambda blk: (lambda c, s, f: (c, blk))
    smap3 = lambda c, s, f: (s[c], 0, 0)
    const2 = lambda c, s, f: (0, 0)
    (cwx, cwbc, cbx, cbbc, dtb, aneg, dsk, gnw, emat, tri) = consts
    grid_spec = pltpu.PrefetchScalarGridSpec(
        num_scalar_prefetch=2,
        grid=(nch,),
        in_specs=[pl.BlockSpec((CHUNK, inner), cmap(0)),
                  pl.BlockSpec((CHUNK, inner), cmap(cx)),
                  pl.BlockSpec((CHUNK, bcw), cmap(bc_blk)),
                  pl.BlockSpec((CHUNK, LANES), lambda c, s, f: (c, 0)),
                  pl.BlockSpec((1, 8, inner), smap3),
                  pl.BlockSpec((1, 8, bcw), smap3),
                  pl.BlockSpec((1, SSM_STATE, inner), smap3),
                  pl.BlockSpec(cwx.shape, const2), pl.BlockSpec(cwbc.shape, const2),
                  pl.BlockSpec(cbx.shape, const2), pl.BlockSpec(cbbc.shape, const2),
                  pl.BlockSpec(dtb.shape, const2), pl.BlockSpec(aneg.shape, const2),
                  pl.BlockSpec(dsk.shape, const2), pl.BlockSpec(gnw.shape, const2),
                  pl.BlockSpec(emat.shape, const2), pl.BlockSpec(tri.shape, const2)],
        out_specs=[pl.BlockSpec((CHUNK, inner), lambda c, s, f: (c, 0)),
                   pl.BlockSpec((1, SSM_STATE, inner), smap3)],
        scratch_shapes=[pltpu.VMEM((CHUNK + 8, inner), F32),
                        pltpu.VMEM((CHUNK + 8, bcw), F32),
                        pltpu.VMEM((SSM_STATE, inner), F32)],
    )
    return pl.pallas_call(
        _ssd_kernel,
        grid_spec=grid_spec,
        out_shape=[jax.ShapeDtypeStruct((t, inner), F32),
                   jax.ShapeDtypeStruct((nseq, SSM_STATE, inner), F32)],
        compiler_params=_cparams(("arbitrary",)),
        name="ssd",
    )(seq_of_chunk, first_of_chunk, proj, proj, proj, dt_raw, pre_x, pre_bc, h0t, *consts)


def _attn_pairs(q_ref, kwin, vtwin, bias_ref, o_ref, n_steps, n_masked_fn):
    n_pairs = q_ref.shape[1] // PAIR
    rowp = lax.broadcasted_iota(jnp.int32, (PAIR, QPAIR), 0)
    krow = lax.broadcasted_iota(jnp.int32, (KWIN, 2 * QPAIR), 0)

    for jj in range(n_steps):
        n_masked = n_masked_fn(jj)

        def body(hp, carry, jj=jj, n_masked=n_masked):
            lo = pl.multiple_of(hp * PAIR, PAIR)
            q = q_ref[jj * QPAIR:(jj + 1) * QPAIR, pl.ds(lo, PAIR)] * (ATT_HEAD_DIM ** -0.5)
            qt = q.T
            w = jnp.concatenate([jnp.where(rowp < ATT_HEAD_DIM, qt, 0.0),
                                 jnp.where(rowp >= ATT_HEAD_DIM, qt, 0.0)], axis=1).astype(BF16)
            kb = kwin[jj * QPAIR:jj * QPAIR + KWIN, pl.ds(lo, PAIR)]
            s = _bdot(kb, w) + bias_ref[hp]
            if n_masked is not None:
                s = jnp.where(krow < n_masked, NEG_BIG, s)
            mx = jnp.max(s, axis=0, keepdims=True)
            p = jnp.exp(s - mx)
            den = jnp.sum(p, axis=0, keepdims=True)
            vb = vtwin[pl.ds(lo, PAIR), jj * QPAIR:jj * QPAIR + KWIN]
            o2 = _bdot(vb, p.astype(BF16)) / den
            ot = jnp.where(rowp < ATT_HEAD_DIM, o2[:, 0:QPAIR], o2[:, QPAIR:2 * QPAIR])
            o_ref[jj * QPAIR:(jj + 1) * QPAIR, pl.ds(lo, PAIR)] = ot.T
            return carry

        lax.fori_loop(0, n_pairs, body, 0)


def _attn_prompt_kernel(q_ref, kp_ref, kc_ref, vp_ref, vc_ref, bias_ref, o_ref, kwin, vtwin):
    i = pl.program_id(0)
    tq = q_ref.shape[0]
    kwin[0:ATT_PAST, :] = kp_ref[...].astype(BF16)
    kwin[ATT_PAST:ATT_PAST + tq, :] = kc_ref[...].astype(BF16)
    vtwin[:, 0:ATT_PAST] = vp_ref[...].T.astype(BF16)
    vtwin[:, ATT_PAST:ATT_PAST + tq] = vc_ref[...].T.astype(BF16)
    _attn_pairs(q_ref, kwin, vtwin, bias_ref, o_ref, tq // QPAIR,
                lambda jj: ATT_PAST - jj * QPAIR - i * tq)


def _attn_prompt(proj, bias2, t_prompt, width):
    tq = ATT_PAST
    assert t_prompt % tq == 0
    nb = width // width
    qb, kb, vb = 2 * nb, 3 * nb, 4 * nb
    prev = lambda i: jnp.maximum(i - 1, 0)
    return pl.pallas_call(
        _attn_prompt_kernel,
        grid=(t_prompt // tq,),
        in_specs=[pl.BlockSpec((tq, width), lambda i: (i, qb)),
                  pl.BlockSpec((tq, width), lambda i: (prev(i), kb)),
                  pl.BlockSpec((tq, width), lambda i: (i, kb)),
                  pl.BlockSpec((tq, width), lambda i: (prev(i), vb)),
                  pl.BlockSpec((tq, width), lambda i: (i, vb)),
                  pl.BlockSpec(bias2.shape, lambda i: (0, 0, 0))],
        out_specs=pl.BlockSpec((tq, width), lambda i: (i, 0)),
        out_shape=jax.ShapeDtypeStruct((t_prompt, width), F32),
        scratch_shapes=[pltpu.VMEM((ATT_PAST + tq, width), BF16),
                        pltpu.VMEM((width, ATT_PAST + tq), BF16)],
        compiler_params=_cparams(("arbitrary",)),
        name="attn_prompt",
    )(proj, proj, proj, proj, proj, bias2)


def _attn_sample_kernel(q_ref, k_ref, v_ref, bias_ref, o_ref, kwin, vtwin):
    kwin[...] = k_ref[...].astype(BF16)
    vtwin[...] = v_ref[...].T.astype(BF16)
    _attn_pairs(q_ref, kwin, vtwin, bias_ref, o_ref, 1, lambda jj: None)


def _attn_sample(q_pad, k_win, v_win, bias2, n_seq, width):
    return pl.pallas_call(
        _attn_sample_kernel,
        grid=(n_seq,),
        in_specs=[pl.BlockSpec((QPAIR, width), lambda b: (b, 0)),
                  pl.BlockSpec((KWIN, width), lambda b: (b, 0)),
                  pl.BlockSpec((KWIN, width), lambda b: (b, 0)),
                  pl.BlockSpec(bias2.shape, lambda b: (0, 0, 0))],
        out_specs=pl.BlockSpec((QPAIR, width), lambda b: (b, 0)),
        out_shape=jax.ShapeDtypeStruct((n_seq * QPAIR, width), F32),
        scratch_shapes=[pltpu.VMEM((KWIN, width), BF16),
                        pltpu.VMEM((width, KWIN), BF16)],
        compiler_params=_cparams(("arbitrary",)),
        name="attn_sample",
    )(q_pad, k_win, v_win, bias2)


def _attn_bias(table):
    h = table.shape[0]
    j = np.arange(KWIN)[:, None]
    col = np.arange(2 * QPAIR)[None, :]
    hloc, q = col // QPAIR, col % QPAIR
    u, i = q // CHUNK, q % CHUNK
    jj = j - CHUNK * u
    valid = (jj >= 0) & (jj < ATT_PAST + CHUNK)
    rel = np.clip(ATT_PAST + i - jj, -REL_CLIP, REL_CLIP) + REL_CLIP
    b = table.reshape(h // 2, 2, -1)[:, np.broadcast_to(hloc, rel.shape), rel]
    return jnp.where(jnp.asarray(valid)[None], b, NEG_BIG).astype(F32)


def _outproj_kernel(y_ref, a_ref, x_ref, gm_ref, scf_ref, shf_ref, npost_ref, npre_ref,
                    wo1_ref, wo2_ref, wr_ref, x1_ref, hf_ref, lg_ref):
    tm, d = x_ref.shape
    mix = _bdot(y_ref[...].astype(BF16), wo1_ref[...]) + _bdot(a_ref[...].astype(BF16), wo2_ref[...])
    nm = _rms(mix, npost_ref[...]).reshape(tm // CHUNK, CHUNK, d)
    x1 = x_ref[...].reshape(tm // CHUNK, CHUNK, d) + gm_ref[...] * nm
    x1_ref[...] = x1.reshape(tm, d)
    hn = _rms(x1, npre_ref[...])
    hf = (hn * (1.0 + scf_ref[...]) + shf_ref[...]).reshape(tm, d)
    hf_ref[...] = hf
    lg_ref[...] = lax.dot_general(wr_ref[...], hf, (((1,), (1,)), ((), ())),
                                  precision=HIGHEST, preferred_element_type=F32)


def _outproj(y_ssd, att, x_all, gate_m, scale_f, shift_f, npost, npre, wo1, wo2, wr_t):
    t, d = x_all.shape
    inner = y_ssd.shape[1]
    ne = wr_t.shape[0]
    tm = _pick(t, (256, 128, 64))
    nc = tm // CHUNK
    row = lambda i: (i, 0)
    tab = lambda i: (i, 0, 0)
    const = lambda i: (0, 0)
    return pl.pallas_call(
        _outproj_kernel,
        grid=(t // tm,),
        in_specs=[pl.BlockSpec((tm, inner), row), pl.BlockSpec((tm, att.shape[1]), row),
                  pl.BlockSpec((tm, d), row),
                  pl.BlockSpec((nc, 1, d), tab), pl.BlockSpec((nc, 1, d), tab), pl.BlockSpec((nc, 1, d), tab),
                  pl.BlockSpec((1, d), const), pl.BlockSpec((1, d), const),
                  pl.BlockSpec(wo1.shape, const), pl.BlockSpec(wo2.shape, const),
                  pl.BlockSpec(wr_t.shape, const)],
        out_specs=[pl.BlockSpec((tm, d), row), pl.BlockSpec((tm, d), row),
                   pl.BlockSpec((ne, tm), lambda i: (0, i))],
        out_shape=[jax.ShapeDtypeStruct((t, d), F32), jax.ShapeDtypeStruct((t, d), F32),
                   jax.ShapeDtypeStruct((ne, t), F32)],
        compiler_params=_cparams(("arbitrary",)),
        name="outproj",
    )(y_ssd, att, x_all, gate_m, scale_f, shift_f, npost, npre, wo1, wo2, wr_t)


def _route_kernel(lg_ref, rb_ref, ti_ref, tw_ref):
    ne, tt = lg_ref.shape
    gs = ne // N_EXPERT_GROUPS
    scores = jax.nn.sigmoid(lg_ref[...])
    sel = scores + rb_ref[...]
    g3 = sel.reshape(N_EXPERT_GROUPS, gs, tt)
    i3 = lax.broadcasted_iota(jnp.int32, g3.shape, 1)
    m1 = jnp.max(g3, axis=1, keepdims=True)
    first = jnp.min(jnp.where(g3 == m1, i3, gs), axis=1, keepdims=True)
    m2 = jnp.max(jnp.where(i3 == first, -jnp.inf, g3), axis=1, keepdims=True)
    gscore = (m1 + m2).reshape(N_EXPERT_GROUPS, tt)
    gi = lax.broadcasted_iota(jnp.int32, gscore.shape, 0)
    gmask = jnp.zeros(gscore.shape, jnp.bool_)
    rem = gscore
    for _ in range(TOPK_GROUPS):
        mg = jnp.max(rem, axis=0, keepdims=True)
        pick = jnp.min(jnp.where(rem == mg, gi, N_EXPERT_GROUPS), axis=0, keepdims=True)
        hit = gi == pick
        gmask = gmask | hit
        rem = jnp.where(hit, -jnp.inf, rem)
    emask = jnp.broadcast_to(gmask.reshape(N_EXPERT_GROUPS, 1, tt), g3.shape).reshape(ne, tt)
    rem = jnp.where(emask, sel, -jnp.inf)
    ei = lax.broadcasted_iota(jnp.int32, (ne, tt), 0)
    idx, wts = [], []
    for _ in range(TOP_K):
        me = jnp.max(rem, axis=0, keepdims=True)
        pick = jnp.min(jnp.where(rem == me, ei, ne), axis=0, keepdims=True)
        hit = ei == pick
        idx.append(pick)
        wts.append(jnp.sum(jnp.where(hit, scores, 0.0), axis=0, keepdims=True))
        rem = jnp.where(hit, -jnp.inf, rem)
    w = jnp.concatenate(wts, axis=0)
    ti_ref[...] = jnp.concatenate(idx, axis=0)
    tw_ref[...] = w / jnp.sum(w, axis=0, keepdims=True) * ROUTED_SCALE


def _route(logits_t, router_bias):
    ne, t = logits_t.shape
    tt = _pick(t, (2176, 2048, 1024, 512, 256, 128))
    return pl.pallas_call(
        _route_kernel,
        grid=(t // tt,),
        in_specs=[pl.BlockSpec((ne, tt), lambda i: (0, i)),
                  pl.BlockSpec((ne, 1), lambda i: (0, 0))],
        out_specs=[pl.BlockSpec((TOP_K, tt), lambda i: (0, i)),
                   pl.BlockSpec((TOP_K, tt), lambda i: (0, i))],
        out_shape=[jax.ShapeDtypeStruct((TOP_K, t), jnp.int32),
                   jax.ShapeDtypeStruct((TOP_K, t), F32)],
        compiler_params=_cparams(("arbitrary",)),
        name="route",
    )(logits_t, router_bias.reshape(ne, 1))


def _moe_kernel(be_ref, r0_ref, n_ref, first_ref, a_ref,
                hf_hbm, w1_ref, w3_ref, w2_ref, y8_hbm,
                xbuf, obuf, w1b, w3b, w2b, gsem, ssem, *, n_tok):
    del be_ref
    b = pl.program_id(0)
    n = n_ref[b]
    r0 = r0_ref[b]

    @pl.when(b == 0)
    def _():
        xbuf[...] = jnp.zeros_like(xbuf)

    @pl.when(first_ref[b] == 1)
    def _():
        w1b[...] = w1_ref[0].astype(BF16)
        w3b[...] = w3_ref[0].astype(BF16)
        w2b[...] = w2_ref[0].astype(BF16)

    def gather_copy(r, tok):
        return pltpu.make_async_copy(hf_hbm.at[pl.ds(tok, 1)], xbuf.at[pl.ds(r, 1)], gsem)

    def scatter_copy(r, dst):
        return pltpu.make_async_copy(obuf.at[pl.ds(r, 1)], y8_hbm.at[pl.ds(dst, 1)], ssem)

    @pl.when(n > 0)
    def _():
        def g_start(r, c):
            gather_copy(r, a_ref[r0 + r] >> 3).start()
            return c

        def g_wait(r, c):
            gather_copy(0, 0).wait()
            return c

        lax.fori_loop(0, n, g_start, 0)
        lax.fori_loop(0, n, g_wait, 0)

        x = xbuf[...].astype(BF16)
        h = _silu(_bdot(x, w1b[...])) * _bdot(x, w3b[...])
        obuf[...] = _bdot(h.astype(BF16), w2b[...])

        def s_start(r, c):
            a = a_ref[r0 + r]
            scatter_copy(r, (a & 7) * n_tok + (a >> 3)).start()
            return c

        def s_wait(r, c):
            scatter_copy(0, 0).wait()
            return c

        lax.fori_loop(0, n, s_start, 0)
        lax.fori_loop(0, n, s_wait, 0)


def _moe(hf, a_sorted, blk_e, blk_r0, blk_n, blk_first, w1, w3, w2, tme):
    t, d = hf.shape
    ne, _, de = w1.shape
    nb = blk_e.shape[0]
    wmap = lambda b, be, r0, n, f, a: (be[b], 0, 0)
    grid_spec = pltpu.PrefetchScalarGridSpec(
        num_scalar_prefetch=5,
        grid=(nb,),
        in_specs=[pl.BlockSpec(memory_space=pl.ANY),
                  pl.BlockSpec((1, d, de), wmap),
                  pl.BlockSpec((1, d, de), wmap),
                  pl.BlockSpec((1, de, d), wmap)],
        out_specs=pl.BlockSpec(memory_space=pl.ANY),
        scratch_shapes=[pltpu.VMEM((tme, d), F32), pltpu.VMEM((tme, d), F32),
                        pltpu.VMEM((d, de), BF16), pltpu.VMEM((d, de), BF16), pltpu.VMEM((de, d), BF16),
                        pltpu.SemaphoreType.DMA(()), pltpu.SemaphoreType.DMA(())],
    )
    return pl.pallas_call(
        functools.partial(_moe_kernel, n_tok=t),
        grid_spec=grid_spec,
        out_shape=jax.ShapeDtypeStruct((TOP_K * t, d), F32),
        compiler_params=_cparams(("arbitrary",)),
        name="moe",
    )(blk_e, blk_r0, blk_n, blk_first, a_sorted, hf, w1, w3, w2)


def _dispatch_plan(topi_t, tme, ne):
    k, t = topi_t.shape
    a_cnt = k * t
    shift = int(np.ceil(np.log2(a_cnt)))
    assert ne << shift < 2 ** 31
    a_id = (jnp.arange(t, dtype=jnp.int32)[None, :] * k + jnp.arange(k, dtype=jnp.int32)[:, None])
    keys = (topi_t << shift) + a_id
    a_sorted = jnp.sort(keys.reshape(-1)) & ((1 << shift) - 1)
    counts = jnp.sum(topi_t.reshape(-1, 1) == jnp.arange(ne, dtype=jnp.int32)[None, :], axis=0,
                     dtype=jnp.int32)
    starts = jnp.cumsum(counts) - counts
    nblk = (counts + tme - 1) // tme
    blk_end = jnp.cumsum(nblk)
    blk_start = blk_end - nblk
    nb = a_cnt // tme + ne
    b = jnp.arange(nb, dtype=jnp.int32)
    valid = b < blk_end[-1]
    e_raw = jnp.minimum(jnp.searchsorted(blk_end, b, side="right"), ne - 1).astype(jnp.int32)
    e_last = e_raw[jnp.maximum(blk_end[-1] - 1, 0)]
    blk_e = jnp.where(valid, e_raw, e_last)
    j = b - blk_start[blk_e]
    blk_r0 = jnp.where(valid, starts[blk_e] + j * tme, 0)
    blk_n = jnp.where(valid, jnp.minimum(tme, counts[blk_e] - j * tme), 0)
    blk_first = (valid & (j == 0)).astype(jnp.int32)
    return a_sorted, blk_e, blk_r0.astype(jnp.int32), blk_n.astype(jnp.int32), blk_first


def _final_kernel(y8_ref, tw_ref, hf_ref, x1_ref, gf_ref, npost_ref, ws1_ref, ws3_ref, ws2_ref, o_ref):
    tm, d = hf_ref.shape
    tw = tw_ref[...]
    routed = y8_ref[0] * tw[:, 0:1]
    for k in range(1, TOP_K):
        routed = routed + y8_ref[k] * tw[:, k:k + 1]
    hb = hf_ref[...].astype(BF16)
    hs = _silu(_bdot(hb, ws1_ref[...])) * _bdot(hb, ws3_ref[...])
    shared = _bdot(hs.astype(BF16), ws2_ref[...])
    nm = _rms(routed + shared, npost_ref[...]).reshape(tm // CHUNK, CHUNK, d)
    o_ref[...] = (x1_ref[...].reshape(tm // CHUNK, CHUNK, d) + gf_ref[...] * nm).reshape(tm, d)


def _final(y8, topw, hf, x1, gate_f, npost, ws1, ws3, ws2):
    t, d = hf.shape
    tm = _pick(t, (128, 64))
    nc = tm // CHUNK
    row = lambda i: (i, 0)
    const = lambda i: (0, 0)
    return pl.pallas_call(
        _final_kernel,
        grid=(t // tm,),
        in_specs=[pl.BlockSpec((TOP_K, tm, d), lambda i: (0, i, 0)),
                  pl.BlockSpec((tm, TOP_K), row),
                  pl.BlockSpec((tm, d), row), pl.BlockSpec((tm, d), row),
                  pl.BlockSpec((nc, 1, d), lambda i: (i, 0, 0)),
                  pl.BlockSpec((1, d), const),
                  pl.BlockSpec(ws1.shape, const), pl.BlockSpec(ws3.shape, const),
                  pl.BlockSpec(ws2.shape, const)],
        out_specs=pl.BlockSpec((tm, d), row),
        out_shape=jax.ShapeDtypeStruct((t, d), F32),
        compiler_params=_cparams(("arbitrary",)),
        name="final",
    )(y8.reshape(TOP_K, t, d), topw, hf, x1, gate_f, npost, ws1, ws3, ws2)


def kernel(x_prompt, x_sample, cache_conv, state_ssm, cache_k, cache_v, c_prompt, c_sample,
           w_ada, b_ada, norm_pre_mix, norm_post_mix, norm_pre_ffn, norm_post_ffn,
           w_in, conv_w, conv_b, dt_bias, a_log, d_skip, gn_w, rel_bias, w_out,
           w_router, router_bias, w1, w3, w2, ws1, ws3, ws2):
    assert w_ada.shape[0] == 1, "single layer"
    bp, lp, d = x_prompt.shape
    bs, ls, _ = x_sample.shape
    assert bp == 1 and ls == CHUNK and lp % ATT_PAST == 0
    assert cache_k.shape[2] == ATT_PAST
    heads = a_log.shape[1]
    inner = heads * SSM_HEAD_DIM
    att_w = rel_bias.shape[1] * ATT_HEAD_DIM
    assert att_w == inner
    bcw = 2 * SSM_GROUPS * SSM_STATE
    ne = w_router.shape[2]
    tp, ts = bp * lp, bs * ls
    t = tp + ts
    nseq = bp + bs
    nch = t // CHUNK

    x_all = jnp.concatenate([x_prompt.reshape(tp, d), x_sample.reshape(ts, d)], axis=0)
    seq_np = np.concatenate([np.repeat(np.arange(bp), lp // CHUNK), bp + np.arange(bs)]).astype(np.int32)
    first_np = np.concatenate([[1], (seq_np[1:] != seq_np[:-1])]).astype(np.int32)
    seq_of_chunk, first_of_chunk = jnp.asarray(seq_np), jnp.asarray(first_np)

    c_all = jnp.concatenate([c_prompt, c_sample], axis=0)
    c_pad = jnp.pad(c_all, ((0, -nseq % 8), (0, 0)))
    mod = _ada(c_pad, w_ada[0], b_ada[0])[:nseq].reshape(nseq, 6, d)
    mod_c = mod[seq_of_chunk]
    shift_m, scale_m, gate_m, shift_f, scale_f, gate_f = [mod_c[:, i:i + 1, :] for i in range(6)]

    wi = w_in[0]
    o_z, o_x, o_bc = 0, inner, 2 * inner
    o_dt = inner + inner + bcw
    o_q = o_dt + heads
    o_k, o_v = o_q + att_w, o_q + 2 * att_w
    cols = lambda o, n: wi[:, o:o + n]
    w_main = jnp.concatenate([cols(o_z, inner), cols(o_x, inner), cols(o_q, att_w), cols(o_k, att_w),
                              cols(o_v, att_w), cols(o_bc, bcw)], axis=1).astype(BF16)
    w_dt = jnp.pad(cols(o_dt, heads), ((0, 0), (0, LANES - heads))).astype(BF16)
    proj, dt_raw = _inproj(x_all, scale_m, shift_m, norm_pre_mix, w_main, w_dt)
    c_x, c_q, c_k, c_v, c_bc = inner, 2 * inner, 3 * inner, 4 * inner, 5 * inner

    pad_rows = lambda a: jnp.pad(a, ((0, 0), (8 - (CONV_W - 1), 0), (0, 0)))
    pre = jnp.concatenate([jnp.zeros((bp, CONV_W - 1, inner + bcw), F32), cache_conv[0]], axis=0)
    pre_x, pre_bc = pad_rows(pre[:, :, :inner]), pad_rows(pre[:, :, inner:])
    h0 = jnp.concatenate([jnp.zeros((bp,) + state_ssm.shape[2:], F32), state_ssm[0]], axis=0)
    h0t = h0.transpose(0, 3, 1, 2).reshape(nseq, SSM_STATE, inner)
    lane_pad = lambda v: jnp.pad(v, (0, LANES - heads)).reshape(1, LANES)
    expand = (np.arange(LANES)[:, None] == (np.arange(inner)[None, :] // SSM_HEAD_DIM)).astype(np.float32)
    tri = np.tril(np.ones((CHUNK, CHUNK), np.float32))
    consts = (conv_w[0][:, :inner], conv_w[0][:, inner:],
              conv_b[0][:inner].reshape(1, inner), conv_b[0][inner:].reshape(1, bcw),
              lane_pad(dt_bias[0]), lane_pad(-jnp.exp(a_log[0])),
              jnp.repeat(d_skip[0], SSM_HEAD_DIM).reshape(1, inner), gn_w[0].reshape(1, inner),
              jnp.asarray(expand), jnp.asarray(tri))
    y_ssd, st_out = _ssd(proj, dt_raw, seq_of_chunk, first_of_chunk, pre_x, pre_bc, h0t, consts, inner)

    bias2 = _attn_bias(rel_bias[0])
    att_p = _attn_prompt(proj, bias2, tp, att_w)
    q_s = proj[tp:, c_q:c_q + att_w].reshape(bs, ls, att_w)
    k_s = proj[tp:, c_k:c_k + att_w].reshape(bs, ls, att_w)
    v_s = proj[tp:, c_v:c_v + att_w].reshape(bs, ls, att_w)
    zpad = jnp.zeros((bs, CHUNK, att_w), F32)
    q_pad = jnp.concatenate([zpad, q_s], axis=1).reshape(bs * QPAIR, att_w)
    k_win = jnp.concatenate([zpad, cache_k[0].reshape(bs, ATT_PAST, att_w), k_s], axis=1).reshape(bs * KWIN, att_w)
    v_win = jnp.concatenate([zpad, cache_v[0].reshape(bs, ATT_PAST, att_w), v_s], axis=1).reshape(bs * KWIN, att_w)
    att_s = _attn_sample(q_pad, k_win, v_win, bias2, bs, att_w)
    att_s = att_s.reshape(bs, QPAIR, att_w)[:, CHUNK:, :].reshape(ts, att_w)
    att = jnp.concatenate([att_p, att_s], axis=0)

    wo = w_out[0].astype(BF16)
    x1, hf, logits_t = _outproj(y_ssd, att, x_all, gate_m, scale_f, shift_f,
                                norm_post_mix, norm_pre_ffn, wo[:inner], wo[inner:], w_router[0].T)

    topi_t, topw_t = _route(logits_t, router_bias[0])
    tme = 256
    a_sorted, blk_e, blk_r0, blk_n, blk_first = _dispatch_plan(topi_t, tme, ne)
    y8 = _moe(hf, a_sorted, blk_e, blk_r0, blk_n, blk_first, w1[0], w3[0], w2[0], tme)
    y_all = _final(y8, topw_t.T, hf, x1, gate_f, norm_post_ffn,
                   ws1[0].astype(BF16), ws3[0].astype(BF16), ws2[0].astype(BF16))

    y_prompt = y_all[:tp].reshape(bp, lp, d)
    y_sample = y_all[tp:].reshape(bs, ls, d)
    tail = lambda rows: jnp.concatenate([rows[..., c_x:c_x + inner], rows[..., c_bc:c_bc + bcw]], axis=-1)
    conv_prompt = tail(proj[:tp].reshape(bp, lp, -1)[:, lp - (CONV_W - 1):, :])[None]
    conv_sample = tail(proj[tp:].reshape(bs, ls, -1)[:, ls - (CONV_W - 1):, :])[None]
    st = st_out.reshape(nseq, SSM_STATE, heads, SSM_HEAD_DIM).transpose(0, 2, 3, 1)
    ssm_prompt, ssm_sample = st[:bp][None], st[bp:][None]
    keep = min(ATT_PAST, lp)
    hd = (rel_bias.shape[1], ATT_HEAD_DIM)
    k_prompt = proj[:tp, c_k:c_k + att_w].reshape(bp, lp, *hd)[:, lp - keep:][None]
    v_prompt = proj[:tp, c_v:c_v + att_w].reshape(bp, lp, *hd)[:, lp - keep:][None]
    k_sample = k_s.reshape(bs, ls, *hd)[None]
    v_sample = v_s.reshape(bs, ls, *hd)[None]
    return (y_prompt, y_sample, conv_prompt, ssm_prompt, k_prompt, v_prompt,
            conv_sample, ssm_sample, k_sample, v_sample)
```

```python
import functools

import numpy as np
import jax
import jax.numpy as jnp
from jax import lax
from jax.experimental import pallas as pl
from jax.experimental.pallas import tpu as pltpu

F32 = jnp.float32
BF16 = jnp.bfloat16
I32 = jnp.int32
HIGHEST = lax.Precision.HIGHEST

CHUNK = 64
SSM_HEAD_DIM = 64
SSM_GROUPS = 2
SSM_STATE = 128
CONV_W = 4
ATT_HEAD_DIM = 64
LEFT_CHUNKS = 8
ATT_PAST = LEFT_CHUNKS * CHUNK
BAND = ATT_PAST + CHUNK
REL_CLIP = 128
TOP_K = 8
N_EXPERT_GROUPS = 8
TOPK_GROUPS = 4
ROUTED_SCALE = 2.5
EPS = 1e-6
NEG_BIG = -1e30

LANES = 128
SUBLANES = 8
PAIR = 2 * ATT_HEAD_DIM
QPAIR = 2 * CHUNK
KWIN = ATT_PAST + QPAIR
VMEM_LIMIT = 56 * 1024 * 1024
MOE_ROWS = 256
DMA_UNROLL = 8


def _cparams(sem):
    return pltpu.CompilerParams(dimension_semantics=sem, vmem_limit_bytes=VMEM_LIMIT)


def _pick(ns, cands):
    for c in cands:
        if all(n % c == 0 for n in ns):
            return c
    raise ValueError(f"no tile for {ns} in {cands}")


def _silu(x):
    return x * jax.nn.sigmoid(x)


def _rms(x, g):
    ms = jnp.mean(x * x, axis=-1, keepdims=True)
    return x * lax.rsqrt(ms + EPS) * g


def _bdot(a, b):
    return jnp.dot(a, b, preferred_element_type=F32)


def _store_token_tiles(ref, base, x):
    rows, d = x.shape
    nb = d // LANES
    for c in range(nb):
        ref[pl.ds(base * nb + c, rows, stride=nb), :] = x[:, c * LANES:(c + 1) * LANES]


def _load_token_tiles(ref, base, rows, nb):
    return [ref[pl.ds(base * nb + c, rows, stride=nb), :] for c in range(nb)]


def _split_rows(npt):
    first = lambda i, *_: (jnp.minimum(i, npt - 1), 0)
    second = lambda i, *_: (jnp.maximum(i - npt, 0), 0)
    return first, second


def _ada_kernel(c_ref, w_ref, b_ref, o_ref):
    a = _silu(c_ref[...])
    o_ref[...] = jnp.dot(a, w_ref[...], precision=HIGHEST, preferred_element_type=F32) + b_ref[...]


def _ada(c_pad, w_ada, b_ada):
    m, d = c_pad.shape
    n = w_ada.shape[1]
    tn = _pick((n,), (1024, 512, 256, 128))
    return pl.pallas_call(
        _ada_kernel,
        grid=(n // tn,),
        in_specs=[pl.BlockSpec((m, d), lambda j: (0, 0)),
                  pl.BlockSpec((d, tn), lambda j: (0, j)),
                  pl.BlockSpec((1, tn), lambda j: (0, j))],
        out_specs=pl.BlockSpec((m, tn), lambda j: (0, j)),
        out_shape=jax.ShapeDtypeStruct((m, n), F32),
        compiler_params=_cparams(("arbitrary",)),
        name="ada",
    )(c_pad, w_ada, b_ada.reshape(1, n))


def _inproj_kernel(xp_ref, xs_ref, sc_ref, sh_ref, g_ref, w_ref, wdt_ref, o_ref, dt_ref, hm_ref, *, npt):
    i = pl.program_id(0)

    def prep(x_ref):
        x = x_ref[...]
        tm, d = x.shape
        y = _rms(x, g_ref[...]).reshape(tm // CHUNK, CHUNK, d)
        h = (y * (1.0 + sc_ref[...]) + sh_ref[...]).reshape(tm, d).astype(BF16)
        hm_ref[...] = h
        dt_ref[...] = _bdot(h, wdt_ref[...])

    @pl.when(pl.program_id(1) == 0)
    def _():
        @pl.when(i < npt)
        def _():
            prep(xp_ref)

        @pl.when(i >= npt)
        def _():
            prep(xs_ref)

    o_ref[...] = _bdot(hm_ref[...], w_ref[...])


def _inproj(xp, xs, scale, shift, g, w_main, w_dt):
    tp, d = xp.shape
    ts = xs.shape[0]
    t = tp + ts
    n = w_main.shape[1]
    tm = _pick((tp, ts), (1024, 512, 256, 128, 64))
    tn = _pick((n,), (512, 256, 128))
    nc = tm // CHUNK
    first, second = _split_rows(tp // tm)
    return pl.pallas_call(
        functools.partial(_inproj_kernel, npt=tp // tm),
        grid=(t // tm, n // tn),
        in_specs=[pl.BlockSpec((tm, d), first),
                  pl.BlockSpec((tm, d), second),
                  pl.BlockSpec((nc, 1, d), lambda i, j: (i, 0, 0)),
                  pl.BlockSpec((nc, 1, d), lambda i, j: (i, 0, 0)),
                  pl.BlockSpec((1, d), lambda i, j: (0, 0)),
                  pl.BlockSpec((d, tn), lambda i, j: (0, j)),
                  pl.BlockSpec((d, LANES), lambda i, j: (0, 0))],
        out_specs=[pl.BlockSpec((tm, tn), lambda i, j: (i, j)),
                   pl.BlockSpec((tm, LANES), lambda i, j: (i, 0))],
        out_shape=[jax.ShapeDtypeStruct((t, n), F32),
                   jax.ShapeDtypeStruct((t, LANES), F32)],
        scratch_shapes=[pltpu.VMEM((tm, d), BF16)],
        compiler_params=_cparams(("arbitrary", "arbitrary")),
        name="inproj",
    )(xp, xs, scale, shift, g, w_main, w_dt)


def _ssd_kernel(seq_ref, first_ref,
                z_ref, xs_ref, bc_ref, dt_ref, prex_ref, prebc_ref, h0_ref,
                cwx_ref, cwbc_ref, cbx_ref, cbbc_ref, dtb_ref, aneg_ref, dsk_ref, gnw_ref,
                e_ref, tri_ref,
                y_ref, st_out_ref,
                xpx_scr, xpbc_scr, st_scr):
    del seq_ref
    c = pl.program_id(0)
    inner = xs_ref.shape[1]
    gw = inner // SSM_GROUPS
    n = SSM_STATE
    pad = 8

    @pl.when(first_ref[c] == 1)
    def _():
        xpx_scr[0:pad, :] = prex_ref[0]
        xpbc_scr[0:pad, :] = prebc_ref[0]
        st_scr[...] = h0_ref[0]

    xpx_scr[pad:pad + CHUNK, :] = xs_ref[...]
    xpbc_scr[pad:pad + CHUNK, :] = bc_ref[...]

    def conv(xp, w_ref, b_ref):
        base = pad - (CONV_W - 1)
        acc = b_ref[...] + xp[base:base + CHUNK, :] * w_ref[0:1, :]
        for k in range(1, CONV_W):
            acc = acc + xp[base + k:base + k + CHUNK, :] * w_ref[k:k + 1, :]
        return _silu(acc)

    xs = conv(xpx_scr, cwx_ref, cbx_ref)
    bc = conv(xpbc_scr, cwbc_ref, cbbc_ref)
    xpx_scr[0:pad, :] = xpx_scr[CHUNK:CHUNK + pad, :]
    xpbc_scr[0:pad, :] = xpbc_scr[CHUNK:CHUNK + pad, :]

    dtv = dt_ref[...] + dtb_ref[...]
    dt = jnp.maximum(dtv, 0.0) + jnp.log(1.0 + jnp.exp(-jnp.abs(dtv)))
    da = dt * aneg_ref[...]
    acs = jnp.dot(tri_ref[...], da, precision=HIGHEST, preferred_element_type=F32)
    full = jnp.dot(jnp.concatenate([dt, acs], axis=0), e_ref[...],
                   precision=HIGHEST, preferred_element_type=F32)
    dtf = full[0:CHUNK]
    af = full[CHUNK:2 * CHUNK]

    row = lax.broadcasted_iota(I32, (CHUNK, inner), 0)
    lj = lax.broadcasted_iota(I32, (CHUNK, inner), 1) & (SSM_HEAD_DIM - 1)
    aj = jnp.sum(jnp.where(row == lj, af, 0.0), axis=0, keepdims=True)
    lmat = jnp.exp(jnp.where(row >= lj, af - aj, NEG_BIG))
    alast = af[CHUNK - 1:CHUNK, :]
    xdt = xs * dtf
    xw = xdt * jnp.exp(alast - af)
    cdec = jnp.exp(alast)
    eaf = jnp.exp(af)

    lane = lax.broadcasted_iota(I32, (CHUNK, PAIR), 1)
    st = st_scr[...]
    ydiag, yoff, stn = [], [], []
    for g in range(SSM_GROUPS):
        bg = bc[:, g * n:(g + 1) * n].astype(BF16)
        cg = bc[:, (SSM_GROUPS + g) * n:(SSM_GROUPS + g + 1) * n].astype(BF16)
        bb = jnp.concatenate([bg, bg], axis=0)
        cbb = lax.dot_general(cg, bb, (((1,), (1,)), ((), ())), preferred_element_type=F32)
        stg = st[:, g * gw:(g + 1) * gw]
        yoff.append(_bdot(cg, stg.astype(BF16)))
        for p in range(gw // PAIR):
            lo = g * gw + p * PAIR
            m = (cbb * lmat[:, lo:lo + PAIR]).astype(BF16)
            xd = xdt[:, lo:lo + PAIR]
            w = jnp.concatenate([jnp.where(lane < SSM_HEAD_DIM, xd, 0.0),
                                 jnp.where(lane >= SSM_HEAD_DIM, xd, 0.0)], axis=0).astype(BF16)
            ydiag.append(_bdot(m, w))
        upd = lax.dot_general(bg, xw[:, g * gw:(g + 1) * gw].astype(BF16),
                              (((0,), (0,)), ((), ())), preferred_element_type=F32)
        stn.append(cdec[:, g * gw:(g + 1) * gw] * stg + upd)

    y = jnp.concatenate(ydiag, axis=1) + jnp.concatenate(yoff, axis=1) * eaf + dsk_ref[...] * xs
    y = y * _silu(z_ref[...])
    y_ref[...] = _rms(y, gnw_ref[...])
    st_new = jnp.concatenate(stn, axis=1)
    st_scr[...] = st_new
    st_out_ref[0] = st_new


def _ssd(proj, dt_raw, seq_of_chunk, first_of_chunk, pre_x, pre_bc, h0t, consts, inner):
    t = proj.shape[0]
    nch = t // CHUNK
    nseq = h0t.shape[0]
    bcw = pre_bc.shape[-1]
    assert (5 * inner) % bcw == 0
    cmap = lambda blk: (lambda c, s, f: (c, blk))
    smap3 = lambda c, s, f: (s[c], 0, 0)
    const2 = lambda c, s, f: (0, 0)
    grid_spec = pltpu.PrefetchScalarGridSpec(
        num_scalar_prefetch=2,
        grid=(nch,),
        in_specs=[pl.BlockSpec((CHUNK, inner), cmap(0)),
                  pl.BlockSpec((CHUNK, inner), cmap(1)),
                  pl.BlockSpec((CHUNK, bcw), cmap((5 * inner) // bcw)),
                  pl.BlockSpec((CHUNK, LANES), lambda c, s, f: (c, 0)),
                  pl.BlockSpec((1, 8, inner), smap3),
                  pl.BlockSpec((1, 8, bcw), smap3),
                  pl.BlockSpec((1, SSM_STATE, inner), smap3)]
                 + [pl.BlockSpec(a.shape, const2) for a in consts],
        out_specs=[pl.BlockSpec((CHUNK, inner), lambda c, s, f: (c, 0)),
                   pl.BlockSpec((1, SSM_STATE, inner), smap3)],
        scratch_shapes=[pltpu.VMEM((CHUNK + 8, inner), F32),
                        pltpu.VMEM((CHUNK + 8, bcw), F32),
                        pltpu.VMEM((SSM_STATE, inner), F32)],
    )
    return pl.pallas_call(
        _ssd_kernel,
        grid_spec=grid_spec,
        out_shape=[jax.ShapeDtypeStruct((t, inner), F32),
                   jax.ShapeDtypeStruct((nseq, SSM_STATE, inner), F32)],
        compiler_params=_cparams(("arbitrary",)),
        name="ssd",
    )(seq_of_chunk, first_of_chunk, proj, proj, proj, dt_raw, pre_x, pre_bc, h0t, *consts)


def _attn_pairs(q_ref, kwin, vtwin, bias_ref, o_ref, n_steps, n_masked_fn, out_rows):
    n_pairs = q_ref.shape[1] // PAIR
    rowp = lax.broadcasted_iota(I32, (PAIR, QPAIR), 0)
    krow = lax.broadcasted_iota(I32, (KWIN, 2 * QPAIR), 0)

    for jj in range(n_steps):
        n_masked = n_masked_fn(jj)

        def body(hp, carry, jj=jj, n_masked=n_masked):
            lo = pl.multiple_of(hp * PAIR, PAIR)
            q = q_ref[jj * QPAIR:(jj + 1) * QPAIR, pl.ds(lo, PAIR)] * (ATT_HEAD_DIM ** -0.5)
            qt = q.T
            w = jnp.concatenate([jnp.where(rowp < ATT_HEAD_DIM, qt, 0.0),
                                 jnp.where(rowp >= ATT_HEAD_DIM, qt, 0.0)], axis=1).astype(BF16)
            kb = kwin[jj * QPAIR:jj * QPAIR + KWIN, pl.ds(lo, PAIR)]
            s = _bdot(kb, w) + bias_ref[hp]
            if n_masked is not None:
                s = jnp.where(krow < n_masked, NEG_BIG, s)
            mx = jnp.max(s, axis=0, keepdims=True)
            p = jnp.exp(s - mx)
            den = jnp.sum(p, axis=0, keepdims=True)
            vb = vtwin[pl.ds(lo, PAIR), jj * QPAIR:jj * QPAIR + KWIN]
            o2 = _bdot(vb, p.astype(BF16)) / den
            ot = jnp.where(rowp < ATT_HEAD_DIM, o2[:, 0:QPAIR], o2[:, QPAIR:2 * QPAIR])
            o_ref[jj * out_rows:(jj + 1) * out_rows, pl.ds(lo, PAIR)] = ot.T[0:out_rows]
            return carry

        lax.fori_loop(0, n_pairs, body, 0)


def _attn_prompt_kernel(q_ref, kp_ref, kc_ref, vp_ref, vc_ref, bias_ref, o_ref, kwin, vtwin):
    i = pl.program_id(0)
    tq = q_ref.shape[0]
    kwin[0:ATT_PAST, :] = kp_ref[...].astype(BF16)
    kwin[ATT_PAST:ATT_PAST + tq, :] = kc_ref[...].astype(BF16)
    vtwin[:, 0:ATT_PAST] = vp_ref[...].T.astype(BF16)
    vtwin[:, ATT_PAST:ATT_PAST + tq] = vc_ref[...].T.astype(BF16)
    _attn_pairs(q_ref, kwin, vtwin, bias_ref, o_ref, tq // QPAIR,
                lambda jj: ATT_PAST - jj * QPAIR - i * tq, QPAIR)


def _attn_prompt(proj, bias2, t_prompt, width):
    tq = ATT_PAST
    assert t_prompt % tq == 0
    qb, kb, vb = 2, 3, 4
    prev = lambda i: jnp.maximum(i - 1, 0)
    return pl.pallas_call(
        _attn_prompt_kernel,
        grid=(t_prompt // tq,),
        in_specs=[pl.BlockSpec((tq, width), lambda i: (i, qb)),
                  pl.BlockSpec((tq, width), lambda i: (prev(i), kb)),
                  pl.BlockSpec((tq, width), lambda i: (i, kb)),
                  pl.BlockSpec((tq, width), lambda i: (prev(i), vb)),
                  pl.BlockSpec((tq, width), lambda i: (i, vb)),
                  pl.BlockSpec(bias2.shape, lambda i: (0, 0, 0))],
        out_specs=pl.BlockSpec((tq, width), lambda i: (i, 0)),
        out_shape=jax.ShapeDtypeStruct((t_prompt, width), F32),
        scratch_shapes=[pltpu.VMEM((ATT_PAST + tq, width), BF16),
                        pltpu.VMEM((width, ATT_PAST + tq), BF16)],
        compiler_params=_cparams(("arbitrary",)),
        name="attn_prompt",
    )(proj, proj, proj, proj, proj, bias2)


def _attn_sample_kernel(q_ref, kc_ref, ks_ref, vc_ref, vs_ref, bias_ref, o_ref, qpad, kwin, vtwin):
    width = q_ref.shape[1]
    qpad[0:CHUNK, :] = q_ref[...]
    qpad[CHUNK:QPAIR, :] = jnp.zeros((CHUNK, width), F32)
    kwin[0:ATT_PAST, :] = kc_ref[...].astype(BF16)
    kwin[ATT_PAST:BAND, :] = ks_ref[...].astype(BF16)
    kwin[BAND:KWIN, :] = jnp.zeros((KWIN - BAND, width), BF16)
    vtwin[:, 0:ATT_PAST] = vc_ref[...].T.astype(BF16)
    vtwin[:, ATT_PAST:KWIN] = jnp.concatenate(
        [vs_ref[...], jnp.zeros((KWIN - BAND, width), F32)], axis=0).T.astype(BF16)
    _attn_pairs(qpad, kwin, vtwin, bias_ref, o_ref, 1, lambda jj: None, CHUNK)


def _attn_sample(proj, cache_k, cache_v, bias2, t_prompt, n_seq, width):
    qb, kb, vb = 2, 3, 4
    c0 = t_prompt // CHUNK
    return pl.pallas_call(
        _attn_sample_kernel,
        grid=(n_seq,),
        in_specs=[pl.BlockSpec((CHUNK, width), lambda b: (c0 + b, qb)),
                  pl.BlockSpec((ATT_PAST, width), lambda b: (b, 0)),
                  pl.BlockSpec((CHUNK, width), lambda b: (c0 + b, kb)),
                  pl.BlockSpec((ATT_PAST, width), lambda b: (b, 0)),
                  pl.BlockSpec((CHUNK, width), lambda b: (c0 + b, vb)),
                  pl.BlockSpec(bias2.shape, lambda b: (0, 0, 0))],
        out_specs=pl.BlockSpec((CHUNK, width), lambda b: (b, 0)),
        out_shape=jax.ShapeDtypeStruct((n_seq * CHUNK, width), F32),
        scratch_shapes=[pltpu.VMEM((QPAIR, width), F32),
                        pltpu.VMEM((KWIN, width), BF16),
                        pltpu.VMEM((width, KWIN), BF16)],
        compiler_params=_cparams(("arbitrary",)),
        name="attn_sample",
    )(proj, cache_k, proj, cache_v, proj, bias2)


def _attn_bias(table):
    h = table.shape[0]
    x = np.arange(BAND + CHUNK - 1)
    rel = np.clip(BAND - 1 - x, -REL_CLIP, REL_CLIP) + REL_CLIP
    u = table[:, rel]
    std = jnp.stack([u[:, CHUNK - 1 - i:CHUNK - 1 - i + BAND] for i in range(CHUNK)], axis=1)
    neg = jnp.full((h, CHUNK, KWIN - BAND), NEG_BIG, F32)
    b = jnp.stack([jnp.concatenate([std, neg], axis=2),
                   jnp.concatenate([neg, std], axis=2)],
                  axis=1)
    b = b.reshape(h // 2, 2, 2, CHUNK, KWIN).transpose(0, 4, 1, 2, 3)
    return b.reshape(h // 2, KWIN, 2 * QPAIR)


def _outproj_kernel(y_ref, ap_ref, as_ref, xp_ref, xs_ref, gm_ref, scf_ref, shf_ref, npost_ref, npre_ref,
                    wo1_ref, wo2_ref, wr_ref, x1_ref, ht_ref, hb_ref, lg_ref, *, npt):
    i = pl.program_id(0)

    def body(a_ref, x_ref):
        tm, d = x_ref.shape
        mix = _bdot(y_ref[...].astype(BF16), wo1_ref[...]) + _bdot(a_ref[...].astype(BF16), wo2_ref[...])
        nm = _rms(mix, npost_ref[...]).reshape(tm // CHUNK, CHUNK, d)
        x1 = x_ref[...].reshape(tm // CHUNK, CHUNK, d) + gm_ref[...] * nm
        x1_ref[...] = x1.reshape(tm, d)
        hn = _rms(x1, npre_ref[...])
        hf = (hn * (1.0 + scf_ref[...]) + shf_ref[...]).reshape(tm, d)
        _store_token_tiles(ht_ref, 0, hf)
        hb_ref[...] = hf.astype(BF16)
        lg_ref[...] = lax.dot_general(wr_ref[...], hf, (((1,), (1,)), ((), ())),
                                      precision=HIGHEST, preferred_element_type=F32)

    @pl.when(i < npt)
    def _():
        body(ap_ref, xp_ref)

    @pl.when(i >= npt)
    def _():
        body(as_ref, xs_ref)


def _outproj(y_ssd, att_p, att_s, xp, xs, gate_m, scale_f, shift_f, npost, npre, wo1, wo2, wr_t):
    tp, d = xp.shape
    ts = xs.shape[0]
    t = tp + ts
    inner = y_ssd.shape[1]
    ne = wr_t.shape[0]
    tm = _pick((tp, ts), (256, 128, 64))
    nc = tm // CHUNK
    first, second = _split_rows(tp // tm)
    row = lambda i: (i, 0)
    tab = lambda i: (i, 0, 0)
    const = lambda i: (0, 0)
    return pl.pallas_call(
        functools.partial(_outproj_kernel, npt=tp // tm),
        grid=(t // tm,),
        in_specs=[pl.BlockSpec((tm, inner), row),
                  pl.BlockSpec((tm, att_p.shape[1]), first), pl.BlockSpec((tm, att_s.shape[1]), second),
                  pl.BlockSpec((tm, d), first), pl.BlockSpec((tm, d), second),
                  pl.BlockSpec((nc, 1, d), tab), pl.BlockSpec((nc, 1, d), tab), pl.BlockSpec((nc, 1, d), tab),
                  pl.BlockSpec((1, d), const), pl.BlockSpec((1, d), const),
                  pl.BlockSpec(wo1.shape, const), pl.BlockSpec(wo2.shape, const),
                  pl.BlockSpec(wr_t.shape, const)],
        out_specs=[pl.BlockSpec((tm, d), row), pl.BlockSpec((tm * (d // LANES), LANES), row),
                   pl.BlockSpec((tm, d), row), pl.BlockSpec((ne, tm), lambda i: (0, i))],
        out_shape=[jax.ShapeDtypeStruct((t, d), F32), jax.ShapeDtypeStruct((t * (d // LANES), LANES), F32),
                   jax.ShapeDtypeStruct((t, d), BF16), jax.ShapeDtypeStruct((ne, t), F32)],
        compiler_params=_cparams(("arbitrary",)),
        name="outproj",
    )(y_ssd, att_p, att_s, xp, xs, gate_m, scale_f, shift_f, npost, npre, wo1, wo2, wr_t)


def _route_kernel(lg_ref, rb_ref, ti_ref, tw_ref):
    ne, tt = lg_ref.shape
    gs = ne // N_EXPERT_GROUPS
    scores = jax.nn.sigmoid(lg_ref[...])
    sel = scores + rb_ref[...]
    g3 = sel.reshape(N_EXPERT_GROUPS, gs, tt)
    i3 = lax.broadcasted_iota(I32, g3.shape, 1)
    m1 = jnp.max(g3, axis=1, keepdims=True)
    first = jnp.min(jnp.where(g3 == m1, i3, gs), axis=1, keepdims=True)
    m2 = jnp.max(jnp.where(i3 == first, -jnp.inf, g3), axis=1, keepdims=True)
    gscore = (m1 + m2).reshape(N_EXPERT_GROUPS, tt)
    gi = lax.broadcasted_iota(I32, gscore.shape, 0)
    gmask = jnp.zeros(gscore.shape, jnp.bool_)
    rem = gscore
    for _ in range(TOPK_GROUPS):
        mg = jnp.max(rem, axis=0, keepdims=True)
        pick = jnp.min(jnp.where(rem == mg, gi, N_EXPERT_GROUPS), axis=0, keepdims=True)
        hit = gi == pick
        gmask = gmask | hit
        rem = jnp.where(hit, -jnp.inf, rem)
    emask = jnp.broadcast_to(gmask.reshape(N_EXPERT_GROUPS, 1, tt), g3.shape).reshape(ne, tt)
    rem = jnp.where(emask, sel, -jnp.inf)
    ei = lax.broadcasted_iota(I32, (ne, tt), 0)
    idx, wts = [], []
    for _ in range(TOP_K):
        me = jnp.max(rem, axis=0, keepdims=True)
        pick = jnp.min(jnp.where(rem == me, ei, ne), axis=0, keepdims=True)
        hit = ei == pick
        idx.append(pick)
        wts.append(jnp.sum(jnp.where(hit, scores, 0.0), axis=0, keepdims=True))
        rem = jnp.where(hit, -jnp.inf, rem)
    w = jnp.concatenate(wts, axis=0)
    ti_ref[...] = jnp.concatenate(idx, axis=0)
    tw_ref[...] = w / jnp.sum(w, axis=0, keepdims=True) * ROUTED_SCALE


def _route(logits_t, router_bias):
    ne, t = logits_t.shape
    tt = _pick((t,), (2176, 2048, 1024, 512, 256, 128))
    return pl.pallas_call(
        _route_kernel,
        grid=(t // tt,),
        in_specs=[pl.BlockSpec((ne, tt), lambda i: (0, i)),
                  pl.BlockSpec((ne, 1), lambda i: (0, 0))],
        out_specs=[pl.BlockSpec((TOP_K, tt), lambda i: (0, i)),
                   pl.BlockSpec((TOP_K, tt), lambda i: (0, i))],
        out_shape=[jax.ShapeDtypeStruct((TOP_K, t), I32),
                   jax.ShapeDtypeStruct((TOP_K, t), F32)],
        compiler_params=_cparams(("arbitrary",)),
        name="route",
    )(logits_t, router_bias.reshape(ne, 1))


def _moe_kernel(be_ref, r0_ref, n_ref, first_ref, a_ref,
                ht_hbm, w1_ref, w3_ref, w2_ref, y8_hbm,
                xbuf, obuf, xs_scr, w1b, w3b, w2b, gsem, ssem, *, n_tok, tme):
    del be_ref
    b = pl.program_id(0)
    nb = pl.num_programs(0)
    slot = b % 2
    n = n_ref[b]
    nbk = xs_scr.shape[1] // LANES
    spare0 = TOP_K * n_tok

    def rows(ref, start, size):
        start = start if isinstance(start, int) else pl.multiple_of(start, nbk)
        return ref.at[pl.ds(start, size)]

    def tile(ref, row):
        return rows(ref, row * nbk, nbk)

    def buf(ref, s):
        return rows(ref, s * (tme * nbk), tme * nbk)

    def gather_row(s, r, tok):
        return pltpu.make_async_copy(tile(ht_hbm, tok), tile(xbuf, s * tme + r), gsem.at[s])

    def scatter_row(s, r, dst):
        return pltpu.make_async_copy(tile(obuf, s * tme + r), tile(y8_hbm, dst), ssem.at[s])

    def gather_all(s):
        return pltpu.make_async_copy(buf(ht_hbm, 0), buf(xbuf, s), gsem.at[s])

    def scatter_all(s):
        return pltpu.make_async_copy(buf(obuf, s), buf(y8_hbm, 0), ssem.at[s])

    def start_gather(blk, s):
        r0 = r0_ref[blk]

        def body(i, c):
            for u in range(DMA_UNROLL):
                r = i * DMA_UNROLL + u
                gather_row(s, r, a_ref[r0 + r] >> 3).start()
            return c

        lax.fori_loop(0, tme // DMA_UNROLL, body, 0)

    @pl.when(b == 0)
    def _():
        start_gather(0, 0)
        obuf[...] = jnp.zeros_like(obuf)
        for s in range(2):
            spare = pltpu.make_async_copy(buf(obuf, s), buf(y8_hbm, spare0 // tme + s), ssem.at[s])
            spare.start()
            spare.wait()

    nxt = jnp.minimum(b + 1, nb - 1)
    has_next = (b + 1 < nb) & (n_ref[nxt] > 0)

    @pl.when(has_next)
    def _():
        start_gather(nxt, 1 - slot)

    @pl.when(first_ref[b] == 1)
    def _():
        w1b[...] = w1_ref[0].astype(BF16)
        w3b[...] = w3_ref[0].astype(BF16)
        w2b[...] = w2_ref[0].astype(BF16)

    @pl.when(n > 0)
    def _():
        gather_all(slot).wait()

        @pl.when(b >= 2)
        def _():
            scatter_all(slot).wait()

        for c, blk in enumerate(_load_token_tiles(xbuf, slot * tme, tme, nbk)):
            xs_scr[:, c * LANES:(c + 1) * LANES] = blk.astype(BF16)
        x = xs_scr[...]
        h = (_silu(_bdot(x, w1b[...])) * _bdot(x, w3b[...])).astype(BF16)
        _store_token_tiles(obuf, slot * tme, _bdot(h, w2b[...]))

        r0 = r0_ref[b]

        def body(i, c):
            for u in range(DMA_UNROLL):
                r = i * DMA_UNROLL + u
                a = a_ref[r0 + r]
                dst = jnp.where(r < n, (a & (TOP_K - 1)) * n_tok + (a >> 3), spare0 + slot * tme + r)
                scatter_row(slot, r, dst).start()
            return c

        lax.fori_loop(0, tme // DMA_UNROLL, body, 0)

        @pl.when(jnp.logical_not(has_next))
        def _():
            scatter_all(slot).wait()

            @pl.when(b >= 1)
            def _():
                scatter_all(1 - slot).wait()


def _moe(ht, a_sorted, blk_e, blk_r0, blk_n, blk_first, w1, w3, w2):
    ne, d, de = w1.shape
    nbk = d // LANES
    t = ht.shape[0] // nbk
    nb = blk_e.shape[0]
    tme = MOE_ROWS
    assert (TOP_K * t) % tme == 0 and nbk % SUBLANES == 0
    wmap = lambda b, be, r0, n, f, a: (be[b], 0, 0)
    grid_spec = pltpu.PrefetchScalarGridSpec(
        num_scalar_prefetch=5,
        grid=(nb,),
        in_specs=[pl.BlockSpec(memory_space=pl.ANY),
                  pl.BlockSpec((1, d, de), wmap),
                  pl.BlockSpec((1, d, de), wmap),
                  pl.BlockSpec((1, de, d), wmap)],
        out_specs=pl.BlockSpec(memory_space=pl.ANY),
        scratch_shapes=[pltpu.VMEM((2 * tme * nbk, LANES), F32), pltpu.VMEM((2 * tme * nbk, LANES), F32),
                        pltpu.VMEM((tme, d), BF16),
                        pltpu.VMEM((d, de), BF16), pltpu.VMEM((d, de), BF16), pltpu.VMEM((de, d), BF16),
                        pltpu.SemaphoreType.DMA((2,)), pltpu.SemaphoreType.DMA((2,))],
    )
    return pl.pallas_call(
        functools.partial(_moe_kernel, n_tok=t, tme=tme),
        grid_spec=grid_spec,
        out_shape=jax.ShapeDtypeStruct(((TOP_K * t + 2 * tme) * nbk, LANES), F32),
        compiler_params=_cparams(("arbitrary",)),
        name="moe",
    )(blk_e, blk_r0, blk_n, blk_first, a_sorted, ht, w1, w3, w2)


def _dispatch_plan(topi_t, tme, ne):
    k, t = topi_t.shape
    assert k == TOP_K
    a_cnt = k * t
    shift = int(np.ceil(np.log2(a_cnt)))
    assert ne << shift < 2 ** 31
    a_id = jnp.arange(t, dtype=I32)[None, :] * k + jnp.arange(k, dtype=I32)[:, None]
    keys = (topi_t << shift) + a_id
    a_sorted = jnp.sort(keys.reshape(-1)) & ((1 << shift) - 1)
    a_sorted = jnp.concatenate([a_sorted, jnp.zeros((tme,), I32)])
    counts = jnp.sum(topi_t.reshape(-1, 1) == jnp.arange(ne, dtype=I32)[None, :], axis=0, dtype=I32)
    starts = jnp.cumsum(counts) - counts
    nblk = (counts + tme - 1) // tme
    blk_end = jnp.cumsum(nblk)
    nb = a_cnt // tme + ne
    b = jnp.arange(nb, dtype=I32)
    valid = b < blk_end[-1]
    e_raw = jnp.minimum(jnp.sum(blk_end[None, :] <= b[:, None], axis=1, dtype=I32), ne - 1)
    onehot = (e_raw[:, None] == jnp.arange(ne, dtype=I32)[None, :]).astype(I32)
    pick = lambda v: jnp.sum(onehot * v[None, :], axis=1)
    j = b - pick(blk_end - nblk)
    blk_r0 = jnp.where(valid, pick(starts) + j * tme, 0)
    blk_n = jnp.where(valid, jnp.minimum(tme, pick(counts) - j * tme), 0)
    blk_first = (valid & (j == 0)).astype(I32)
    e_last = jnp.max(jnp.where(valid, e_raw, 0))
    blk_e = jnp.where(valid, e_raw, e_last)
    return a_sorted, blk_e, blk_r0, blk_n, blk_first


def _final_kernel(*refs, npt):
    y8_refs = refs[:TOP_K]
    tw_ref, hb_ref, x1_ref, gf_ref, npost_ref, ws1_ref, ws3_ref, ws2_ref, op_ref, os_ref = refs[TOP_K:]
    i = pl.program_id(0)
    tm, d = x1_ref.shape
    nbk = d // LANES
    tw = tw_ref[...]
    routed = None
    for k in range(TOP_K):
        wk = jnp.broadcast_to(tw[:, k:k + 1], (tm, LANES))
        part = [blk * wk for blk in _load_token_tiles(y8_refs[k], 0, tm, nbk)]
        routed = part if routed is None else [a + p for a, p in zip(routed, part)]
    hb = hb_ref[...]
    shared = _bdot((_silu(_bdot(hb, ws1_ref[...])) * _bdot(hb, ws3_ref[...])).astype(BF16), ws2_ref[...])
    nm = _rms(jnp.concatenate(routed, axis=1) + shared, npost_ref[...]).reshape(tm // CHUNK, CHUNK, d)
    y = (x1_ref[...].reshape(tm // CHUNK, CHUNK, d) + gf_ref[...] * nm).reshape(tm, d)

    @pl.when(i < npt)
    def _():
        op_ref[...] = y

    @pl.when(i >= npt)
    def _():
        os_ref[...] = y


def _final(y8, topw, hb, x1, gate_f, npost, ws1, ws3, ws2, tp):
    t, d = x1.shape
    ts = t - tp
    nbk = d // LANES
    tm = _pick((tp, ts), (128, 64))
    nc = tm // CHUNK
    nt = t // tm
    first, second = _split_rows(tp // tm)
    row = lambda i: (i, 0)
    const = lambda i: (0, 0)
    planes = [pl.BlockSpec((tm * nbk, LANES), functools.partial(lambda i, k: (k * nt + i, 0), k=k))
              for k in range(TOP_K)]
    return pl.pallas_call(
        functools.partial(_final_kernel, npt=tp // tm),
        grid=(nt,),
        in_specs=planes + [pl.BlockSpec((tm, TOP_K), row),
                           pl.BlockSpec((tm, d), row), pl.BlockSpec((tm, d), row),
                           pl.BlockSpec((nc, 1, d), lambda i: (i, 0, 0)),
                           pl.BlockSpec((1, d), const),
                           pl.BlockSpec(ws1.shape, const), pl.BlockSpec(ws3.shape, const),
                           pl.BlockSpec(ws2.shape, const)],
        out_specs=[pl.BlockSpec((tm, d), first), pl.BlockSpec((tm, d), second)],
        out_shape=[jax.ShapeDtypeStruct((tp, d), F32), jax.ShapeDtypeStruct((ts, d), F32)],
        compiler_params=_cparams(("arbitrary",)),
        name="final",
    )(*([y8] * TOP_K), topw, hb, x1, gate_f, npost, ws1, ws3, ws2)


def kernel(x_prompt, x_sample, cache_conv, state_ssm, cache_k, cache_v, c_prompt, c_sample,
           w_ada, b_ada, norm_pre_mix, norm_post_mix, norm_pre_ffn, norm_post_ffn,
           w_in, conv_w, conv_b, dt_bias, a_log, d_skip, gn_w, rel_bias, w_out,
           w_router, router_bias, w1, w3, w2, ws1, ws3, ws2):
    assert w_ada.shape[0] == 1, "single layer"
    bp, lp, d = x_prompt.shape
    bs, ls, _ = x_sample.shape
    assert bp == 1 and ls == CHUNK and lp % ATT_PAST == 0
    assert cache_k.shape[2] == ATT_PAST
    heads = a_log.shape[1]
    inner = heads * SSM_HEAD_DIM
    att_w = rel_bias.shape[1] * ATT_HEAD_DIM
    assert att_w == inner
    bcw = 2 * SSM_GROUPS * SSM_STATE
    ne = w_router.shape[2]
    tp, ts = bp * lp, bs * ls
    nseq = bp + bs

    xp, xs = x_prompt.reshape(tp, d), x_sample.reshape(ts, d)
    seq_np = np.concatenate([np.repeat(np.arange(bp), lp // CHUNK), bp + np.arange(bs)]).astype(np.int32)
    first_np = np.concatenate([[1], (seq_np[1:] != seq_np[:-1])]).astype(np.int32)
    seq_of_chunk, first_of_chunk = jnp.asarray(seq_np), jnp.asarray(first_np)

    c_all = jnp.concatenate([c_prompt, c_sample], axis=0)
    c_pad = jnp.pad(c_all, ((0, -nseq % 8), (0, 0)))
    mod = _ada(c_pad, w_ada[0], b_ada[0])[:nseq].reshape(nseq, 6, d)
    mod_c = mod[seq_of_chunk]
    shift_m, scale_m, gate_m, shift_f, scale_f, gate_f = [mod_c[:, i:i + 1, :] for i in range(6)]

    wi = w_in[0]
    o_z, o_x, o_bc = 0, inner, 2 * inner
    o_dt = inner + inner + bcw
    o_q = o_dt + heads
    o_k, o_v = o_q + att_w, o_q + 2 * att_w
    cols = lambda o, n: wi[:, o:o + n]
    w_main = jnp.concatenate([cols(o_z, inner), cols(o_x, inner), cols(o_q, att_w), cols(o_k, att_w),
                              cols(o_v, att_w), cols(o_bc, bcw)], axis=1).astype(BF16)
    w_dt = jnp.pad(cols(o_dt, heads), ((0, 0), (0, LANES - heads))).astype(BF16)
    proj, dt_raw = _inproj(xp, xs, scale_m, shift_m, norm_pre_mix, w_main, w_dt)
    c_x, c_k, c_v, c_bc = inner, 3 * inner, 4 * inner, 5 * inner

    pad_rows = lambda a: jnp.pad(a, ((0, 0), (8 - (CONV_W - 1), 0), (0, 0)))
    pre = jnp.concatenate([jnp.zeros((bp, CONV_W - 1, inner + bcw), F32), cache_conv[0]], axis=0)
    pre_x, pre_bc = pad_rows(pre[:, :, :inner]), pad_rows(pre[:, :, inner:])
    h0 = jnp.concatenate([jnp.zeros((bp,) + state_ssm.shape[2:], F32), state_ssm[0]], axis=0)
    h0t = h0.transpose(0, 3, 1, 2).reshape(nseq, SSM_STATE, inner)
    lane_pad = lambda v: jnp.pad(v, (0, LANES - heads)).reshape(1, LANES)
    expand = (np.arange(LANES)[:, None] == (np.arange(inner)[None, :] // SSM_HEAD_DIM)).astype(np.float32)
    tri = np.tril(np.ones((CHUNK, CHUNK), np.float32))
    consts = (conv_w[0][:, :inner], conv_w[0][:, inner:],
              conv_b[0][:inner].reshape(1, inner), conv_b[0][inner:].reshape(1, bcw),
              lane_pad(dt_bias[0]), lane_pad(-jnp.exp(a_log[0])),
              jnp.repeat(d_skip[0], SSM_HEAD_DIM).reshape(1, inner), gn_w[0].reshape(1, inner),
              jnp.asarray(expand), jnp.asarray(tri))
    y_ssd, st_out = _ssd(proj, dt_raw, seq_of_chunk, first_of_chunk, pre_x, pre_bc, h0t, consts, inner)

    bias2 = _attn_bias(rel_bias[0])
    att_p = _attn_prompt(proj, bias2, tp, att_w)
    att_s = _attn_sample(proj, cache_k[0].reshape(bs * ATT_PAST, att_w), cache_v[0].reshape(bs * ATT_PAST, att_w),
                         bias2, tp, bs, att_w)

    wo = w_out[0].astype(BF16)
    x1, ht, hb, logits_t = _outproj(y_ssd, att_p, att_s, xp, xs, gate_m, scale_f, shift_f,
                                norm_post_mix, norm_pre_ffn, wo[:inner], wo[inner:], w_router[0].T)

    topi_t, topw_t = _route(logits_t, router_bias[0])
    a_sorted, blk_e, blk_r0, blk_n, blk_first = _dispatch_plan(topi_t, MOE_ROWS, ne)
    y8 = _moe(ht, a_sorted, blk_e, blk_r0, blk_n, blk_first, w1[0], w3[0], w2[0])
    y_p, y_s = _final(y8, topw_t.T, hb, x1, gate_f, norm_post_ffn,
                      ws1[0].astype(BF16), ws3[0].astype(BF16), ws2[0].astype(BF16), tp)

    tail = lambda rows: jnp.concatenate([rows[..., c_x:c_x + inner], rows[..., c_bc:c_bc + bcw]], axis=-1)
    conv_prompt = tail(proj[:tp].reshape(bp, lp, -1)[:, lp - (CONV_W - 1):, :])[None]
    conv_sample = tail(proj[tp:].reshape(bs, ls, -1)[:, ls - (CONV_W - 1):, :])[None]
    st = st_out.reshape(nseq, SSM_STATE, heads, SSM_HEAD_DIM).transpose(0, 2, 3, 1)
    keep = min(ATT_PAST, lp)
    hd = (rel_bias.shape[1], ATT_HEAD_DIM)
    kv = lambda c0, r0, r1, b, l: proj[r0:r1, c0:c0 + att_w].reshape(b, l, *hd)[None]
    return (y_p.reshape(bp, lp, d), y_s.reshape(bs, ls, d),
            conv_prompt, st[:bp][None], kv(c_k, tp - keep, tp, bp, keep), kv(c_v, tp - keep, tp, bp, keep),
            conv_sample, st[bp:][None], kv(c_k, tp, tp + ts, bs, ls), kv(c_v, tp, tp + ts, bs, ls))
```

```python
import functools

import numpy as np
import jax
import jax.numpy as jnp
from jax import lax
from jax.experimental import pallas as pl
from jax.experimental.pallas import tpu as pltpu

F32 = jnp.float32
BF16 = jnp.bfloat16
I32 = jnp.int32
HIGHEST = lax.Precision.HIGHEST

CHUNK = 64
SSM_HEAD_DIM = 64
SSM_GROUPS = 2
SSM_STATE = 128
CONV_W = 4
ATT_HEAD_DIM = 64
LEFT_CHUNKS = 8
ATT_PAST = LEFT_CHUNKS * CHUNK
BAND = ATT_PAST + CHUNK
REL_CLIP = 128
TOP_K = 8
N_EXPERT_GROUPS = 8
TOPK_GROUPS = 4
ROUTED_SCALE = 2.5
EPS = 1e-6
NEG_BIG = -1e30

LANES = 128
SUBLANES = 8
PAIR = 2 * ATT_HEAD_DIM
QPAIR = 2 * CHUNK
KWIN = ATT_PAST + QPAIR
VMEM_LIMIT = 56 * 1024 * 1024
MOE_ROWS = 256


def _cparams(sem):
    return pltpu.CompilerParams(dimension_semantics=sem, vmem_limit_bytes=VMEM_LIMIT)


def _pick(ns, cands):
    for c in cands:
        if all(n % c == 0 for n in ns):
            return c
    raise ValueError(f"no tile for {ns} in {cands}")


def _silu(x):
    return x * jax.nn.sigmoid(x)


def _rms(x, g):
    ms = jnp.mean(x * x, axis=-1, keepdims=True)
    return x * lax.rsqrt(ms + EPS) * g


def _bdot(a, b):
    return jnp.dot(a, b, preferred_element_type=F32)


def _store_token_tiles(ref, base, x):
    rows, d = x.shape
    nb = d // LANES
    for c in range(nb):
        ref[pl.ds(base * nb + c, rows, stride=nb), :] = x[:, c * LANES:(c + 1) * LANES]


def _load_token_tiles(ref, base, rows, nb):
    return [ref[pl.ds(base * nb + c, rows, stride=nb), :] for c in range(nb)]


def _split_rows(npt):
    first = lambda i, *_: (jnp.minimum(i, npt - 1), 0)
    second = lambda i, *_: (jnp.maximum(i - npt, 0), 0)
    return first, second


def _ada_kernel(c_ref, w_ref, b_ref, o_ref):
    a = _silu(c_ref[...])
    o_ref[...] = jnp.dot(a, w_ref[...], precision=HIGHEST, preferred_element_type=F32) + b_ref[...]


def _ada(c_pad, w_ada, b_ada):
    m, d = c_pad.shape
    n = w_ada.shape[1]
    tn = _pick((n,), (1024, 512, 256, 128))
    return pl.pallas_call(
        _ada_kernel,
        grid=(n // tn,),
        in_specs=[pl.BlockSpec((m, d), lambda j: (0, 0)),
                  pl.BlockSpec((d, tn), lambda j: (0, j)),
                  pl.BlockSpec((1, tn), lambda j: (0, j))],
        out_specs=pl.BlockSpec((m, tn), lambda j: (0, j)),
        out_shape=jax.ShapeDtypeStruct((m, n), F32),
        compiler_params=_cparams(("arbitrary",)),
        name="ada",
    )(c_pad, w_ada, b_ada.reshape(1, n))


def _inproj_kernel(xp_ref, xs_ref, sc_ref, sh_ref, g_ref, w_ref, wdt_ref, o_ref, dt_ref, hm_ref, *, npt):
    i = pl.program_id(0)

    def prep(x_ref):
        x = x_ref[...]
        tm, d = x.shape
        y = _rms(x, g_ref[...]).reshape(tm // CHUNK, CHUNK, d)
        h = (y * (1.0 + sc_ref[...]) + sh_ref[...]).reshape(tm, d).astype(BF16)
        hm_ref[...] = h
        dt_ref[...] = _bdot(h, wdt_ref[...])

    @pl.when(pl.program_id(1) == 0)
    def _():
        @pl.when(i < npt)
        def _():
            prep(xp_ref)

        @pl.when(i >= npt)
        def _():
            prep(xs_ref)

    o_ref[...] = _bdot(hm_ref[...], w_ref[...])


def _inproj(xp, xs, scale, shift, g, w_main, w_dt):
    tp, d = xp.shape
    ts = xs.shape[0]
    t = tp + ts
    n = w_main.shape[1]
    tm = _pick((tp, ts), (1024, 512, 256, 128, 64))
    tn = _pick((n,), (512, 256, 128))
    nc = tm // CHUNK
    first, second = _split_rows(tp // tm)
    return pl.pallas_call(
        functools.partial(_inproj_kernel, npt=tp // tm),
        grid=(t // tm, n // tn),
        in_specs=[pl.BlockSpec((tm, d), first),
                  pl.BlockSpec((tm, d), second),
                  pl.BlockSpec((nc, 1, d), lambda i, j: (i, 0, 0)),
                  pl.BlockSpec((nc, 1, d), lambda i, j: (i, 0, 0)),
                  pl.BlockSpec((1, d), lambda i, j: (0, 0)),
                  pl.BlockSpec((d, tn), lambda i, j: (0, j)),
                  pl.BlockSpec((d, LANES), lambda i, j: (0, 0))],
        out_specs=[pl.BlockSpec((tm, tn), lambda i, j: (i, j)),
                   pl.BlockSpec((tm, LANES), lambda i, j: (i, 0))],
        out_shape=[jax.ShapeDtypeStruct((t, n), F32),
                   jax.ShapeDtypeStruct((t, LANES), F32)],
        scratch_shapes=[pltpu.VMEM((tm, d), BF16)],
        compiler_params=_cparams(("arbitrary", "arbitrary")),
        name="inproj",
    )(xp, xs, scale, shift, g, w_main, w_dt)


def _ssd_kernel(seq_ref, first_ref,
                z_ref, xs_ref, bc_ref, dt_ref, prex_ref, prebc_ref, h0_ref,
                cwx_ref, cwbc_ref, cbx_ref, cbbc_ref, dtb_ref, aneg_ref, dsk_ref, gnw_ref,
                e_ref, tri_ref,
                y_ref, st_out_ref,
                xpx_scr, xpbc_scr, st_scr):
    del seq_ref
    c = pl.program_id(0)
    inner = xs_ref.shape[1]
    gw = inner // SSM_GROUPS
    n = SSM_STATE
    pad = 8

    @pl.when(first_ref[c] == 1)
    def _():
        xpx_scr[0:pad, :] = prex_ref[0]
        xpbc_scr[0:pad, :] = prebc_ref[0]
        st_scr[...] = h0_ref[0]

    xpx_scr[pad:pad + CHUNK, :] = xs_ref[...]
    xpbc_scr[pad:pad + CHUNK, :] = bc_ref[...]

    def conv(xp, w_ref, b_ref):
        base = pad - (CONV_W - 1)
        acc = b_ref[...] + xp[base:base + CHUNK, :] * w_ref[0:1, :]
        for k in range(1, CONV_W):
            acc = acc + xp[base + k:base + k + CHUNK, :] * w_ref[k:k + 1, :]
        return _silu(acc)

    xs = conv(xpx_scr, cwx_ref, cbx_ref)
    bc = conv(xpbc_scr, cwbc_ref, cbbc_ref)
    xpx_scr[0:pad, :] = xpx_scr[CHUNK:CHUNK + pad, :]
    xpbc_scr[0:pad, :] = xpbc_scr[CHUNK:CHUNK + pad, :]

    dtv = dt_ref[...] + dtb_ref[...]
    dt = jnp.maximum(dtv, 0.0) + jnp.log(1.0 + jnp.exp(-jnp.abs(dtv)))
    da = dt * aneg_ref[...]
    acs = jnp.dot(tri_ref[...], da, precision=HIGHEST, preferred_element_type=F32)
    full = jnp.dot(jnp.concatenate([dt, acs], axis=0), e_ref[...],
                   precision=HIGHEST, preferred_element_type=F32)
    dtf = full[0:CHUNK]
    af = full[CHUNK:2 * CHUNK]

    row = lax.broadcasted_iota(I32, (CHUNK, inner), 0)
    lj = lax.broadcasted_iota(I32, (CHUNK, inner), 1) & (SSM_HEAD_DIM - 1)
    aj = jnp.sum(jnp.where(row == lj, af, 0.0), axis=0, keepdims=True)
    lmat = jnp.exp(jnp.where(row >= lj, af - aj, NEG_BIG))
    alast = af[CHUNK - 1:CHUNK, :]
    xdt = xs * dtf
    xw = xdt * jnp.exp(alast - af)
    cdec = jnp.exp(alast)
    eaf = jnp.exp(af)

    lane = lax.broadcasted_iota(I32, (CHUNK, PAIR), 1)
    st = st_scr[...]
    ydiag, yoff, stn = [], [], []
    for g in range(SSM_GROUPS):
        bg = bc[:, g * n:(g + 1) * n].astype(BF16)
        cg = bc[:, (SSM_GROUPS + g) * n:(SSM_GROUPS + g + 1) * n].astype(BF16)
        bb = jnp.concatenate([bg, bg], axis=0)
        cbb = lax.dot_general(cg, bb, (((1,), (1,)), ((), ())), preferred_element_type=F32)
        stg = st[:, g * gw:(g + 1) * gw]
        yoff.append(_bdot(cg, stg.astype(BF16)))
        for p in range(gw // PAIR):
            lo = g * gw + p * PAIR
            m = (cbb * lmat[:, lo:lo + PAIR]).astype(BF16)
            xd = xdt[:, lo:lo + PAIR]
            w = jnp.concatenate([jnp.where(lane < SSM_HEAD_DIM, xd, 0.0),
                                 jnp.where(lane >= SSM_HEAD_DIM, xd, 0.0)], axis=0).astype(BF16)
            ydiag.append(_bdot(m, w))
        upd = lax.dot_general(bg, xw[:, g * gw:(g + 1) * gw].astype(BF16),
                              (((0,), (0,)), ((), ())), preferred_element_type=F32)
        stn.append(cdec[:, g * gw:(g + 1) * gw] * stg + upd)

    y = jnp.concatenate(ydiag, axis=1) + jnp.concatenate(yoff, axis=1) * eaf + dsk_ref[...] * xs
    y = y * _silu(z_ref[...])
    y_ref[...] = _rms(y, gnw_ref[...])
    st_new = jnp.concatenate(stn, axis=1)
    st_scr[...] = st_new
    st_out_ref[0] = st_new


def _ssd(proj, dt_raw, seq_of_chunk, first_of_chunk, pre_x, pre_bc, h0t, consts, inner):
    t = proj.shape[0]
    nch = t // CHUNK
    nseq = h0t.shape[0]
    bcw = pre_bc.shape[-1]
    assert (5 * inner) % bcw == 0
    cmap = lambda blk: (lambda c, s, f: (c, blk))
    smap3 = lambda c, s, f: (s[c], 0, 0)
    const2 = lambda c, s, f: (0, 0)
    grid_spec = pltpu.PrefetchScalarGridSpec(
        num_scalar_prefetch=2,
        grid=(nch,),
        in_specs=[pl.BlockSpec((CHUNK, inner), cmap(0)),
                  pl.BlockSpec((CHUNK, inner), cmap(1)),
                  pl.BlockSpec((CHUNK, bcw), cmap((5 * inner) // bcw)),
                  pl.BlockSpec((CHUNK, LANES), lambda c, s, f: (c, 0)),
                  pl.BlockSpec((1, 8, inner), smap3),
                  pl.BlockSpec((1, 8, bcw), smap3),
                  pl.BlockSpec((1, SSM_STATE, inner), smap3)]
                 + [pl.BlockSpec(a.shape, const2) for a in consts],
        out_specs=[pl.BlockSpec((CHUNK, inner), lambda c, s, f: (c, 0)),
                   pl.BlockSpec((1, SSM_STATE, inner), smap3)],
        scratch_shapes=[pltpu.VMEM((CHUNK + 8, inner), F32),
                        pltpu.VMEM((CHUNK + 8, bcw), F32),
                        pltpu.VMEM((SSM_STATE, inner), F32)],
    )
    return pl.pallas_call(
        _ssd_kernel,
        grid_spec=grid_spec,
        out_shape=[jax.ShapeDtypeStruct((t, inner), F32),
                   jax.ShapeDtypeStruct((nseq, SSM_STATE, inner), F32)],
        compiler_params=_cparams(("arbitrary",)),
        name="ssd",
    )(seq_of_chunk, first_of_chunk, proj, proj, proj, dt_raw, pre_x, pre_bc, h0t, *consts)


def _attn_pairs(q_ref, kwin, vtwin, bias_ref, o_ref, n_steps, n_masked_fn, out_rows):
    n_pairs = q_ref.shape[1] // PAIR
    rowp = lax.broadcasted_iota(I32, (PAIR, QPAIR), 0)
    krow = lax.broadcasted_iota(I32, (KWIN, 2 * QPAIR), 0)

    for jj in range(n_steps):
        n_masked = n_masked_fn(jj)

        def body(hp, carry, jj=jj, n_masked=n_masked):
            lo = pl.multiple_of(hp * PAIR, PAIR)
            q = q_ref[jj * QPAIR:(jj + 1) * QPAIR, pl.ds(lo, PAIR)] * (ATT_HEAD_DIM ** -0.5)
            qt = q.T
            w = jnp.concatenate([jnp.where(rowp < ATT_HEAD_DIM, qt, 0.0),
                                 jnp.where(rowp >= ATT_HEAD_DIM, qt, 0.0)], axis=1).astype(BF16)
            kb = kwin[jj * QPAIR:jj * QPAIR + KWIN, pl.ds(lo, PAIR)]
            s = _bdot(kb, w) + bias_ref[hp]
            if n_masked is not None:
                s = jnp.where(krow < n_masked, NEG_BIG, s)
            mx = jnp.max(s, axis=0, keepdims=True)
            p = jnp.exp(s - mx)
            den = jnp.sum(p, axis=0, keepdims=True)
            vb = vtwin[pl.ds(lo, PAIR), jj * QPAIR:jj * QPAIR + KWIN]
            o2 = _bdot(vb, p.astype(BF16)) / den
            ot = jnp.where(rowp < ATT_HEAD_DIM, o2[:, 0:QPAIR], o2[:, QPAIR:2 * QPAIR])
            o_ref[jj * out_rows:(jj + 1) * out_rows, pl.ds(lo, PAIR)] = ot.T[0:out_rows]
            return carry

        lax.fori_loop(0, n_pairs, body, 0)


def _attn_prompt_kernel(q_ref, kp_ref, kc_ref, vp_ref, vc_ref, bias_ref, o_ref, kwin, vtwin):
    i = pl.program_id(0)
    tq = q_ref.shape[0]
    kwin[0:ATT_PAST, :] = kp_ref[...].astype(BF16)
    kwin[ATT_PAST:ATT_PAST + tq, :] = kc_ref[...].astype(BF16)
    vtwin[:, 0:ATT_PAST] = vp_ref[...].T.astype(BF16)
    vtwin[:, ATT_PAST:ATT_PAST + tq] = vc_ref[...].T.astype(BF16)
    _attn_pairs(q_ref, kwin, vtwin, bias_ref, o_ref, tq // QPAIR,
                lambda jj: ATT_PAST - jj * QPAIR - i * tq, QPAIR)


def _attn_prompt(proj, bias2, t_prompt, width):
    tq = ATT_PAST
    assert t_prompt % tq == 0
    qb, kb, vb = 2, 3, 4
    prev = lambda i: jnp.maximum(i - 1, 0)
    return pl.pallas_call(
        _attn_prompt_kernel,
        grid=(t_prompt // tq,),
        in_specs=[pl.BlockSpec((tq, width), lambda i: (i, qb)),
                  pl.BlockSpec((tq, width), lambda i: (prev(i), kb)),
                  pl.BlockSpec((tq, width), lambda i: (i, kb)),
                  pl.BlockSpec((tq, width), lambda i: (prev(i), vb)),
                  pl.BlockSpec((tq, width), lambda i: (i, vb)),
                  pl.BlockSpec(bias2.shape, lambda i: (0, 0, 0))],
        out_specs=pl.BlockSpec((tq, width), lambda i: (i, 0)),
        out_shape=jax.ShapeDtypeStruct((t_prompt, width), F32),
        scratch_shapes=[pltpu.VMEM((ATT_PAST + tq, width), BF16),
                        pltpu.VMEM((width, ATT_PAST + tq), BF16)],
        compiler_params=_cparams(("arbitrary",)),
        name="attn_prompt",
    )(proj, proj, proj, proj, proj, bias2)


def _attn_sample_kernel(q_ref, kc_ref, ks_ref, vc_ref, vs_ref, bias_ref, o_ref, qpad, kwin, vtwin):
    width = q_ref.shape[1]
    qpad[0:CHUNK, :] = q_ref[...]
    qpad[CHUNK:QPAIR, :] = jnp.zeros((CHUNK, width), F32)
    kwin[0:ATT_PAST, :] = kc_ref[...].astype(BF16)
    kwin[ATT_PAST:BAND, :] = ks_ref[...].astype(BF16)
    kwin[BAND:KWIN, :] = jnp.zeros((KWIN - BAND, width), BF16)
    vtwin[:, 0:ATT_PAST] = vc_ref[...].T.astype(BF16)
    vtwin[:, ATT_PAST:KWIN] = jnp.concatenate(
        [vs_ref[...], jnp.zeros((KWIN - BAND, width), F32)], axis=0).T.astype(BF16)
    _attn_pairs(qpad, kwin, vtwin, bias_ref, o_ref, 1, lambda jj: None, CHUNK)


def _attn_sample(proj, cache_k, cache_v, bias2, t_prompt, n_seq, width):
    qb, kb, vb = 2, 3, 4
    c0 = t_prompt // CHUNK
    return pl.pallas_call(
        _attn_sample_kernel,
        grid=(n_seq,),
        in_specs=[pl.BlockSpec((CHUNK, width), lambda b: (c0 + b, qb)),
                  pl.BlockSpec((ATT_PAST, width), lambda b: (b, 0)),
                  pl.BlockSpec((CHUNK, width), lambda b: (c0 + b, kb)),
                  pl.BlockSpec((ATT_PAST, width), lambda b: (b, 0)),
                  pl.BlockSpec((CHUNK, width), lambda b: (c0 + b, vb)),
                  pl.BlockSpec(bias2.shape, lambda b: (0, 0, 0))],
        out_specs=pl.BlockSpec((CHUNK, width), lambda b: (b, 0)),
        out_shape=jax.ShapeDtypeStruct((n_seq * CHUNK, width), F32),
        scratch_shapes=[pltpu.VMEM((QPAIR, width), F32),
                        pltpu.VMEM((KWIN, width), BF16),
                        pltpu.VMEM((width, KWIN), BF16)],
        compiler_params=_cparams(("arbitrary",)),
        name="attn_sample",
    )(proj, cache_k, proj, cache_v, proj, bias2)


def _attn_bias(table):
    h = table.shape[0]
    x = np.arange(BAND + CHUNK - 1)
    rel = np.clip(BAND - 1 - x, -REL_CLIP, REL_CLIP) + REL_CLIP
    u = table[:, rel]
    std = jnp.stack([u[:, CHUNK - 1 - i:CHUNK - 1 - i + BAND] for i in range(CHUNK)], axis=1)
    neg = jnp.full((h, CHUNK, KWIN - BAND), NEG_BIG, F32)
    b = jnp.stack([jnp.concatenate([std, neg], axis=2),
                   jnp.concatenate([neg, std], axis=2)],
                  axis=1)
    b = b.reshape(h // 2, 2, 2, CHUNK, KWIN).transpose(0, 4, 1, 2, 3)
    return b.reshape(h // 2, KWIN, 2 * QPAIR)


def _outproj_kernel(y_ref, ap_ref, as_ref, xp_ref, xs_ref, gm_ref, scf_ref, shf_ref, npost_ref, npre_ref,
                    wo1_ref, wo2_ref, wr_ref, x1_ref, ht_ref, hb_ref, lg_ref, *, npt):
    i = pl.program_id(0)

    def body(a_ref, x_ref):
        tm, d = x_ref.shape
        mix = _bdot(y_ref[...].astype(BF16), wo1_ref[...]) + _bdot(a_ref[...].astype(BF16), wo2_ref[...])
        nm = _rms(mix, npost_ref[...]).reshape(tm // CHUNK, CHUNK, d)
        x1 = x_ref[...].reshape(tm // CHUNK, CHUNK, d) + gm_ref[...] * nm
        x1_ref[...] = x1.reshape(tm, d)
        hn = _rms(x1, npre_ref[...])
        hf = (hn * (1.0 + scf_ref[...]) + shf_ref[...]).reshape(tm, d)
        _store_token_tiles(ht_ref, 0, hf)
        hb_ref[...] = hf.astype(BF16)
        lg_ref[...] = lax.dot_general(wr_ref[...], hf, (((1,), (1,)), ((), ())),
                                      precision=HIGHEST, preferred_element_type=F32)

    @pl.when(i < npt)
    def _():
        body(ap_ref, xp_ref)

    @pl.when(i >= npt)
    def _():
        body(as_ref, xs_ref)


def _outproj(y_ssd, att_p, att_s, xp, xs, gate_m, scale_f, shift_f, npost, npre, wo1, wo2, wr_t):
    tp, d = xp.shape
    ts = xs.shape[0]
    t = tp + ts
    inner = y_ssd.shape[1]
    ne = wr_t.shape[0]
    tm = _pick((tp, ts), (256, 128, 64))
    nc = tm // CHUNK
    first, second = _split_rows(tp // tm)
    row = lambda i: (i, 0)
    tab = lambda i: (i, 0, 0)
    const = lambda i: (0, 0)
    return pl.pallas_call(
        functools.partial(_outproj_kernel, npt=tp // tm),
        grid=(t // tm,),
        in_specs=[pl.BlockSpec((tm, inner), row),
                  pl.BlockSpec((tm, att_p.shape[1]), first), pl.BlockSpec((tm, att_s.shape[1]), second),
                  pl.BlockSpec((tm, d), first), pl.BlockSpec((tm, d), second),
                  pl.BlockSpec((nc, 1, d), tab), pl.BlockSpec((nc, 1, d), tab), pl.BlockSpec((nc, 1, d), tab),
                  pl.BlockSpec((1, d), const), pl.BlockSpec((1, d), const),
                  pl.BlockSpec(wo1.shape, const), pl.BlockSpec(wo2.shape, const),
                  pl.BlockSpec(wr_t.shape, const)],
        out_specs=[pl.BlockSpec((tm, d), row), pl.BlockSpec((tm * (d // LANES), LANES), row),
                   pl.BlockSpec((tm, d), row), pl.BlockSpec((ne, tm), lambda i: (0, i))],
        out_shape=[jax.ShapeDtypeStruct((t, d), F32), jax.ShapeDtypeStruct((t * (d // LANES), LANES), F32),
                   jax.ShapeDtypeStruct((t, d), BF16), jax.ShapeDtypeStruct((ne, t), F32)],
        compiler_params=_cparams(("arbitrary",)),
        name="outproj",
    )(y_ssd, att_p, att_s, xp, xs, gate_m, scale_f, shift_f, npost, npre, wo1, wo2, wr_t)


def _route_kernel(lg_ref, rb_ref, ti_ref, tw_ref):
    ne, tt = lg_ref.shape
    gs = ne // N_EXPERT_GROUPS
    scores = jax.nn.sigmoid(lg_ref[...])
    sel = scores + rb_ref[...]
    g3 = sel.reshape(N_EXPERT_GROUPS, gs, tt)
    i3 = lax.broadcasted_iota(I32, g3.shape, 1)
    m1 = jnp.max(g3, axis=1, keepdims=True)
    first = jnp.min(jnp.where(g3 == m1, i3, gs), axis=1, keepdims=True)
    m2 = jnp.max(jnp.where(i3 == first, -jnp.inf, g3), axis=1, keepdims=True)
    gscore = (m1 + m2).reshape(N_EXPERT_GROUPS, tt)
    gi = lax.broadcasted_iota(I32, gscore.shape, 0)
    gmask = jnp.zeros(gscore.shape, jnp.bool_)
    rem = gscore
    for _ in range(TOPK_GROUPS):
        mg = jnp.max(rem, axis=0, keepdims=True)
        pick = jnp.min(jnp.where(rem == mg, gi, N_EXPERT_GROUPS), axis=0, keepdims=True)
        hit = gi == pick
        gmask = gmask | hit
        rem = jnp.where(hit, -jnp.inf, rem)
    emask = jnp.broadcast_to(gmask.reshape(N_EXPERT_GROUPS, 1, tt), g3.shape).reshape(ne, tt)
    rem = jnp.where(emask, sel, -jnp.inf)
    ei = lax.broadcasted_iota(I32, (ne, tt), 0)
    idx, wts = [], []
    for _ in range(TOP_K):
        me = jnp.max(rem, axis=0, keepdims=True)
        pick = jnp.min(jnp.where(rem == me, ei, ne), axis=0, keepdims=True)
        hit = ei == pick
        idx.append(pick)
        wts.append(jnp.sum(jnp.where(hit, scores, 0.0), axis=0, keepdims=True))
        rem = jnp.where(hit, -jnp.inf, rem)
    w = jnp.concatenate(wts, axis=0)
    ti_ref[...] = jnp.concatenate(idx, axis=0)
    tw_ref[...] = w / jnp.sum(w, axis=0, keepdims=True) * ROUTED_SCALE


def _route(logits_t, router_bias):
    ne, t = logits_t.shape
    tt = _pick((t,), (2176, 2048, 1024, 512, 256, 128))
    return pl.pallas_call(
        _route_kernel,
        grid=(t // tt,),
        in_specs=[pl.BlockSpec((ne, tt), lambda i: (0, i)),
                  pl.BlockSpec((ne, 1), lambda i: (0, 0))],
        out_specs=[pl.BlockSpec((TOP_K, tt), lambda i: (0, i)),
                   pl.BlockSpec((TOP_K, tt), lambda i: (0, i))],
        out_shape=[jax.ShapeDtypeStruct((TOP_K, t), I32),
                   jax.ShapeDtypeStruct((TOP_K, t), F32)],
        compiler_params=_cparams(("arbitrary",)),
        name="route",
    )(logits_t, router_bias.reshape(ne, 1))


def _moe_kernel(be_ref, r0_ref, n_ref, first_ref, a_ref,
                ht_hbm, w1_ref, w3_ref, w2_ref, y8_hbm,
                xbuf_a, xbuf_b, obuf_a, obuf_b, xs_scr, w1b, w3b, w2b, gsem, ssem, *, n_tok, tme):
    del be_ref
    b = pl.program_id(0)
    nb = pl.num_programs(0)
    nbk = xs_scr.shape[1] // LANES
    spare0 = TOP_K * n_tok
    whole = tme * nbk

    def tile(ref, row):
        start = row * nbk
        return ref.at[pl.ds(start if isinstance(start, int) else pl.multiple_of(start, nbk), nbk)]

    def gather_all(xb, s):
        return pltpu.make_async_copy(ht_hbm.at[pl.ds(0, whole)], xb, gsem.at[s])

    def scatter_all(ob, s):
        return pltpu.make_async_copy(ob, y8_hbm.at[pl.ds(0, whole)], ssem.at[s])

    def start_gather(blk, xb, s):
        r0 = r0_ref[blk]
        for r in range(tme):
            pltpu.make_async_copy(tile(ht_hbm, a_ref[r0 + r] >> 3), tile(xb, r), gsem.at[s]).start()

    def start_scatter(blk, n_valid, ob, s):
        r0 = r0_ref[blk]
        for r in range(tme):
            dst = jnp.where(r < n_valid, a_ref[r0 + r], spare0 + s * tme + r)
            pltpu.make_async_copy(tile(ob, r), tile(y8_hbm, dst), ssem.at[s]).start()

    @pl.when(b == 0)
    def _():
        start_gather(0, xbuf_a, 0)
        for s, ob in enumerate((obuf_a, obuf_b)):
            ob[...] = jnp.zeros_like(ob)
            spare = pltpu.make_async_copy(ob, y8_hbm.at[pl.ds((spare0 + s * tme) * nbk, whole)], ssem.at[s])
            spare.start()
            spare.wait()

    @pl.when(first_ref[b] == 1)
    def _():
        w1b[...] = w1_ref[0].astype(BF16)
        w3b[...] = w3_ref[0].astype(BF16)
        w2b[...] = w2_ref[0].astype(BF16)

    nxt = jnp.minimum(b + 1, nb - 1)
    prv = jnp.maximum(b - 1, 0)
    n_prv = jnp.where(b >= 1, n_ref[prv], 0)

    def step(s, xb_cur, xb_nxt, ob_cur, ob_prv):
        gather_all(xb_cur, s).wait()

        @pl.when(b >= 1)
        def _():
            scatter_all(ob_cur, s).wait()

        start_gather(nxt, xb_nxt, 1 - s)
        for c, blk in enumerate(_load_token_tiles(xb_cur, 0, tme, nbk)):
            xs_scr[:, c * LANES:(c + 1) * LANES] = blk.astype(BF16)
        x = xs_scr[...]
        h = (_silu(_bdot(x, w1b[...])) * _bdot(x, w3b[...])).astype(BF16)
        _store_token_tiles(ob_cur, 0, _bdot(h, w2b[...]))
        start_scatter(prv, n_prv, ob_prv, 1 - s)

        @pl.when(b == nb - 1)
        def _():
            gather_all(xb_nxt, 1 - s).wait()
            scatter_all(ob_prv, 1 - s).wait()
            start_scatter(b, n_ref[b], ob_cur, s)
            scatter_all(ob_cur, s).wait()

    @pl.when(b % 2 == 0)
    def _():
        step(0, xbuf_a, xbuf_b, obuf_a, obuf_b)

    @pl.when(b % 2 == 1)
    def _():
        step(1, xbuf_b, xbuf_a, obuf_b, obuf_a)


def _moe(ht, a_sorted, blk_e, blk_r0, blk_n, blk_first, w1, w3, w2):
    ne, d, de = w1.shape
    nbk = d // LANES
    t = ht.shape[0] // nbk
    nb = blk_e.shape[0]
    tme = MOE_ROWS
    assert (TOP_K * t) % tme == 0 and nbk % SUBLANES == 0
    wmap = lambda b, be, r0, n, f, a: (be[b], 0, 0)
    grid_spec = pltpu.PrefetchScalarGridSpec(
        num_scalar_prefetch=5,
        grid=(nb,),
        in_specs=[pl.BlockSpec(memory_space=pl.ANY),
                  pl.BlockSpec((1, d, de), wmap),
                  pl.BlockSpec((1, d, de), wmap),
                  pl.BlockSpec((1, de, d), wmap)],
        out_specs=pl.BlockSpec(memory_space=pl.ANY),
        scratch_shapes=[pltpu.VMEM((tme * nbk, LANES), F32), pltpu.VMEM((tme * nbk, LANES), F32),
                        pltpu.VMEM((tme * nbk, LANES), F32), pltpu.VMEM((tme * nbk, LANES), F32),
                        pltpu.VMEM((tme, d), BF16),
                        pltpu.VMEM((d, de), BF16), pltpu.VMEM((d, de), BF16), pltpu.VMEM((de, d), BF16),
                        pltpu.SemaphoreType.DMA((2,)), pltpu.SemaphoreType.DMA((2,))],
    )
    return pl.pallas_call(
        functools.partial(_moe_kernel, n_tok=t, tme=tme),
        grid_spec=grid_spec,
        out_shape=jax.ShapeDtypeStruct(((TOP_K * t + 2 * tme) * nbk, LANES), F32),
        compiler_params=_cparams(("arbitrary",)),
        name="moe",
    )(blk_e, blk_r0, blk_n, blk_first, a_sorted, ht, w1, w3, w2)


def _dispatch_plan(topi_t, tme, ne):
    k, t = topi_t.shape
    assert k == TOP_K
    a_cnt = k * t
    shift = int(np.ceil(np.log2(a_cnt)))
    assert ne << shift < 2 ** 31
    a_id = jnp.arange(t, dtype=I32)[None, :] * k + jnp.arange(k, dtype=I32)[:, None]
    keys = (topi_t << shift) + a_id
    a_sorted = jnp.sort(keys.reshape(-1)) & ((1 << shift) - 1)
    a_sorted = jnp.concatenate([a_sorted, jnp.zeros((tme,), I32)])
    counts = jnp.sum(topi_t.reshape(-1, 1) == jnp.arange(ne, dtype=I32)[None, :], axis=0, dtype=I32)
    starts = jnp.cumsum(counts) - counts
    nblk = (counts + tme - 1) // tme
    blk_end = jnp.cumsum(nblk)
    nb = a_cnt // tme + ne
    b = jnp.arange(nb, dtype=I32)
    valid = b < blk_end[-1]
    e_raw = jnp.minimum(jnp.sum(blk_end[None, :] <= b[:, None], axis=1, dtype=I32), ne - 1)
    onehot = (e_raw[:, None] == jnp.arange(ne, dtype=I32)[None, :]).astype(I32)
    pick = lambda v: jnp.sum(onehot * v[None, :], axis=1)
    j = b - pick(blk_end - nblk)
    blk_r0 = jnp.where(valid, pick(starts) + j * tme, 0)
    blk_n = jnp.where(valid, jnp.minimum(tme, pick(counts) - j * tme), 0)
    blk_first = (valid & (j == 0)).astype(I32)
    e_last = jnp.max(jnp.where(valid, e_raw, 0))
    blk_e = jnp.where(valid, e_raw, e_last)
    return a_sorted, blk_e, blk_r0, blk_n, blk_first


def _final_kernel(y8_ref, tw_ref, hb_ref, x1_ref, gf_ref, npost_ref, ws1_ref, ws3_ref, ws2_ref,
                  op_ref, os_ref, *, npt):
    i = pl.program_id(0)
    tm, d = x1_ref.shape
    nbk = d // LANES
    tw = tw_ref[...]
    routed = None
    for k in range(TOP_K):
        wk = jnp.broadcast_to(tw[:, k:k + 1], (tm, LANES))
        part = [y8_ref[pl.ds(k * nbk + c, tm, stride=TOP_K * nbk), :] * wk for c in range(nbk)]
        routed = part if routed is None else [a + p for a, p in zip(routed, part)]
    hb = hb_ref[...]
    shared = _bdot((_silu(_bdot(hb, ws1_ref[...])) * _bdot(hb, ws3_ref[...])).astype(BF16), ws2_ref[...])
    nm = _rms(jnp.concatenate(routed, axis=1) + shared, npost_ref[...]).reshape(tm // CHUNK, CHUNK, d)
    y = (x1_ref[...].reshape(tm // CHUNK, CHUNK, d) + gf_ref[...] * nm).reshape(tm, d)

    @pl.when(i < npt)
    def _():
        op_ref[...] = y

    @pl.when(i >= npt)
    def _():
        os_ref[...] = y


def _final(y8, topw, hb, x1, gate_f, npost, ws1, ws3, ws2, tp):
    t, d = x1.shape
    ts = t - tp
    nbk = d // LANES
    tm = _pick((tp, ts), (128, 64))
    nc = tm // CHUNK
    nt = t // tm
    first, second = _split_rows(tp // tm)
    row = lambda i: (i, 0)
    const = lambda i: (0, 0)
    return pl.pallas_call(
        functools.partial(_final_kernel, npt=tp // tm),
        grid=(nt,),
        in_specs=[pl.BlockSpec((tm * TOP_K * nbk, LANES), row)] + [pl.BlockSpec((tm, TOP_K), row),
                           pl.BlockSpec((tm, d), row), pl.BlockSpec((tm, d), row),
                           pl.BlockSpec((nc, 1, d), lambda i: (i, 0, 0)),
                           pl.BlockSpec((1, d), const),
                           pl.BlockSpec(ws1.shape, const), pl.BlockSpec(ws3.shape, const),
                           pl.BlockSpec(ws2.shape, const)],
        out_specs=[pl.BlockSpec((tm, d), first), pl.BlockSpec((tm, d), second)],
        out_shape=[jax.ShapeDtypeStruct((tp, d), F32), jax.ShapeDtypeStruct((ts, d), F32)],
        compiler_params=_cparams(("arbitrary",)),
        name="final",
    )(y8, topw, hb, x1, gate_f, npost, ws1, ws3, ws2)


def kernel(x_prompt, x_sample, cache_conv, state_ssm, cache_k, cache_v, c_prompt, c_sample,
           w_ada, b_ada, norm_pre_mix, norm_post_mix, norm_pre_ffn, norm_post_ffn,
           w_in, conv_w, conv_b, dt_bias, a_log, d_skip, gn_w, rel_bias, w_out,
           w_router, router_bias, w1, w3, w2, ws1, ws3, ws2):
    assert w_ada.shape[0] == 1, "single layer"
    bp, lp, d = x_prompt.shape
    bs, ls, _ = x_sample.shape
    assert bp == 1 and ls == CHUNK and lp % ATT_PAST == 0
    assert cache_k.shape[2] == ATT_PAST
    heads = a_log.shape[1]
    inner = heads * SSM_HEAD_DIM
    att_w = rel_bias.shape[1] * ATT_HEAD_DIM
    assert att_w == inner
    bcw = 2 * SSM_GROUPS * SSM_STATE
    ne = w_router.shape[2]
    tp, ts = bp * lp, bs * ls
    nseq = bp + bs

    xp, xs = x_prompt.reshape(tp, d), x_sample.reshape(ts, d)
    seq_np = np.concatenate([np.repeat(np.arange(bp), lp // CHUNK), bp + np.arange(bs)]).astype(np.int32)
    first_np = np.concatenate([[1], (seq_np[1:] != seq_np[:-1])]).astype(np.int32)
    seq_of_chunk, first_of_chunk = jnp.asarray(seq_np), jnp.asarray(first_np)

    c_all = jnp.concatenate([c_prompt, c_sample], axis=0)
    c_pad = jnp.pad(c_all, ((0, -nseq % 8), (0, 0)))
    mod = _ada(c_pad, w_ada[0], b_ada[0])[:nseq].reshape(nseq, 6, d)
    mod_c = mod[seq_of_chunk]
    shift_m, scale_m, gate_m, shift_f, scale_f, gate_f = [mod_c[:, i:i + 1, :] for i in range(6)]

    wi = w_in[0]
    o_z, o_x, o_bc = 0, inner, 2 * inner
    o_dt = inner + inner + bcw
    o_q = o_dt + heads
    o_k, o_v = o_q + att_w, o_q + 2 * att_w
    cols = lambda o, n: wi[:, o:o + n]
    w_main = jnp.concatenate([cols(o_z, inner), cols(o_x, inner), cols(o_q, att_w), cols(o_k, att_w),
                              cols(o_v, att_w), cols(o_bc, bcw)], axis=1).astype(BF16)
    w_dt = jnp.pad(cols(o_dt, heads), ((0, 0), (0, LANES - heads))).astype(BF16)
    proj, dt_raw = _inproj(xp, xs, scale_m, shift_m, norm_pre_mix, w_main, w_dt)
    c_x, c_k, c_v, c_bc = inner, 3 * inner, 4 * inner, 5 * inner

    pad_rows = lambda a: jnp.pad(a, ((0, 0), (8 - (CONV_W - 1), 0), (0, 0)))
    pre = jnp.concatenate([jnp.zeros((bp, CONV_W - 1, inner + bcw), F32), cache_conv[0]], axis=0)
    pre_x, pre_bc = pad_rows(pre[:, :, :inner]), pad_rows(pre[:, :, inner:])
    h0 = jnp.concatenate([jnp.zeros((bp,) + state_ssm.shape[2:], F32), state_ssm[0]], axis=0)
    h0t = h0.transpose(0, 3, 1, 2).reshape(nseq, SSM_STATE, inner)
    lane_pad = lambda v: jnp.pad(v, (0, LANES - heads)).reshape(1, LANES)
    expand = (np.arange(LANES)[:, None] == (np.arange(inner)[None, :] // SSM_HEAD_DIM)).astype(np.float32)
    tri = np.tril(np.ones((CHUNK, CHUNK), np.float32))
    consts = (conv_w[0][:, :inner], conv_w[0][:, inner:],
              conv_b[0][:inner].reshape(1, inner), conv_b[0][inner:].reshape(1, bcw),
              lane_pad(dt_bias[0]), lane_pad(-jnp.exp(a_log[0])),
              jnp.repeat(d_skip[0], SSM_HEAD_DIM).reshape(1, inner), gn_w[0].reshape(1, inner),
              jnp.asarray(expand), jnp.asarray(tri))
    y_ssd, st_out = _ssd(proj, dt_raw, seq_of_chunk, first_of_chunk, pre_x, pre_bc, h0t, consts, inner)

    bias2 = _attn_bias(rel_bias[0])
    att_p = _attn_prompt(proj, bias2, tp, att_w)
    att_s = _attn_sample(proj, cache_k[0].reshape(bs * ATT_PAST, att_w), cache_v[0].reshape(bs * ATT_PAST, att_w),
                         bias2, tp, bs, att_w)

    wo = w_out[0].astype(BF16)
    x1, ht, hb, logits_t = _outproj(y_ssd, att_p, att_s, xp, xs, gate_m, scale_f, shift_f,
                                norm_post_mix, norm_pre_ffn, wo[:inner], wo[inner:], w_router[0].T)

    topi_t, topw_t = _route(logits_t, router_bias[0])
    a_sorted, blk_e, blk_r0, blk_n, blk_first = _dispatch_plan(topi_t, MOE_ROWS, ne)
    y8 = _moe(ht, a_sorted, blk_e, blk_r0, blk_n, blk_first, w1[0], w3[0], w2[0])
    y_p, y_s = _final(y8, topw_t.T, hb, x1, gate_f, norm_post_ffn,
                      ws1[0].astype(BF16), ws3[0].astype(BF16), ws2[0].astype(BF16), tp)

    tail = lambda rows: jnp.concatenate([rows[..., c_x:c_x + inner], rows[..., c_bc:c_bc + bcw]], axis=-1)
    conv_prompt = tail(proj[:tp].reshape(bp, lp, -1)[:, lp - (CONV_W - 1):, :])[None]
    conv_sample = tail(proj[tp:].reshape(bs, ls, -1)[:, ls - (CONV_W - 1):, :])[None]
    st = st_out.reshape(nseq, SSM_STATE, heads, SSM_HEAD_DIM).transpose(0, 2, 3, 1)
    keep = min(ATT_PAST, lp)
    hd = (rel_bias.shape[1], ATT_HEAD_DIM)
    kv = lambda c0, r0, r1, b, l: proj[r0:r1, c0:c0 + att_w].reshape(b, l, *hd)[None]
    return (y_p.reshape(bp, lp, d), y_s.reshape(bs, ls, d),
            conv_prompt, st[:bp][None], kv(c_k, tp - keep, tp, bp, keep), kv(c_v, tp - keep, tp, bp, keep),
            conv_sample, st[bp:][None], kv(c_k, tp, tp + ts, bs, ls), kv(c_v, tp, tp + ts, bs, ls))
```

```python
import functools

import numpy as np
import jax
import jax.numpy as jnp
from jax import lax
from jax.experimental import pallas as pl
from jax.experimental.pallas import tpu as pltpu

F32 = jnp.float32
BF16 = jnp.bfloat16
I32 = jnp.int32
HIGHEST = lax.Precision.HIGHEST

CHUNK = 64
SSM_HEAD_DIM = 64
SSM_GROUPS = 2
SSM_STATE = 128
CONV_W = 4
ATT_HEAD_DIM = 64
LEFT_CHUNKS = 8
ATT_PAST = LEFT_CHUNKS * CHUNK
BAND = ATT_PAST + CHUNK
REL_CLIP = 128
TOP_K = 8
N_EXPERT_GROUPS = 8
TOPK_GROUPS = 4
ROUTED_SCALE = 2.5
EPS = 1e-6
NEG_BIG = -1e30

LANES = 128
SUBLANES = 8
PAIR = 2 * ATT_HEAD_DIM
QPAIR = 2 * CHUNK
KWIN = ATT_PAST + QPAIR
VMEM_LIMIT = 56 * 1024 * 1024
MOE_ROWS = 256
ATT_UNROLL = 4


def _cparams(sem):
    return pltpu.CompilerParams(dimension_semantics=sem, vmem_limit_bytes=VMEM_LIMIT)


def _pick(ns, cands):
    for c in cands:
        if all(n % c == 0 for n in ns):
            return c
    raise ValueError(f"no tile for {ns} in {cands}")


def _silu(x):
    return x * jax.nn.sigmoid(x)


def _rms(x, g):
    ms = jnp.mean(x * x, axis=-1, keepdims=True)
    return x * lax.rsqrt(ms + EPS) * g


def _bdot(a, b):
    return jnp.dot(a, b, preferred_element_type=F32)


def _split3(x):
    p0 = x.astype(BF16)
    r0 = x - p0.astype(F32)
    p1 = r0.astype(BF16)
    p2 = (r0 - p1.astype(F32)).astype(BF16)
    return p0, p1, p2


def _store_token_tiles(ref, base, x):
    rows, d = x.shape
    nb = d // LANES
    for c in range(nb):
        ref[pl.ds(base * nb + c, rows, stride=nb), :] = x[:, c * LANES:(c + 1) * LANES]


def _load_token_tiles(ref, base, rows, nb):
    return [ref[pl.ds(base * nb + c, rows, stride=nb), :] for c in range(nb)]


def _split_rows(npt):
    first = lambda i, *_: (jnp.minimum(i, npt - 1), 0)
    second = lambda i, *_: (jnp.maximum(i - npt, 0), 0)
    return first, second


def _ada_kernel(c_ref, w_ref, b_ref, o_ref):
    a = _silu(c_ref[...])
    o_ref[...] = jnp.dot(a, w_ref[...], precision=HIGHEST, preferred_element_type=F32) + b_ref[...]


def _ada(c_pad, w_ada, b_ada):
    m, d = c_pad.shape
    n = w_ada.shape[1]
    tn = _pick((n,), (1024, 512, 256, 128))
    return pl.pallas_call(
        _ada_kernel,
        grid=(n // tn,),
        in_specs=[pl.BlockSpec((m, d), lambda j: (0, 0)),
                  pl.BlockSpec((d, tn), lambda j: (0, j)),
                  pl.BlockSpec((1, tn), lambda j: (0, j))],
        out_specs=pl.BlockSpec((m, tn), lambda j: (0, j)),
        out_shape=jax.ShapeDtypeStruct((m, n), F32),
        compiler_params=_cparams(("arbitrary",)),
        name="ada",
    )(c_pad, w_ada, b_ada.reshape(1, n))


def _inproj_kernel(xp_ref, xs_ref, sc_ref, sh_ref, g_ref, w_ref, wdt_ref, o_ref, dt_ref, hm_ref, *, npt):
    i = pl.program_id(0)

    def prep(x_ref):
        x = x_ref[...]
        tm, d = x.shape
        y = _rms(x, g_ref[...]).reshape(tm // CHUNK, CHUNK, d)
        h = (y * (1.0 + sc_ref[...]) + sh_ref[...]).reshape(tm, d).astype(BF16)
        hm_ref[...] = h
        dt_ref[...] = _bdot(h, wdt_ref[...])

    @pl.when(pl.program_id(1) == 0)
    def _():
        @pl.when(i < npt)
        def _():
            prep(xp_ref)

        @pl.when(i >= npt)
        def _():
            prep(xs_ref)

    o_ref[...] = _bdot(hm_ref[...], w_ref[...])


def _inproj(xp, xs, scale, shift, g, w_main, w_dt):
    tp, d = xp.shape
    ts = xs.shape[0]
    t = tp + ts
    n = w_main.shape[1]
    tm = _pick((tp, ts), (1024, 512, 256, 128, 64))
    tn = _pick((n,), (512, 256, 128))
    nc = tm // CHUNK
    first, second = _split_rows(tp // tm)
    return pl.pallas_call(
        functools.partial(_inproj_kernel, npt=tp // tm),
        grid=(t // tm, n // tn),
        in_specs=[pl.BlockSpec((tm, d), first),
                  pl.BlockSpec((tm, d), second),
                  pl.BlockSpec((nc, 1, d), lambda i, j: (i, 0, 0)),
                  pl.BlockSpec((nc, 1, d), lambda i, j: (i, 0, 0)),
                  pl.BlockSpec((1, d), lambda i, j: (0, 0)),
                  pl.BlockSpec((d, tn), lambda i, j: (0, j)),
                  pl.BlockSpec((d, LANES), lambda i, j: (0, 0))],
        out_specs=[pl.BlockSpec((tm, tn), lambda i, j: (i, j)),
                   pl.BlockSpec((tm, LANES), lambda i, j: (i, 0))],
        out_shape=[jax.ShapeDtypeStruct((t, n), F32),
                   jax.ShapeDtypeStruct((t, LANES), F32)],
        scratch_shapes=[pltpu.VMEM((tm, d), BF16)],
        compiler_params=_cparams(("arbitrary", "arbitrary")),
        name="inproj",
    )(xp, xs, scale, shift, g, w_main, w_dt)


def _ssd_kernel(seq_ref, first_ref,
                z_ref, xs_ref, bc_ref, dt_ref, prex_ref, prebc_ref, h0_ref,
                cwx_ref, cwbc_ref, cbx_ref, cbbc_ref, dtb_ref, aneg_ref, dsk_ref, gnw_ref,
                e_ref, tri_ref,
                y_ref, st_out_ref,
                xpx_scr, xpbc_scr, st_scr):
    del seq_ref
    c = pl.program_id(0)
    inner = xs_ref.shape[1]
    gw = inner // SSM_GROUPS
    n = SSM_STATE
    pad = 8

    @pl.when(first_ref[c] == 1)
    def _():
        xpx_scr[0:pad, :] = prex_ref[0]
        xpbc_scr[0:pad, :] = prebc_ref[0]
        st_scr[...] = h0_ref[0]

    xpx_scr[pad:pad + CHUNK, :] = xs_ref[...]
    xpbc_scr[pad:pad + CHUNK, :] = bc_ref[...]

    def conv(xp, w_ref, b_ref):
        base = pad - (CONV_W - 1)
        acc = b_ref[...] + xp[base:base + CHUNK, :] * w_ref[0:1, :]
        for k in range(1, CONV_W):
            acc = acc + xp[base + k:base + k + CHUNK, :] * w_ref[k:k + 1, :]
        return _silu(acc)

    xs = conv(xpx_scr, cwx_ref, cbx_ref)
    bc = conv(xpbc_scr, cwbc_ref, cbbc_ref)
    xpx_scr[0:pad, :] = xpx_scr[CHUNK:CHUNK + pad, :]
    xpbc_scr[0:pad, :] = xpbc_scr[CHUNK:CHUNK + pad, :]

    dtv = dt_ref[...] + dtb_ref[...]
    dt = jnp.maximum(dtv, 0.0) + jnp.log(1.0 + jnp.exp(-jnp.abs(dtv)))
    da = dt * aneg_ref[...]
    acs = sum(_bdot(tri_ref[...], p) for p in _split3(da))
    full = sum(_bdot(p, e_ref[...]) for p in _split3(jnp.concatenate([dt, acs], axis=0)))
    dtf = full[0:CHUNK]
    af = full[CHUNK:2 * CHUNK]

    row = lax.broadcasted_iota(I32, (CHUNK, inner), 0)
    lj = lax.broadcasted_iota(I32, (CHUNK, inner), 1) & (SSM_HEAD_DIM - 1)
    aj = jnp.sum(jnp.where(row == lj, af, 0.0), axis=0, keepdims=True)
    lmat = jnp.exp(jnp.where(row >= lj, af - aj, NEG_BIG))
    alast = af[CHUNK - 1:CHUNK, :]
    xdt = xs * dtf
    xw = xdt * jnp.exp(alast - af)
    cdec = jnp.exp(alast)
    eaf = jnp.exp(af)

    lane = lax.broadcasted_iota(I32, (CHUNK, PAIR), 1)
    st = st_scr[...]
    ydiag, yoff, stn = [], [], []
    for g in range(SSM_GROUPS):
        bg = bc[:, g * n:(g + 1) * n].astype(BF16)
        cg = bc[:, (SSM_GROUPS + g) * n:(SSM_GROUPS + g + 1) * n].astype(BF16)
        bb = jnp.concatenate([bg, bg], axis=0)
        cbb = lax.dot_general(cg, bb, (((1,), (1,)), ((), ())), preferred_element_type=F32)
        stg = st[:, g * gw:(g + 1) * gw]
        yoff.append(_bdot(cg, stg.astype(BF16)))
        for p in range(gw // PAIR):
            lo = g * gw + p * PAIR
            m = (cbb * lmat[:, lo:lo + PAIR]).astype(BF16)
            xd = xdt[:, lo:lo + PAIR]
            w = jnp.concatenate([jnp.where(lane < SSM_HEAD_DIM, xd, 0.0),
                                 jnp.where(lane >= SSM_HEAD_DIM, xd, 0.0)], axis=0).astype(BF16)
            ydiag.append(_bdot(m, w))
        upd = lax.dot_general(bg, xw[:, g * gw:(g + 1) * gw].astype(BF16),
                              (((0,), (0,)), ((), ())), preferred_element_type=F32)
        stn.append(cdec[:, g * gw:(g + 1) * gw] * stg + upd)

    y = jnp.concatenate(ydiag, axis=1) + jnp.concatenate(yoff, axis=1) * eaf + dsk_ref[...] * xs
    y = y * _silu(z_ref[...])
    y_ref[...] = _rms(y, gnw_ref[...])
    st_new = jnp.concatenate(stn, axis=1)
    st_scr[...] = st_new
    st_out_ref[0] = st_new


def _ssd(proj, dt_raw, seq_of_chunk, first_of_chunk, pre_x, pre_bc, h0t, consts, inner):
    t = proj.shape[0]
    nch = t // CHUNK
    nseq = h0t.shape[0]
    bcw = pre_bc.shape[-1]
    assert (5 * inner) % bcw == 0
    cmap = lambda blk: (lambda c, s, f: (c, blk))
    smap3 = lambda c, s, f: (s[c], 0, 0)
    const2 = lambda c, s, f: (0, 0)
    grid_spec = pltpu.PrefetchScalarGridSpec(
        num_scalar_prefetch=2,
        grid=(nch,),
        in_specs=[pl.BlockSpec((CHUNK, inner), cmap(0)),
                  pl.BlockSpec((CHUNK, inner), cmap(1)),
                  pl.BlockSpec((CHUNK, bcw), cmap((5 * inner) // bcw)),
                  pl.BlockSpec((CHUNK, LANES), lambda c, s, f: (c, 0)),
                  pl.BlockSpec((1, 8, inner), smap3),
                  pl.BlockSpec((1, 8, bcw), smap3),
                  pl.BlockSpec((1, SSM_STATE, inner), smap3)]
                 + [pl.BlockSpec(a.shape, const2) for a in consts],
        out_specs=[pl.BlockSpec((CHUNK, inner), lambda c, s, f: (c, 0)),
                   pl.BlockSpec((1, SSM_STATE, inner), smap3)],
        scratch_shapes=[pltpu.VMEM((CHUNK + 8, inner), F32),
                        pltpu.VMEM((CHUNK + 8, bcw), F32),
                        pltpu.VMEM((SSM_STATE, inner), F32)],
    )
    return pl.pallas_call(
        _ssd_kernel,
        grid_spec=grid_spec,
        out_shape=[jax.ShapeDtypeStruct((t, inner), F32),
                   jax.ShapeDtypeStruct((nseq, SSM_STATE, inner), F32)],
        compiler_params=_cparams(("arbitrary",)),
        name="ssd",
    )(seq_of_chunk, first_of_chunk, proj, proj, proj, dt_raw, pre_x, pre_bc, h0t, *consts)


def _attn_pairs(q_ref, kwin, vtwin, bias_ref, o_ref, n_steps, n_masked_fn, out_rows):
    n_pairs = q_ref.shape[1] // PAIR
    rowp = lax.broadcasted_iota(I32, (PAIR, QPAIR), 0)
    krow = lax.broadcasted_iota(I32, (KWIN, 2 * QPAIR), 0)

    for jj in range(n_steps):
        n_masked = n_masked_fn(jj)

        def one_pair(hp, jj=jj, n_masked=n_masked):
            lo = pl.multiple_of(hp * PAIR, PAIR)
            q = q_ref[jj * QPAIR:(jj + 1) * QPAIR, pl.ds(lo, PAIR)] * (ATT_HEAD_DIM ** -0.5)
            qt = q.T
            w = jnp.concatenate([jnp.where(rowp < ATT_HEAD_DIM, qt, 0.0),
                                 jnp.where(rowp >= ATT_HEAD_DIM, qt, 0.0)], axis=1).astype(BF16)
            kb = kwin[jj * QPAIR:jj * QPAIR + KWIN, pl.ds(lo, PAIR)]
            s = _bdot(kb, w) + bias_ref[hp]
            if n_masked is not None:
                s = jnp.where(krow < n_masked, NEG_BIG, s)
            mx = jnp.max(s, axis=0, keepdims=True)
            p = jnp.exp(s - mx)
            den = jnp.sum(p, axis=0, keepdims=True)
            vb = vtwin[pl.ds(lo, PAIR), jj * QPAIR:jj * QPAIR + KWIN]
            o2 = _bdot(vb, p.astype(BF16)) / den
            ot = jnp.where(rowp < ATT_HEAD_DIM, o2[:, 0:QPAIR], o2[:, QPAIR:2 * QPAIR])
            o_ref[jj * out_rows:(jj + 1) * out_rows, pl.ds(lo, PAIR)] = ot.T[0:out_rows]

        def body(i, carry, one_pair=one_pair):
            for u in range(ATT_UNROLL):
                one_pair(i * ATT_UNROLL + u)
            return carry

        lax.fori_loop(0, n_pairs // ATT_UNROLL, body, 0)


def _attn_prompt_kernel(q_ref, kp_ref, kc_ref, vp_ref, vc_ref, bias_ref, o_ref, kwin, vtwin):
    i = pl.program_id(0)
    tq = q_ref.shape[0]
    kwin[0:ATT_PAST, :] = kp_ref[...].astype(BF16)
    kwin[ATT_PAST:ATT_PAST + tq, :] = kc_ref[...].astype(BF16)
    vtwin[:, 0:ATT_PAST] = vp_ref[...].T.astype(BF16)
    vtwin[:, ATT_PAST:ATT_PAST + tq] = vc_ref[...].T.astype(BF16)
    _attn_pairs(q_ref, kwin, vtwin, bias_ref, o_ref, tq // QPAIR,
                lambda jj: ATT_PAST - jj * QPAIR - i * tq, QPAIR)


def _attn_prompt(proj, bias2, t_prompt, width):
    tq = ATT_PAST
    assert t_prompt % tq == 0
    qb, kb, vb = 2, 3, 4
    prev = lambda i: jnp.maximum(i - 1, 0)
    return pl.pallas_call(
        _attn_prompt_kernel,
        grid=(t_prompt // tq,),
        in_specs=[pl.BlockSpec((tq, width), lambda i: (i, qb)),
                  pl.BlockSpec((tq, width), lambda i: (prev(i), kb)),
                  pl.BlockSpec((tq, width), lambda i: (i, kb)),
                  pl.BlockSpec((tq, width), lambda i: (prev(i), vb)),
                  pl.BlockSpec((tq, width), lambda i: (i, vb)),
                  pl.BlockSpec(bias2.shape, lambda i: (0, 0, 0))],
        out_specs=pl.BlockSpec((tq, width), lambda i: (i, 0)),
        out_shape=jax.ShapeDtypeStruct((t_prompt, width), F32),
        scratch_shapes=[pltpu.VMEM((ATT_PAST + tq, width), BF16),
                        pltpu.VMEM((width, ATT_PAST + tq), BF16)],
        compiler_params=_cparams(("arbitrary",)),
        name="attn_prompt",
    )(proj, proj, proj, proj, proj, bias2)


def _attn_sample_kernel(q_ref, kc_ref, ks_ref, vc_ref, vs_ref, bias_ref, o_ref, qpad, kwin, vtwin):
    width = q_ref.shape[1]
    qpad[0:CHUNK, :] = q_ref[...]
    qpad[CHUNK:QPAIR, :] = jnp.zeros((CHUNK, width), F32)
    kwin[0:ATT_PAST, :] = kc_ref[...].astype(BF16)
    kwin[ATT_PAST:BAND, :] = ks_ref[...].astype(BF16)
    kwin[BAND:KWIN, :] = jnp.zeros((KWIN - BAND, width), BF16)
    vtwin[:, 0:ATT_PAST] = vc_ref[...].T.astype(BF16)
    vtwin[:, ATT_PAST:KWIN] = jnp.concatenate(
        [vs_ref[...], jnp.zeros((KWIN - BAND, width), F32)], axis=0).T.astype(BF16)
    _attn_pairs(qpad, kwin, vtwin, bias_ref, o_ref, 1, lambda jj: None, CHUNK)


def _attn_sample(proj, cache_k, cache_v, bias2, t_prompt, n_seq, width):
    qb, kb, vb = 2, 3, 4
    c0 = t_prompt // CHUNK
    return pl.pallas_call(
        _attn_sample_kernel,
        grid=(n_seq,),
        in_specs=[pl.BlockSpec((CHUNK, width), lambda b: (c0 + b, qb)),
                  pl.BlockSpec((ATT_PAST, width), lambda b: (b, 0)),
                  pl.BlockSpec((CHUNK, width), lambda b: (c0 + b, kb)),
                  pl.BlockSpec((ATT_PAST, width), lambda b: (b, 0)),
                  pl.BlockSpec((CHUNK, width), lambda b: (c0 + b, vb)),
                  pl.BlockSpec(bias2.shape, lambda b: (0, 0, 0))],
        out_specs=pl.BlockSpec((CHUNK, width), lambda b: (b, 0)),
        out_shape=jax.ShapeDtypeStruct((n_seq * CHUNK, width), F32),
        scratch_shapes=[pltpu.VMEM((QPAIR, width), F32),
                        pltpu.VMEM((KWIN, width), BF16),
                        pltpu.VMEM((width, KWIN), BF16)],
        compiler_params=_cparams(("arbitrary",)),
        name="attn_sample",
    )(proj, cache_k, proj, cache_v, proj, bias2)


def _attn_bias(table):
    h = table.shape[0]
    x = np.arange(BAND + CHUNK - 1)
    rel = np.clip(BAND - 1 - x, -REL_CLIP, REL_CLIP) + REL_CLIP
    u = table[:, rel]
    std = jnp.stack([u[:, CHUNK - 1 - i:CHUNK - 1 - i + BAND] for i in range(CHUNK)], axis=1)
    neg = jnp.full((h, CHUNK, KWIN - BAND), NEG_BIG, F32)
    b = jnp.stack([jnp.concatenate([std, neg], axis=2),
                   jnp.concatenate([neg, std], axis=2)],
                  axis=1)
    b = b.reshape(h // 2, 2, 2, CHUNK, KWIN).transpose(0, 4, 1, 2, 3)
    return b.reshape(h // 2, KWIN, 2 * QPAIR)


def _outproj_kernel(y_ref, ap_ref, as_ref, xp_ref, xs_ref, gm_ref, scf_ref, shf_ref, npost_ref, npre_ref,
                    wo1_ref, wo2_ref, wrh_ref, wrl_ref, x1_ref, ht_ref, hb_ref, lg_ref, *, npt):
    i = pl.program_id(0)

    def body(a_ref, x_ref):
        tm, d = x_ref.shape
        mix = _bdot(y_ref[...].astype(BF16), wo1_ref[...]) + _bdot(a_ref[...].astype(BF16), wo2_ref[...])
        nm = _rms(mix, npost_ref[...]).reshape(tm // CHUNK, CHUNK, d)
        x1 = x_ref[...].reshape(tm // CHUNK, CHUNK, d) + gm_ref[...] * nm
        x1_ref[...] = x1.reshape(tm, d)
        hn = _rms(x1, npre_ref[...])
        hf = (hn * (1.0 + scf_ref[...]) + shf_ref[...]).reshape(tm, d)
        _store_token_tiles(ht_ref, 0, hf)
        h_hi = hf.astype(BF16)
        hb_ref[...] = h_hi
        h_lo = (hf - h_hi.astype(F32)).astype(BF16)
        lg_ref[...] = _bdot(h_hi, wrh_ref[...]) + _bdot(h_lo, wrh_ref[...]) + _bdot(h_hi, wrl_ref[...])

    @pl.when(i < npt)
    def _():
        body(ap_ref, xp_ref)

    @pl.when(i >= npt)
    def _():
        body(as_ref, xs_ref)


def _outproj(y_ssd, att_p, att_s, xp, xs, gate_m, scale_f, shift_f, npost, npre, wo1, wo2, wr):
    tp, d = xp.shape
    ts = xs.shape[0]
    t = tp + ts
    inner = y_ssd.shape[1]
    ne = wr.shape[1]
    wr_hi = wr.astype(BF16)
    wr_lo = (wr - wr_hi.astype(F32)).astype(BF16)
    tm = _pick((tp, ts), (256, 128, 64))
    nc = tm // CHUNK
    first, second = _split_rows(tp // tm)
    row = lambda i: (i, 0)
    tab = lambda i: (i, 0, 0)
    const = lambda i: (0, 0)
    return pl.pallas_call(
        functools.partial(_outproj_kernel, npt=tp // tm),
        grid=(t // tm,),
        in_specs=[pl.BlockSpec((tm, inner), row),
                  pl.BlockSpec((tm, att_p.shape[1]), first), pl.BlockSpec((tm, att_s.shape[1]), second),
                  pl.BlockSpec((tm, d), first), pl.BlockSpec((tm, d), second),
                  pl.BlockSpec((nc, 1, d), tab), pl.BlockSpec((nc, 1, d), tab), pl.BlockSpec((nc, 1, d), tab),
                  pl.BlockSpec((1, d), const), pl.BlockSpec((1, d), const),
                  pl.BlockSpec(wo1.shape, const), pl.BlockSpec(wo2.shape, const),
                  pl.BlockSpec(wr.shape, const), pl.BlockSpec(wr.shape, const)],
        out_specs=[pl.BlockSpec((tm, d), row), pl.BlockSpec((tm * (d // LANES), LANES), row),
                   pl.BlockSpec((tm, d), row), pl.BlockSpec((tm, ne), row)],
        out_shape=[jax.ShapeDtypeStruct((t, d), F32), jax.ShapeDtypeStruct((t * (d // LANES), LANES), F32),
                   jax.ShapeDtypeStruct((t, d), BF16), jax.ShapeDtypeStruct((t, ne), F32)],
        compiler_params=_cparams(("arbitrary",)),
        name="outproj",
    )(y_ssd, att_p, att_s, xp, xs, gate_m, scale_f, shift_f, npost, npre, wo1, wo2, wr_hi, wr_lo)


def _route_kernel(lg_ref, rb_ref, ti_ref, tw_ref):
    ne, tt = lg_ref.shape
    gs = ne // N_EXPERT_GROUPS
    scores = jax.nn.sigmoid(lg_ref[...])
    sel = scores + rb_ref[...]
    g3 = sel.reshape(N_EXPERT_GROUPS, gs, tt)
    i3 = lax.broadcasted_iota(I32, g3.shape, 1)
    m1 = jnp.max(g3, axis=1, keepdims=True)
    first = jnp.min(jnp.where(g3 == m1, i3, gs), axis=1, keepdims=True)
    m2 = jnp.max(jnp.where(i3 == first, -jnp.inf, g3), axis=1, keepdims=True)
    gscore = (m1 + m2).reshape(N_EXPERT_GROUPS, tt)
    gi = lax.broadcasted_iota(I32, gscore.shape, 0)
    gmask = jnp.zeros(gscore.shape, jnp.bool_)
    rem = gscore
    for _ in range(TOPK_GROUPS):
        mg = jnp.max(rem, axis=0, keepdims=True)
        pick = jnp.min(jnp.where(rem == mg, gi, N_EXPERT_GROUPS), axis=0, keepdims=True)
        hit = gi == pick
        gmask = gmask | hit
        rem = jnp.where(hit, -jnp.inf, rem)
    emask = jnp.broadcast_to(gmask.reshape(N_EXPERT_GROUPS, 1, tt), g3.shape).reshape(ne, tt)
    rem = jnp.where(emask, sel, -jnp.inf)
    ei = lax.broadcasted_iota(I32, (ne, tt), 0)
    idx, wts = [], []
    for _ in range(TOP_K):
        me = jnp.max(rem, axis=0, keepdims=True)
        pick = jnp.min(jnp.where(rem == me, ei, ne), axis=0, keepdims=True)
        hit = ei == pick
        idx.append(pick)
        wts.append(jnp.sum(jnp.where(hit, scores, 0.0), axis=0, keepdims=True))
        rem = jnp.where(hit, -jnp.inf, rem)
    w = jnp.concatenate(wts, axis=0)
    ti_ref[...] = jnp.concatenate(idx, axis=0)
    tw_ref[...] = w / jnp.sum(w, axis=0, keepdims=True) * ROUTED_SCALE


def _route(logits_t, router_bias):
    ne, t = logits_t.shape
    tt = _pick((t,), (2176, 2048, 1024, 512, 256, 128))
    return pl.pallas_call(
        _route_kernel,
        grid=(t // tt,),
        in_specs=[pl.BlockSpec((ne, tt), lambda i: (0, i)),
                  pl.BlockSpec((ne, 1), lambda i: (0, 0))],
        out_specs=[pl.BlockSpec((TOP_K, tt), lambda i: (0, i)),
                   pl.BlockSpec((TOP_K, tt), lambda i: (0, i))],
        out_shape=[jax.ShapeDtypeStruct((TOP_K, t), I32),
                   jax.ShapeDtypeStruct((TOP_K, t), F32)],
        compiler_params=_cparams(("arbitrary",)),
        name="route",
    )(logits_t, router_bias.reshape(ne, 1))


def _moe_kernel(be_ref, r0_ref, n_ref, first_ref, a_ref,
                ht_hbm, w1_ref, w3_ref, w2_ref, y8_hbm,
                xbuf_a, xbuf_b, obuf_a, obuf_b, xs_scr, w1b, w3b, w2b, gsem, ssem, *, n_tok, tme):
    del be_ref
    b = pl.program_id(0)
    nb = pl.num_programs(0)
    nbk = xs_scr.shape[1] // LANES
    spare0 = TOP_K * n_tok
    whole = tme * nbk

    def tile(ref, row):
        start = row * nbk
        return ref.at[pl.ds(start if isinstance(start, int) else pl.multiple_of(start, nbk), nbk)]

    def gather_all(xb, s):
        return pltpu.make_async_copy(ht_hbm.at[pl.ds(0, whole)], xb, gsem.at[s])

    def scatter_all(ob, s):
        return pltpu.make_async_copy(ob, y8_hbm.at[pl.ds(0, whole)], ssem.at[s])

    def start_gather(blk, xb, s):
        r0 = r0_ref[blk]
        for r in range(tme):
            pltpu.make_async_copy(tile(ht_hbm, a_ref[r0 + r] >> 3), tile(xb, r), gsem.at[s]).start()

    def start_scatter(blk, n_valid, ob, s):
        r0 = r0_ref[blk]
        for r in range(tme):
            a = a_ref[r0 + r]
            dst = jnp.where(r < n_valid, (a & (TOP_K - 1)) * n_tok + (a >> 3), spare0 + s * tme + r)
            pltpu.async_copy(tile(ob, r), tile(y8_hbm, dst), ssem.at[s], priority=1)

    @pl.when(b == 0)
    def _():
        start_gather(0, xbuf_a, 0)
        for s, ob in enumerate((obuf_a, obuf_b)):
            ob[...] = jnp.zeros_like(ob)
            spare = pltpu.make_async_copy(ob, y8_hbm.at[pl.ds((spare0 + s * tme) * nbk, whole)], ssem.at[s])
            spare.start()
            spare.wait()

    @pl.when(first_ref[b] == 1)
    def _():
        w1b[...] = w1_ref[0].astype(BF16)
        w3b[...] = w3_ref[0].astype(BF16)
        w2b[...] = w2_ref[0].astype(BF16)

    nxt = jnp.minimum(b + 1, nb - 1)
    prv = jnp.maximum(b - 1, 0)
    n_prv = jnp.where(b >= 1, n_ref[prv], 0)

    def step(s, xb_cur, xb_nxt, ob_cur, ob_prv):
        gather_all(xb_cur, s).wait()

        @pl.when(b >= 1)
        def _():
            scatter_all(ob_cur, s).wait()

        start_gather(nxt, xb_nxt, 1 - s)
        for c, blk in enumerate(_load_token_tiles(xb_cur, 0, tme, nbk)):
            xs_scr[:, c * LANES:(c + 1) * LANES] = blk.astype(BF16)
        x = xs_scr[...]
        h = (_silu(_bdot(x, w1b[...])) * _bdot(x, w3b[...])).astype(BF16)
        _store_token_tiles(ob_cur, 0, _bdot(h, w2b[...]))
        start_scatter(prv, n_prv, ob_prv, 1 - s)

        @pl.when(b == nb - 1)
        def _():
            gather_all(xb_nxt, 1 - s).wait()
            scatter_all(ob_prv, 1 - s).wait()
            start_scatter(b, n_ref[b], ob_cur, s)
            scatter_all(ob_cur, s).wait()

    @pl.when(b % 2 == 0)
    def _():
        step(0, xbuf_a, xbuf_b, obuf_a, obuf_b)

    @pl.when(b % 2 == 1)
    def _():
        step(1, xbuf_b, xbuf_a, obuf_b, obuf_a)


def _moe(ht, a_sorted, blk_e, blk_r0, blk_n, blk_first, w1, w3, w2):
    ne, d, de = w1.shape
    nbk = d // LANES
    t = ht.shape[0] // nbk
    nb = blk_e.shape[0]
    tme = MOE_ROWS
    assert (TOP_K * t) % tme == 0 and nbk % SUBLANES == 0
    wmap = lambda b, be, r0, n, f, a: (be[b], 0, 0)
    grid_spec = pltpu.PrefetchScalarGridSpec(
        num_scalar_prefetch=5,
        grid=(nb,),
        in_specs=[pl.BlockSpec(memory_space=pl.ANY),
                  pl.BlockSpec((1, d, de), wmap),
                  pl.BlockSpec((1, d, de), wmap),
                  pl.BlockSpec((1, de, d), wmap)],
        out_specs=pl.BlockSpec(memory_space=pl.ANY),
        scratch_shapes=[pltpu.VMEM((tme * nbk, LANES), F32), pltpu.VMEM((tme * nbk, LANES), F32),
                        pltpu.VMEM((tme * nbk, LANES), F32), pltpu.VMEM((tme * nbk, LANES), F32),
                        pltpu.VMEM((tme, d), BF16),
                        pltpu.VMEM((d, de), BF16), pltpu.VMEM((d, de), BF16), pltpu.VMEM((de, d), BF16),
                        pltpu.SemaphoreType.DMA((2,)), pltpu.SemaphoreType.DMA((2,))],
    )
    return pl.pallas_call(
        functools.partial(_moe_kernel, n_tok=t, tme=tme),
        grid_spec=grid_spec,
        out_shape=jax.ShapeDtypeStruct(((TOP_K * t + 2 * tme) * nbk, LANES), F32),
        compiler_params=_cparams(("arbitrary",)),
        name="moe",
    )(blk_e, blk_r0, blk_n, blk_first, a_sorted, ht, w1, w3, w2)


def _dispatch_plan(topi_t, tme, ne):
    k, t = topi_t.shape
    assert k == TOP_K
    a_cnt = k * t
    shift = int(np.ceil(np.log2(a_cnt)))
    assert ne << shift < 2 ** 31
    a_id = jnp.arange(t, dtype=I32)[None, :] * k + jnp.arange(k, dtype=I32)[:, None]
    keys = (topi_t << shift) + a_id
    a_sorted = jnp.sort(keys.reshape(-1)) & ((1 << shift) - 1)
    a_sorted = jnp.concatenate([a_sorted, jnp.zeros((tme,), I32)])
    counts = jnp.sum(topi_t.reshape(-1, 1) == jnp.arange(ne, dtype=I32)[None, :], axis=0, dtype=I32)
    starts = jnp.cumsum(counts) - counts
    nblk = (counts + tme - 1) // tme
    blk_end = jnp.cumsum(nblk)
    nb = a_cnt // tme + ne
    b = jnp.arange(nb, dtype=I32)
    valid = b < blk_end[-1]
    e_raw = jnp.minimum(jnp.sum(blk_end[None, :] <= b[:, None], axis=1, dtype=I32), ne - 1)
    onehot = (e_raw[:, None] == jnp.arange(ne, dtype=I32)[None, :]).astype(I32)
    pick = lambda v: jnp.sum(onehot * v[None, :], axis=1)
    j = b - pick(blk_end - nblk)
    blk_r0 = jnp.where(valid, pick(starts) + j * tme, 0)
    blk_n = jnp.where(valid, jnp.minimum(tme, pick(counts) - j * tme), 0)
    blk_first = (valid & (j == 0)).astype(I32)
    e_last = jnp.max(jnp.where(valid, e_raw, 0))
    blk_e = jnp.where(valid, e_raw, e_last)
    return a_sorted, blk_e, blk_r0, blk_n, blk_first


def _final_kernel(*refs, npt):
    y8_refs = refs[:TOP_K]
    tw_ref, hb_ref, x1_ref, gf_ref, npost_ref, ws1_ref, ws3_ref, ws2_ref, op_ref, os_ref = refs[TOP_K:]
    i = pl.program_id(0)
    tm, d = x1_ref.shape
    nbk = d // LANES
    tw = tw_ref[...]
    routed = None
    for k in range(TOP_K):
        wk = jnp.broadcast_to(tw[:, k:k + 1], (tm, LANES))
        part = [blk * wk for blk in _load_token_tiles(y8_refs[k], 0, tm, nbk)]
        routed = part if routed is None else [a + p for a, p in zip(routed, part)]
    hb = hb_ref[...]
    shared = _bdot((_silu(_bdot(hb, ws1_ref[...])) * _bdot(hb, ws3_ref[...])).astype(BF16), ws2_ref[...])
    nm = _rms(jnp.concatenate(routed, axis=1) + shared, npost_ref[...]).reshape(tm // CHUNK, CHUNK, d)
    y = (x1_ref[...].reshape(tm // CHUNK, CHUNK, d) + gf_ref[...] * nm).reshape(tm, d)

    @pl.when(i < npt)
    def _():
        op_ref[...] = y

    @pl.when(i >= npt)
    def _():
        os_ref[...] = y


def _final(y8, topw, hb, x1, gate_f, npost, ws1, ws3, ws2, tp):
    t, d = x1.shape
    ts = t - tp
    nbk = d // LANES
    tm = _pick((tp, ts), (128, 64))
    nc = tm // CHUNK
    nt = t // tm
    first, second = _split_rows(tp // tm)
    row = lambda i: (i, 0)
    const = lambda i: (0, 0)
    planes = [pl.BlockSpec((tm * nbk, LANES), functools.partial(lambda i, k: (k * nt + i, 0), k=k))
              for k in range(TOP_K)]
    return pl.pallas_call(
        functools.partial(_final_kernel, npt=tp // tm),
        grid=(nt,),
        in_specs=planes + [pl.BlockSpec((tm, TOP_K), row),
                           pl.BlockSpec((tm, d), row), pl.BlockSpec((tm, d), row),
                           pl.BlockSpec((nc, 1, d), lambda i: (i, 0, 0)),
                           pl.BlockSpec((1, d), const),
                           pl.BlockSpec(ws1.shape, const), pl.BlockSpec(ws3.shape, const),
                           pl.BlockSpec(ws2.shape, const)],
        out_specs=[pl.BlockSpec((tm, d), first), pl.BlockSpec((tm, d), second)],
        out_shape=[jax.ShapeDtypeStruct((tp, d), F32), jax.ShapeDtypeStruct((ts, d), F32)],
        compiler_params=_cparams(("arbitrary",)),
        name="final",
    )(*([y8] * TOP_K), topw, hb, x1, gate_f, npost, ws1, ws3, ws2)


def kernel(x_prompt, x_sample, cache_conv, state_ssm, cache_k, cache_v, c_prompt, c_sample,
           w_ada, b_ada, norm_pre_mix, norm_post_mix, norm_pre_ffn, norm_post_ffn,
           w_in, conv_w, conv_b, dt_bias, a_log, d_skip, gn_w, rel_bias, w_out,
           w_router, router_bias, w1, w3, w2, ws1, ws3, ws2):
    assert w_ada.shape[0] == 1, "single layer"
    bp, lp, d = x_prompt.shape
    bs, ls, _ = x_sample.shape
    assert bp == 1 and ls == CHUNK and lp % ATT_PAST == 0
    assert cache_k.shape[2] == ATT_PAST
    heads = a_log.shape[1]
    inner = heads * SSM_HEAD_DIM
    att_w = rel_bias.shape[1] * ATT_HEAD_DIM
    assert att_w == inner
    bcw = 2 * SSM_GROUPS * SSM_STATE
    ne = w_router.shape[2]
    tp, ts = bp * lp, bs * ls
    nseq = bp + bs

    xp, xs = x_prompt.reshape(tp, d), x_sample.reshape(ts, d)
    seq_np = np.concatenate([np.repeat(np.arange(bp), lp // CHUNK), bp + np.arange(bs)]).astype(np.int32)
    first_np = np.concatenate([[1], (seq_np[1:] != seq_np[:-1])]).astype(np.int32)
    seq_of_chunk, first_of_chunk = jnp.asarray(seq_np), jnp.asarray(first_np)

    c_all = jnp.concatenate([c_prompt, c_sample], axis=0)
    c_pad = jnp.pad(c_all, ((0, -nseq % 8), (0, 0)))
    mod = _ada(c_pad, w_ada[0], b_ada[0])[:nseq].reshape(nseq, 6, d)
    mod_c = mod[seq_of_chunk]
    shift_m, scale_m, gate_m, shift_f, scale_f, gate_f = [mod_c[:, i:i + 1, :] for i in range(6)]

    wi = w_in[0]
    o_z, o_x, o_bc = 0, inner, 2 * inner
    o_dt = inner + inner + bcw
    o_q = o_dt + heads
    o_k, o_v = o_q + att_w, o_q + 2 * att_w
    cols = lambda o, n: wi[:, o:o + n]
    w_main = jnp.concatenate([cols(o_z, inner), cols(o_x, inner), cols(o_q, att_w), cols(o_k, att_w),
                              cols(o_v, att_w), cols(o_bc, bcw)], axis=1).astype(BF16)
    w_dt = jnp.pad(cols(o_dt, heads), ((0, 0), (0, LANES - heads))).astype(BF16)
    proj, dt_raw = _inproj(xp, xs, scale_m, shift_m, norm_pre_mix, w_main, w_dt)
    c_x, c_k, c_v, c_bc = inner, 3 * inner, 4 * inner, 5 * inner

    pad_rows = lambda a: jnp.pad(a, ((0, 0), (8 - (CONV_W - 1), 0), (0, 0)))
    pre = jnp.concatenate([jnp.zeros((bp, CONV_W - 1, inner + bcw), F32), cache_conv[0]], axis=0)
    pre_x, pre_bc = pad_rows(pre[:, :, :inner]), pad_rows(pre[:, :, inner:])
    h0 = jnp.concatenate([jnp.zeros((bp,) + state_ssm.shape[2:], F32), state_ssm[0]], axis=0)
    h0t = h0.transpose(0, 3, 1, 2).reshape(nseq, SSM_STATE, inner)
    lane_pad = lambda v: jnp.pad(v, (0, LANES - heads)).reshape(1, LANES)
    expand = (np.arange(LANES)[:, None] == (np.arange(inner)[None, :] // SSM_HEAD_DIM)).astype(np.float32)
    tri = np.tril(np.ones((CHUNK, CHUNK), np.float32))
    consts = (conv_w[0][:, :inner], conv_w[0][:, inner:],
              conv_b[0][:inner].reshape(1, inner), conv_b[0][inner:].reshape(1, bcw),
              lane_pad(dt_bias[0]), lane_pad(-jnp.exp(a_log[0])),
              jnp.repeat(d_skip[0], SSM_HEAD_DIM).reshape(1, inner), gn_w[0].reshape(1, inner),
              jnp.asarray(expand, BF16), jnp.asarray(tri, BF16))
    y_ssd, st_out = _ssd(proj, dt_raw, seq_of_chunk, first_of_chunk, pre_x, pre_bc, h0t, consts, inner)

    bias2 = _attn_bias(rel_bias[0])
    att_p = _attn_prompt(proj, bias2, tp, att_w)
    att_s = _attn_sample(proj, cache_k[0].reshape(bs * ATT_PAST, att_w), cache_v[0].reshape(bs * ATT_PAST, att_w),
                         bias2, tp, bs, att_w)

    wo = w_out[0].astype(BF16)
    x1, ht, hb, logits = _outproj(y_ssd, att_p, att_s, xp, xs, gate_m, scale_f, shift_f,
                                  norm_post_mix, norm_pre_ffn, wo[:inner], wo[inner:], w_router[0])

    topi_t, topw_t = _route(logits.T, router_bias[0])
    a_sorted, blk_e, blk_r0, blk_n, blk_first = _dispatch_plan(topi_t, MOE_ROWS, ne)
    y8 = _moe(ht, a_sorted, blk_e, blk_r0, blk_n, blk_first, w1[0], w3[0], w2[0])
    y_p, y_s = _final(y8, topw_t.T, hb, x1, gate_f, norm_post_ffn,
                      ws1[0].astype(BF16), ws3[0].astype(BF16), ws2[0].astype(BF16), tp)

    tail = lambda rows: jnp.concatenate([rows[..., c_x:c_x + inner], rows[..., c_bc:c_bc + bcw]], axis=-1)
    conv_prompt = tail(proj[:tp].reshape(bp, lp, -1)[:, lp - (CONV_W - 1):, :])[None]
    conv_sample = tail(proj[tp:].reshape(bs, ls, -1)[:, ls - (CONV_W - 1):, :])[None]
    st = st_out.reshape(nseq, SSM_STATE, heads, SSM_HEAD_DIM).transpose(0, 2, 3, 1)
    keep = min(ATT_PAST, lp)
    hd = (rel_bias.shape[1], ATT_HEAD_DIM)
    kv = lambda c0, r0, r1, b, l: proj[r0:r1, c0:c0 + att_w].reshape(b, l, *hd)[None]
    return (y_p.reshape(bp, lp, d), y_s.reshape(bs, ls, d),
            conv_prompt, st[:bp][None], kv(c_k, tp - keep, tp, bp, keep), kv(c_v, tp - keep, tp, bp, keep),
            conv_sample, st[bp:][None], kv(c_k, tp, tp + ts, bs, ls), kv(c_v, tp, tp + ts, bs, ls))
```

```python
import functools

import numpy as np
import jax
import jax.numpy as jnp
from jax import lax
from jax.experimental import pallas as pl
from jax.experimental.pallas import tpu as pltpu

F32 = jnp.float32
BF16 = jnp.bfloat16
I32 = jnp.int32
HIGHEST = lax.Precision.HIGHEST

CHUNK = 64
SSM_HEAD_DIM = 64
SSM_GROUPS = 2
SSM_STATE = 128
CONV_W = 4
ATT_HEAD_DIM = 64
LEFT_CHUNKS = 8
ATT_PAST = LEFT_CHUNKS * CHUNK
BAND = ATT_PAST + CHUNK
REL_CLIP = 128
TOP_K = 8
N_EXPERT_GROUPS = 8
TOPK_GROUPS = 4
ROUTED_SCALE = 2.5
EPS = 1e-6
NEG_BIG = -1e30

LANES = 128
SUBLANES = 8
PAIR = 2 * ATT_HEAD_DIM
QPAIR = 2 * CHUNK
KWIN = ATT_PAST + QPAIR
VMEM_LIMIT = 56 * 1024 * 1024
MOE_ROWS = 256
OUT_CHUNK = 8
FINAL_ROWS = 128
ATT_UNROLL = 4


def _cparams(sem):
    return pltpu.CompilerParams(dimension_semantics=sem, vmem_limit_bytes=VMEM_LIMIT)


def _pick(ns, cands):
    for c in cands:
        if all(n % c == 0 for n in ns):
            return c
    raise ValueError(f"no tile for {ns} in {cands}")


def _silu(x):
    return x * jax.nn.sigmoid(x)


def _rms(x, g):
    ms = jnp.mean(x * x, axis=-1, keepdims=True)
    return x * lax.rsqrt(ms + EPS) * g


def _bdot(a, b):
    return jnp.dot(a, b, preferred_element_type=F32)


def _aligned(x, m):
    return x if isinstance(x, int) else pl.multiple_of(x, m)


def _split3(x):
    p0 = x.astype(BF16)
    r0 = x - p0.astype(F32)
    p1 = r0.astype(BF16)
    p2 = (r0 - p1.astype(F32)).astype(BF16)
    return p0, p1, p2


def _store_token_tiles(ref, base, x):
    rows, d = x.shape
    nb = d // LANES
    for c in range(nb):
        ref[pl.ds(base * nb + c, rows, stride=nb), :] = x[:, c * LANES:(c + 1) * LANES]


def _load_token_tiles(ref, base, rows, nb):
    return [ref[pl.ds(base * nb + c, rows, stride=nb), :] for c in range(nb)]


def _split_rows(npt):
    first = lambda i, *_: (jnp.minimum(i, npt - 1), 0)
    second = lambda i, *_: (jnp.maximum(i - npt, 0), 0)
    return first, second


def _ada_kernel(c_ref, w_ref, b_ref, o_ref):
    a = _silu(c_ref[...])
    o_ref[...] = jnp.dot(a, w_ref[...], precision=HIGHEST, preferred_element_type=F32) + b_ref[...]


def _ada(c_pad, w_ada, b_ada):
    m, d = c_pad.shape
    n = w_ada.shape[1]
    tn = _pick((n,), (1024, 512, 256, 128))
    return pl.pallas_call(
        _ada_kernel,
        grid=(n // tn,),
        in_specs=[pl.BlockSpec((m, d), lambda j: (0, 0)),
                  pl.BlockSpec((d, tn), lambda j: (0, j)),
                  pl.BlockSpec((1, tn), lambda j: (0, j))],
        out_specs=pl.BlockSpec((m, tn), lambda j: (0, j)),
        out_shape=jax.ShapeDtypeStruct((m, n), F32),
        compiler_params=_cparams(("arbitrary",)),
        name="ada",
    )(c_pad, w_ada, b_ada.reshape(1, n))


def _inproj_kernel(xp_ref, xs_ref, sc_ref, sh_ref, g_ref, w_ref, wdt_ref, o_ref, dt_ref, hm_ref, *, npt):
    i = pl.program_id(0)

    def prep(x_ref):
        x = x_ref[...]
        tm, d = x.shape
        y = _rms(x, g_ref[...]).reshape(tm // CHUNK, CHUNK, d)
        h = (y * (1.0 + sc_ref[...]) + sh_ref[...]).reshape(tm, d).astype(BF16)
        hm_ref[...] = h
        dt_ref[...] = _bdot(h, wdt_ref[...])

    @pl.when(pl.program_id(1) == 0)
    def _():
        @pl.when(i < npt)
        def _():
            prep(xp_ref)

        @pl.when(i >= npt)
        def _():
            prep(xs_ref)

    o_ref[...] = _bdot(hm_ref[...], w_ref[...])


def _inproj(xp, xs, scale, shift, g, w_main, w_dt):
    tp, d = xp.shape
    ts = xs.shape[0]
    t = tp + ts
    n = w_main.shape[1]
    tm = _pick((tp, ts), (1024, 512, 256, 128, 64))
    tn = _pick((n,), (512, 256, 128))
    nc = tm // CHUNK
    first, second = _split_rows(tp // tm)
    return pl.pallas_call(
        functools.partial(_inproj_kernel, npt=tp // tm),
        grid=(t // tm, n // tn),
        in_specs=[pl.BlockSpec((tm, d), first),
                  pl.BlockSpec((tm, d), second),
                  pl.BlockSpec((nc, 1, d), lambda i, j: (i, 0, 0)),
                  pl.BlockSpec((nc, 1, d), lambda i, j: (i, 0, 0)),
                  pl.BlockSpec((1, d), lambda i, j: (0, 0)),
                  pl.BlockSpec((d, tn), lambda i, j: (0, j)),
                  pl.BlockSpec((d, LANES), lambda i, j: (0, 0))],
        out_specs=[pl.BlockSpec((tm, tn), lambda i, j: (i, j)),
                   pl.BlockSpec((tm, LANES), lambda i, j: (i, 0))],
        out_shape=[jax.ShapeDtypeStruct((t, n), F32),
                   jax.ShapeDtypeStruct((t, LANES), F32)],
        scratch_shapes=[pltpu.VMEM((tm, d), BF16)],
        compiler_params=_cparams(("arbitrary", "arbitrary")),
        name="inproj",
    )(xp, xs, scale, shift, g, w_main, w_dt)


def _ssd_kernel(seq_ref, first_ref,
                z_ref, xs_ref, bc_ref, dt_ref, prex_ref, prebc_ref, h0_ref,
                cwx_ref, cwbc_ref, cbx_ref, cbbc_ref, dtb_ref, aneg_ref, dsk_ref, gnw_ref,
                e_ref, tri_ref,
                y_ref, st_out_ref,
                xpx_scr, xpbc_scr, st_scr):
    del seq_ref
    c = pl.program_id(0)
    inner = xs_ref.shape[1]
    gw = inner // SSM_GROUPS
    n = SSM_STATE
    pad = 8

    @pl.when(first_ref[c] == 1)
    def _():
        xpx_scr[0:pad, :] = prex_ref[0]
        xpbc_scr[0:pad, :] = prebc_ref[0]
        st_scr[...] = h0_ref[0]

    xpx_scr[pad:pad + CHUNK, :] = xs_ref[...]
    xpbc_scr[pad:pad + CHUNK, :] = bc_ref[...]

    def conv(xp, w_ref, b_ref):
        base = pad - (CONV_W - 1)
        acc = b_ref[...] + xp[base:base + CHUNK, :] * w_ref[0:1, :]
        for k in range(1, CONV_W):
            acc = acc + xp[base + k:base + k + CHUNK, :] * w_ref[k:k + 1, :]
        return _silu(acc)

    xs = conv(xpx_scr, cwx_ref, cbx_ref)
    bc = conv(xpbc_scr, cwbc_ref, cbbc_ref)
    xpx_scr[0:pad, :] = xpx_scr[CHUNK:CHUNK + pad, :]
    xpbc_scr[0:pad, :] = xpbc_scr[CHUNK:CHUNK + pad, :]

    dtv = dt_ref[...] + dtb_ref[...]
    dt = jnp.maximum(dtv, 0.0) + jnp.log(1.0 + jnp.exp(-jnp.abs(dtv)))
    da = dt * aneg_ref[...]
    acs = sum(_bdot(tri_ref[...], p) for p in _split3(da))
    full = sum(_bdot(p, e_ref[...]) for p in _split3(jnp.concatenate([dt, acs], axis=0)))
    dtf = full[0:CHUNK]
    af = full[CHUNK:2 * CHUNK]

    row = lax.broadcasted_iota(I32, (CHUNK, inner), 0)
    lj = lax.broadcasted_iota(I32, (CHUNK, inner), 1) & (SSM_HEAD_DIM - 1)
    aj = jnp.sum(jnp.where(row == lj, af, 0.0), axis=0, keepdims=True)
    lmat = jnp.exp(jnp.where(row >= lj, af - aj, NEG_BIG))
    alast = af[CHUNK - 1:CHUNK, :]
    xdt = xs * dtf
    xw = xdt * jnp.exp(alast - af)
    cdec = jnp.exp(alast)
    eaf = jnp.exp(af)

    lane = lax.broadcasted_iota(I32, (CHUNK, PAIR), 1)
    st = st_scr[...]
    ydiag, yoff, stn = [], [], []
    for g in range(SSM_GROUPS):
        bg = bc[:, g * n:(g + 1) * n].astype(BF16)
        cg = bc[:, (SSM_GROUPS + g) * n:(SSM_GROUPS + g + 1) * n].astype(BF16)
        bb = jnp.concatenate([bg, bg], axis=0)
        cbb = lax.dot_general(cg, bb, (((1,), (1,)), ((), ())), preferred_element_type=F32)
        stg = st[:, g * gw:(g + 1) * gw]
        yoff.append(_bdot(cg, stg.astype(BF16)))
        for p in range(gw // PAIR):
            lo = g * gw + p * PAIR
            m = (cbb * lmat[:, lo:lo + PAIR]).astype(BF16)
            xd = xdt[:, lo:lo + PAIR]
            w = jnp.concatenate([jnp.where(lane < SSM_HEAD_DIM, xd, 0.0),
                                 jnp.where(lane >= SSM_HEAD_DIM, xd, 0.0)], axis=0).astype(BF16)
            ydiag.append(_bdot(m, w))
        upd = lax.dot_general(bg, xw[:, g * gw:(g + 1) * gw].astype(BF16),
                              (((0,), (0,)), ((), ())), preferred_element_type=F32)
        stn.append(cdec[:, g * gw:(g + 1) * gw] * stg + upd)

    y = jnp.concatenate(ydiag, axis=1) + jnp.concatenate(yoff, axis=1) * eaf + dsk_ref[...] * xs
    y = y * _silu(z_ref[...])
    y_ref[...] = _rms(y, gnw_ref[...])
    st_new = jnp.concatenate(stn, axis=1)
    st_scr[...] = st_new
    st_out_ref[0] = st_new


def _ssd(proj, dt_raw, seq_of_chunk, first_of_chunk, pre_x, pre_bc, h0t, consts, inner):
    t = proj.shape[0]
    nch = t // CHUNK
    nseq = h0t.shape[0]
    bcw = pre_bc.shape[-1]
    assert (5 * inner) % bcw == 0
    cmap = lambda blk: (lambda c, s, f: (c, blk))
    smap3 = lambda c, s, f: (s[c], 0, 0)
    const2 = lambda c, s, f: (0, 0)
    grid_spec = pltpu.PrefetchScalarGridSpec(
        num_scalar_prefetch=2,
        grid=(nch,),
        in_specs=[pl.BlockSpec((CHUNK, inner), cmap(0)),
                  pl.BlockSpec((CHUNK, inner), cmap(1)),
                  pl.BlockSpec((CHUNK, bcw), cmap((5 * inner) // bcw)),
                  pl.BlockSpec((CHUNK, LANES), lambda c, s, f: (c, 0)),
                  pl.BlockSpec((1, 8, inner), smap3),
                  pl.BlockSpec((1, 8, bcw), smap3),
                  pl.BlockSpec((1, SSM_STATE, inner), smap3)]
                 + [pl.BlockSpec(a.shape, const2) for a in consts],
        out_specs=[pl.BlockSpec((CHUNK, inner), lambda c, s, f: (c, 0)),
                   pl.BlockSpec((1, SSM_STATE, inner), smap3)],
        scratch_shapes=[pltpu.VMEM((CHUNK + 8, inner), F32),
                        pltpu.VMEM((CHUNK + 8, bcw), F32),
                        pltpu.VMEM((SSM_STATE, inner), F32)],
    )
    return pl.pallas_call(
        _ssd_kernel,
        grid_spec=grid_spec,
        out_shape=[jax.ShapeDtypeStruct((t, inner), F32),
                   jax.ShapeDtypeStruct((nseq, SSM_STATE, inner), F32)],
        compiler_params=_cparams(("arbitrary",)),
        name="ssd",
    )(seq_of_chunk, first_of_chunk, proj, proj, proj, dt_raw, pre_x, pre_bc, h0t, *consts)


def _attn_pairs(q_ref, kwin, vtwin, bias_ref, o_ref, n_steps, n_masked_fn, out_rows):
    n_pairs = q_ref.shape[1] // PAIR
    rowp = lax.broadcasted_iota(I32, (PAIR, QPAIR), 0)
    krow = lax.broadcasted_iota(I32, (KWIN, 2 * QPAIR), 0)

    for jj in range(n_steps):
        n_masked = n_masked_fn(jj)

        def one_pair(hp, jj=jj, n_masked=n_masked):
            lo = pl.multiple_of(hp * PAIR, PAIR)
            q = q_ref[jj * QPAIR:(jj + 1) * QPAIR, pl.ds(lo, PAIR)] * (ATT_HEAD_DIM ** -0.5)
            qt = q.T
            w = jnp.concatenate([jnp.where(rowp < ATT_HEAD_DIM, qt, 0.0),
                                 jnp.where(rowp >= ATT_HEAD_DIM, qt, 0.0)], axis=1).astype(BF16)
            kb = kwin[jj * QPAIR:jj * QPAIR + KWIN, pl.ds(lo, PAIR)]
            s = _bdot(kb, w) + bias_ref[hp]
            if n_masked is not None:
                s = jnp.where(krow < n_masked, NEG_BIG, s)
            mx = jnp.max(s, axis=0, keepdims=True)
            p = jnp.exp(s - mx)
            den = jnp.sum(p, axis=0, keepdims=True)
            vb = vtwin[pl.ds(lo, PAIR), jj * QPAIR:jj * QPAIR + KWIN]
            o2 = _bdot(vb, p.astype(BF16)) / den
            ot = jnp.where(rowp < ATT_HEAD_DIM, o2[:, 0:QPAIR], o2[:, QPAIR:2 * QPAIR])
            o_ref[jj * out_rows:(jj + 1) * out_rows, pl.ds(lo, PAIR)] = ot.T[0:out_rows]

        def body(i, carry, one_pair=one_pair):
            for u in range(ATT_UNROLL):
                one_pair(i * ATT_UNROLL + u)
            return carry

        lax.fori_loop(0, n_pairs // ATT_UNROLL, body, 0)


def _attn_prompt_kernel(q_ref, kp_ref, kc_ref, vp_ref, vc_ref, bias_ref, o_ref, kwin, vtwin):
    i = pl.program_id(0)
    tq = q_ref.shape[0]
    kwin[0:ATT_PAST, :] = kp_ref[...].astype(BF16)
    kwin[ATT_PAST:ATT_PAST + tq, :] = kc_ref[...].astype(BF16)
    vtwin[:, 0:ATT_PAST] = vp_ref[...].T.astype(BF16)
    vtwin[:, ATT_PAST:ATT_PAST + tq] = vc_ref[...].T.astype(BF16)
    _attn_pairs(q_ref, kwin, vtwin, bias_ref, o_ref, tq // QPAIR,
                lambda jj: ATT_PAST - jj * QPAIR - i * tq, QPAIR)


def _attn_prompt(proj, bias2, t_prompt, width):
    tq = ATT_PAST
    assert t_prompt % tq == 0
    qb, kb, vb = 2, 3, 4
    prev = lambda i: jnp.maximum(i - 1, 0)
    return pl.pallas_call(
        _attn_prompt_kernel,
        grid=(t_prompt // tq,),
        in_specs=[pl.BlockSpec((tq, width), lambda i: (i, qb)),
                  pl.BlockSpec((tq, width), lambda i: (prev(i), kb)),
                  pl.BlockSpec((tq, width), lambda i: (i, kb)),
                  pl.BlockSpec((tq, width), lambda i: (prev(i), vb)),
                  pl.BlockSpec((tq, width), lambda i: (i, vb)),
                  pl.BlockSpec(bias2.shape, lambda i: (0, 0, 0))],
        out_specs=pl.BlockSpec((tq, width), lambda i: (i, 0)),
        out_shape=jax.ShapeDtypeStruct((t_prompt, width), F32),
        scratch_shapes=[pltpu.VMEM((ATT_PAST + tq, width), BF16),
                        pltpu.VMEM((width, ATT_PAST + tq), BF16)],
        compiler_params=_cparams(("arbitrary",)),
        name="attn_prompt",
    )(proj, proj, proj, proj, proj, bias2)


def _attn_sample_kernel(q_ref, kc_ref, ks_ref, vc_ref, vs_ref, bias_ref, o_ref, qpad, kwin, vtwin):
    width = q_ref.shape[1]
    qpad[0:CHUNK, :] = q_ref[...]
    qpad[CHUNK:QPAIR, :] = jnp.zeros((CHUNK, width), F32)
    kwin[0:ATT_PAST, :] = kc_ref[...].astype(BF16)
    kwin[ATT_PAST:BAND, :] = ks_ref[...].astype(BF16)
    kwin[BAND:KWIN, :] = jnp.zeros((KWIN - BAND, width), BF16)
    vtwin[:, 0:ATT_PAST] = vc_ref[...].T.astype(BF16)
    vtwin[:, ATT_PAST:KWIN] = jnp.concatenate(
        [vs_ref[...], jnp.zeros((KWIN - BAND, width), F32)], axis=0).T.astype(BF16)
    _attn_pairs(qpad, kwin, vtwin, bias_ref, o_ref, 1, lambda jj: None, CHUNK)


def _attn_sample(proj, cache_k, cache_v, bias2, t_prompt, n_seq, width):
    qb, kb, vb = 2, 3, 4
    c0 = t_prompt // CHUNK
    return pl.pallas_call(
        _attn_sample_kernel,
        grid=(n_seq,),
        in_specs=[pl.BlockSpec((CHUNK, width), lambda b: (c0 + b, qb)),
                  pl.BlockSpec((ATT_PAST, width), lambda b: (b, 0)),
                  pl.BlockSpec((CHUNK, width), lambda b: (c0 + b, kb)),
                  pl.BlockSpec((ATT_PAST, width), lambda b: (b, 0)),
                  pl.BlockSpec((CHUNK, width), lambda b: (c0 + b, vb)),
                  pl.BlockSpec(bias2.shape, lambda b: (0, 0, 0))],
        out_specs=pl.BlockSpec((CHUNK, width), lambda b: (b, 0)),
        out_shape=jax.ShapeDtypeStruct((n_seq * CHUNK, width), F32),
        scratch_shapes=[pltpu.VMEM((QPAIR, width), F32),
                        pltpu.VMEM((KWIN, width), BF16),
                        pltpu.VMEM((width, KWIN), BF16)],
        compiler_params=_cparams(("arbitrary",)),
        name="attn_sample",
    )(proj, cache_k, proj, cache_v, proj, bias2)


def _attn_bias(table):
    h = table.shape[0]
    x = np.arange(BAND + CHUNK - 1)
    rel = np.clip(BAND - 1 - x, -REL_CLIP, REL_CLIP) + REL_CLIP
    u = table[:, rel]
    std = jnp.stack([u[:, CHUNK - 1 - i:CHUNK - 1 - i + BAND] for i in range(CHUNK)], axis=1)
    neg = jnp.full((h, CHUNK, KWIN - BAND), NEG_BIG, F32)
    b = jnp.stack([jnp.concatenate([std, neg], axis=2),
                   jnp.concatenate([neg, std], axis=2)],
                  axis=1)
    b = b.reshape(h // 2, 2, 2, CHUNK, KWIN).transpose(0, 4, 1, 2, 3)
    return b.reshape(h // 2, KWIN, 2 * QPAIR)


def _outproj_kernel(y_ref, ap_ref, as_ref, xp_ref, xs_ref, gm_ref, scf_ref, shf_ref, npost_ref, npre_ref,
                    wo1_ref, wo2_ref, wrh_ref, wrl_ref, x1_ref, ht_ref, hb_ref, lg_ref, *, npt):
    i = pl.program_id(0)

    def body(a_ref, x_ref):
        tm, d = x_ref.shape
        mix = _bdot(y_ref[...].astype(BF16), wo1_ref[...]) + _bdot(a_ref[...].astype(BF16), wo2_ref[...])
        nm = _rms(mix, npost_ref[...]).reshape(tm // CHUNK, CHUNK, d)
        x1 = x_ref[...].reshape(tm // CHUNK, CHUNK, d) + gm_ref[...] * nm
        x1_ref[...] = x1.reshape(tm, d)
        hn = _rms(x1, npre_ref[...])
        hf = (hn * (1.0 + scf_ref[...]) + shf_ref[...]).reshape(tm, d)
        _store_token_tiles(ht_ref, 0, hf)
        h_hi = hf.astype(BF16)
        hb_ref[...] = h_hi
        h_lo = (hf - h_hi.astype(F32)).astype(BF16)
        lg_ref[...] = _bdot(h_hi, wrh_ref[...]) + _bdot(h_lo, wrh_ref[...]) + _bdot(h_hi, wrl_ref[...])

    @pl.when(i < npt)
    def _():
        body(ap_ref, xp_ref)

    @pl.when(i >= npt)
    def _():
        body(as_ref, xs_ref)


def _outproj(y_ssd, att_p, att_s, xp, xs, gate_m, scale_f, shift_f, npost, npre, wo1, wo2, wr):
    tp, d = xp.shape
    ts = xs.shape[0]
    t = tp + ts
    inner = y_ssd.shape[1]
    ne = wr.shape[1]
    wr_hi = wr.astype(BF16)
    wr_lo = (wr - wr_hi.astype(F32)).astype(BF16)
    tm = _pick((tp, ts), (256, 128, 64))
    nc = tm // CHUNK
    first, second = _split_rows(tp // tm)
    row = lambda i: (i, 0)
    tab = lambda i: (i, 0, 0)
    const = lambda i: (0, 0)
    return pl.pallas_call(
        functools.partial(_outproj_kernel, npt=tp // tm),
        grid=(t // tm,),
        in_specs=[pl.BlockSpec((tm, inner), row),
                  pl.BlockSpec((tm, att_p.shape[1]), first), pl.BlockSpec((tm, att_s.shape[1]), second),
                  pl.BlockSpec((tm, d), first), pl.BlockSpec((tm, d), second),
                  pl.BlockSpec((nc, 1, d), tab), pl.BlockSpec((nc, 1, d), tab), pl.BlockSpec((nc, 1, d), tab),
                  pl.BlockSpec((1, d), const), pl.BlockSpec((1, d), const),
                  pl.BlockSpec(wo1.shape, const), pl.BlockSpec(wo2.shape, const),
                  pl.BlockSpec(wr.shape, const), pl.BlockSpec(wr.shape, const)],
        out_specs=[pl.BlockSpec((tm, d), row), pl.BlockSpec((tm * (d // LANES), LANES), row),
                   pl.BlockSpec((tm, d), row), pl.BlockSpec((tm, ne), row)],
        out_shape=[jax.ShapeDtypeStruct((t, d), F32), jax.ShapeDtypeStruct((t * (d // LANES), LANES), F32),
                   jax.ShapeDtypeStruct((t, d), BF16), jax.ShapeDtypeStruct((t, ne), F32)],
        compiler_params=_cparams(("arbitrary",)),
        name="outproj",
    )(y_ssd, att_p, att_s, xp, xs, gate_m, scale_f, shift_f, npost, npre, wo1, wo2, wr_hi, wr_lo)


def _route_kernel(lg_ref, rb_ref, ti_ref, tw_ref):
    ne, tt = lg_ref.shape
    gs = ne // N_EXPERT_GROUPS
    scores = jax.nn.sigmoid(lg_ref[...])
    sel = scores + rb_ref[...]
    g3 = sel.reshape(N_EXPERT_GROUPS, gs, tt)
    i3 = lax.broadcasted_iota(I32, g3.shape, 1)
    m1 = jnp.max(g3, axis=1, keepdims=True)
    first = jnp.min(jnp.where(g3 == m1, i3, gs), axis=1, keepdims=True)
    m2 = jnp.max(jnp.where(i3 == first, -jnp.inf, g3), axis=1, keepdims=True)
    gscore = (m1 + m2).reshape(N_EXPERT_GROUPS, tt)
    gi = lax.broadcasted_iota(I32, gscore.shape, 0)
    gmask = jnp.zeros(gscore.shape, jnp.bool_)
    rem = gscore
    for _ in range(TOPK_GROUPS):
        mg = jnp.max(rem, axis=0, keepdims=True)
        pick = jnp.min(jnp.where(rem == mg, gi, N_EXPERT_GROUPS), axis=0, keepdims=True)
        hit = gi == pick
        gmask = gmask | hit
        rem = jnp.where(hit, -jnp.inf, rem)
    emask = jnp.broadcast_to(gmask.reshape(N_EXPERT_GROUPS, 1, tt), g3.shape).reshape(ne, tt)
    rem = jnp.where(emask, sel, -jnp.inf)
    ei = lax.broadcasted_iota(I32, (ne, tt), 0)
    idx, wts = [], []
    for _ in range(TOP_K):
        me = jnp.max(rem, axis=0, keepdims=True)
        pick = jnp.min(jnp.where(rem == me, ei, ne), axis=0, keepdims=True)
        hit = ei == pick
        idx.append(pick)
        wts.append(jnp.sum(jnp.where(hit, scores, 0.0), axis=0, keepdims=True))
        rem = jnp.where(hit, -jnp.inf, rem)
    w = jnp.concatenate(wts, axis=0)
    ti_ref[...] = jnp.concatenate(idx, axis=0)
    tw_ref[...] = w / jnp.sum(w, axis=0, keepdims=True) * ROUTED_SCALE


def _route(logits_t, router_bias):
    ne, t = logits_t.shape
    tt = _pick((t,), (2176, 2048, 1024, 512, 256, 128))
    return pl.pallas_call(
        _route_kernel,
        grid=(t // tt,),
        in_specs=[pl.BlockSpec((ne, tt), lambda i: (0, i)),
                  pl.BlockSpec((ne, 1), lambda i: (0, 0))],
        out_specs=[pl.BlockSpec((TOP_K, tt), lambda i: (0, i)),
                   pl.BlockSpec((TOP_K, tt), lambda i: (0, i))],
        out_shape=[jax.ShapeDtypeStruct((TOP_K, t), I32),
                   jax.ShapeDtypeStruct((TOP_K, t), F32)],
        compiler_params=_cparams(("arbitrary",)),
        name="route",
    )(logits_t, router_bias.reshape(ne, 1))


def _moe_kernel(be_ref, r0_ref, n_ref, first_ref, o0_ref, a_ref,
                ht_hbm, w1_ref, w3_ref, w2_ref, os_hbm,
                xbuf_a, xbuf_b, obuf_a, obuf_b, xs_scr, w1b, w3b, w2b, gsem, ssem, *, n_rows, tme):
    del be_ref
    b = pl.program_id(0)
    nb = pl.num_programs(0)
    nbk = xs_scr.shape[1] // LANES
    whole = tme * nbk
    chunk = OUT_CHUNK * nbk

    def tile(ref, row):
        return ref.at[pl.ds(_aligned(row * nbk, nbk), nbk)]

    def gather_all(xb, s):
        return pltpu.make_async_copy(ht_hbm.at[pl.ds(0, whole)], xb, gsem.at[s])

    def start_gather(blk, xb, s):
        r0 = r0_ref[blk]
        for r in range(tme):
            pltpu.make_async_copy(tile(ht_hbm, a_ref[r0 + r] >> 3), tile(xb, r), gsem.at[s]).start()

    def n_chunks(n_valid):
        return (n_valid + (OUT_CHUNK - 1)) // OUT_CHUNK

    def out_chunk(ob, c, o0, s):
        src = ob.at[pl.ds(_aligned(c * chunk, chunk), chunk)]
        dst = os_hbm.at[pl.ds(_aligned((o0 + c * OUT_CHUNK) * nbk, chunk), chunk)]
        return pltpu.make_async_copy(src, dst, ssem.at[s])

    def start_write(n_valid, o0, ob, s):
        def body(c, carry):
            out_chunk(ob, c, o0, s).start()
            return carry

        lax.fori_loop(0, n_chunks(n_valid), body, 0)

    def wait_write(n_valid, ob, s):
        def body(c, carry):
            out_chunk(ob, 0, 0, s).wait()
            return carry

        lax.fori_loop(0, n_chunks(n_valid), body, 0)

    @pl.when(b == 0)
    def _():
        start_gather(0, xbuf_a, 0)
        for s, ob in enumerate((obuf_a, obuf_b)):
            ob[...] = jnp.zeros_like(ob)
            tail = pltpu.make_async_copy(ob, os_hbm.at[pl.ds((n_rows + s * tme) * nbk, whole)], ssem.at[s])
            tail.start()
            tail.wait()

    @pl.when(first_ref[b] == 1)
    def _():
        w1b[...] = w1_ref[0].astype(BF16)
        w3b[...] = w3_ref[0].astype(BF16)
        w2b[...] = w2_ref[0].astype(BF16)

    nxt = jnp.minimum(b + 1, nb - 1)
    prv = jnp.maximum(b - 1, 0)
    n_prv = jnp.where(b >= 1, n_ref[prv], 0)
    n_prv2 = jnp.where(b >= 2, n_ref[jnp.maximum(b - 2, 0)], 0)

    def step(s, xb_cur, xb_nxt, ob_cur, ob_prv):
        gather_all(xb_cur, s).wait()
        wait_write(n_prv2, ob_cur, s)
        start_gather(nxt, xb_nxt, 1 - s)
        for c, blk in enumerate(_load_token_tiles(xb_cur, 0, tme, nbk)):
            xs_scr[:, c * LANES:(c + 1) * LANES] = blk.astype(BF16)
        x = xs_scr[...]
        h = (_silu(_bdot(x, w1b[...])) * _bdot(x, w3b[...])).astype(BF16)
        _store_token_tiles(ob_cur, 0, _bdot(h, w2b[...]))
        start_write(n_prv, o0_ref[prv], ob_prv, 1 - s)

        @pl.when(b == nb - 1)
        def _():
            gather_all(xb_nxt, 1 - s).wait()
            wait_write(n_prv, ob_prv, 1 - s)
            start_write(n_ref[b], o0_ref[b], ob_cur, s)
            wait_write(n_ref[b], ob_cur, s)

    @pl.when(b % 2 == 0)
    def _():
        step(0, xbuf_a, xbuf_b, obuf_a, obuf_b)

    @pl.when(b % 2 == 1)
    def _():
        step(1, xbuf_b, xbuf_a, obuf_b, obuf_a)


def _moe(ht, a_sorted, blk_e, blk_r0, blk_n, blk_first, blk_o0, w1, w3, w2):
    ne, d, de = w1.shape
    nbk = d // LANES
    t = ht.shape[0] // nbk
    nb = blk_e.shape[0]
    tme = MOE_ROWS
    n_rows = TOP_K * t
    assert n_rows % tme == 0 and nbk % SUBLANES == 0 and ne * (OUT_CHUNK - 1) + OUT_CHUNK <= 2 * tme
    wmap = lambda b, be, r0, n, f, o, a: (be[b], 0, 0)
    grid_spec = pltpu.PrefetchScalarGridSpec(
        num_scalar_prefetch=6,
        grid=(nb,),
        in_specs=[pl.BlockSpec(memory_space=pl.ANY),
                  pl.BlockSpec((1, d, de), wmap),
                  pl.BlockSpec((1, d, de), wmap),
                  pl.BlockSpec((1, de, d), wmap)],
        out_specs=pl.BlockSpec(memory_space=pl.ANY),
        scratch_shapes=[pltpu.VMEM((tme * nbk, LANES), F32), pltpu.VMEM((tme * nbk, LANES), F32),
                        pltpu.VMEM((tme * nbk, LANES), F32), pltpu.VMEM((tme * nbk, LANES), F32),
                        pltpu.VMEM((tme, d), BF16),
                        pltpu.VMEM((d, de), BF16), pltpu.VMEM((d, de), BF16), pltpu.VMEM((de, d), BF16),
                        pltpu.SemaphoreType.DMA((2,)), pltpu.SemaphoreType.DMA((2,))],
    )
    return pl.pallas_call(
        functools.partial(_moe_kernel, n_rows=n_rows, tme=tme),
        grid_spec=grid_spec,
        out_shape=jax.ShapeDtypeStruct(((n_rows + 2 * tme) * nbk, LANES), F32),
        compiler_params=_cparams(("arbitrary",)),
        name="moe",
    )(blk_e, blk_r0, blk_n, blk_first, blk_o0, a_sorted, ht, w1, w3, w2)


def _dispatch_plan(topi_t, tme, ne, tmf):
    k, t = topi_t.shape
    assert k == TOP_K
    a_cnt = k * t
    shift = int(np.ceil(np.log2(a_cnt)))
    assert ne << shift < 2 ** 31
    a_id = jnp.arange(t, dtype=I32)[None, :] * k + jnp.arange(k, dtype=I32)[:, None]
    keys = (topi_t << shift) + a_id
    a_sorted = jnp.sort(keys.reshape(-1)) & ((1 << shift) - 1)
    a_sorted = jnp.concatenate([a_sorted, jnp.zeros((tme,), I32)])
    chose = jnp.sum(topi_t[:, :, None] == jnp.arange(ne, dtype=I32)[None, None, :], axis=0, dtype=I32)
    hist = jnp.sum(chose.reshape(t // tmf, tmf, ne), axis=1)
    counts = jnp.sum(hist, axis=0)
    starts = jnp.cumsum(counts) - counts
    pcounts = (counts + OUT_CHUNK - 1) // OUT_CHUNK * OUT_CHUNK
    pstarts = jnp.cumsum(pcounts) - pcounts
    strip_start = (pstarts[None, :] + jnp.cumsum(hist, axis=0) - hist).reshape(-1)
    strip_len = hist.reshape(-1)
    nblk = (counts + tme - 1) // tme
    blk_end = jnp.cumsum(nblk)
    nb = a_cnt // tme + ne
    b = jnp.arange(nb, dtype=I32)
    valid = b < blk_end[-1]
    e_raw = jnp.minimum(jnp.sum(blk_end[None, :] <= b[:, None], axis=1, dtype=I32), ne - 1)
    onehot = (e_raw[:, None] == jnp.arange(ne, dtype=I32)[None, :]).astype(I32)
    pick = lambda v: jnp.sum(onehot * v[None, :], axis=1)
    j = b - pick(blk_end - nblk)
    blk_r0 = jnp.where(valid, pick(starts) + j * tme, 0)
    blk_n = jnp.where(valid, jnp.minimum(tme, pick(counts) - j * tme), 0)
    blk_first = (valid & (j == 0)).astype(I32)
    e_last = jnp.max(jnp.where(valid, e_raw, 0))
    blk_e = jnp.where(valid, e_raw, e_last)
    blk_o0 = jnp.where(valid, pick(pstarts) + j * tme, 0)
    return a_sorted, (blk_e, blk_r0, blk_n, blk_first, blk_o0), (strip_start, strip_len)


def _final_kernel(ss_ref, sl_ref,
                  os_hbm, ti_ref, tw_ref, hb_ref, x1_ref, gf_ref, npost_ref, ws1_ref, ws3_ref, ws2_ref,
                  op_ref, osm_ref, stage_a, stage_b, stg, cnt, sem, *, npt, ne):
    i = pl.program_id(0)
    nt = pl.num_programs(0)
    tm, d = x1_ref.shape
    nbk = d // LANES
    cap = stg.shape[0]
    chunk = OUT_CHUNK * nbk

    def chunk_copy(stage, src_row, dst_row, s):
        return pltpu.make_async_copy(os_hbm.at[pl.ds(_aligned(src_row * nbk, nbk), chunk)],
                                     stage.at[pl.ds(_aligned(dst_row * nbk, chunk), chunk)], sem.at[s])

    def start_strips(tile, stage, s):
        def per_expert(e, off):
            first = ss_ref[tile * ne + e]
            nch = (sl_ref[tile * ne + e] + (OUT_CHUNK - 1)) // OUT_CHUNK

            def per_chunk(c, carry):
                chunk_copy(stage, first + c * OUT_CHUNK, off + c * OUT_CHUNK, s).start()
                return carry

            lax.fori_loop(0, nch, per_chunk, 0)
            return off + nch * OUT_CHUNK

        cnt[s] = lax.fori_loop(0, ne, per_expert, 0) // OUT_CHUNK

    def wait_strips(stage, s):
        def body(c, carry):
            chunk_copy(stage, 0, 0, s).wait()
            return carry

        lax.fori_loop(0, cnt[s], body, 0)

    @pl.when(i == 0)
    def _():
        stage_a[...] = jnp.zeros_like(stage_a)
        stage_b[...] = jnp.zeros_like(stage_b)
        start_strips(0, stage_a, 0)

    def step(s, stage_cur, stage_nxt):
        @pl.when(i + 1 < nt)
        def _():
            start_strips(i + 1, stage_nxt, 1 - s)

        wait_strips(stage_cur, s)
        for c in range(nbk):
            stg[:, c * LANES:(c + 1) * LANES] = stage_cur[pl.ds(c, cap, stride=nbk), :].astype(BF16)

    @pl.when(i % 2 == 0)
    def _():
        step(0, stage_a, stage_b)

    @pl.when(i % 2 == 1)
    def _():
        step(1, stage_b, stage_a)

    ti = ti_ref[...]
    tw = tw_ref[...]
    lane_e = lax.broadcasted_iota(I32, (tm, ne), 1)
    chose = jnp.zeros((tm, ne), F32)
    for k in range(TOP_K):
        chose = chose + (lane_e == ti[:, k:k + 1]).astype(F32)
    lens = jnp.sum(chose, axis=0, keepdims=True)
    plen = jnp.ceil(lens * (1.0 / OUT_CHUNK)) * OUT_CHUNK
    before = (lax.broadcasted_iota(I32, (ne, ne), 0) < lax.broadcasted_iota(I32, (ne, ne), 1)).astype(BF16)
    off = _bdot(jnp.broadcast_to(plen, (SUBLANES, ne)).astype(BF16), before)[0:1, :]
    earlier = (lax.broadcasted_iota(I32, (tm, tm), 1) < lax.broadcasted_iota(I32, (tm, tm), 0)).astype(BF16)
    pos = off + _bdot(earlier, chose.astype(BF16))
    lane_s = lax.broadcasted_iota(I32, (tm, cap), 1).astype(F32)
    wt = jnp.zeros((tm, cap), F32)
    for k in range(TOP_K):
        pk = jnp.sum(jnp.where(lane_e == ti[:, k:k + 1], pos, 0.0), axis=1, keepdims=True)
        wt = wt + jnp.where(lane_s == pk, tw[:, k:k + 1], 0.0)
    routed = _bdot(wt.astype(BF16), stg[...])

    hb = hb_ref[...]
    shared = _bdot((_silu(_bdot(hb, ws1_ref[...])) * _bdot(hb, ws3_ref[...])).astype(BF16), ws2_ref[...])
    nm = _rms(routed + shared, npost_ref[...]).reshape(tm // CHUNK, CHUNK, d)
    y = (x1_ref[...].reshape(tm // CHUNK, CHUNK, d) + gf_ref[...] * nm).reshape(tm, d)

    @pl.when(i < npt)
    def _():
        op_ref[...] = y

    @pl.when(i >= npt)
    def _():
        osm_ref[...] = y


def _final(osort, strips, topi, topw, hb, x1, gate_f, npost, ws1, ws3, ws2, tp, tm, ne):
    t, d = x1.shape
    ts = t - tp
    nbk = d // LANES
    assert tp % tm == 0 and ts % tm == 0
    nc = tm // CHUNK
    nt = t // tm
    cap = -(-(tm * TOP_K + ne * (OUT_CHUNK - 1)) // LANES) * LANES
    first, second = _split_rows(tp // tm)
    row = lambda i, *_: (i, 0)
    const = lambda i, *_: (0, 0)
    grid_spec = pltpu.PrefetchScalarGridSpec(
        num_scalar_prefetch=2,
        grid=(nt,),
        in_specs=[pl.BlockSpec(memory_space=pl.ANY),
                  pl.BlockSpec((tm, TOP_K), row), pl.BlockSpec((tm, TOP_K), row),
                  pl.BlockSpec((tm, d), row), pl.BlockSpec((tm, d), row),
                  pl.BlockSpec((nc, 1, d), lambda i, *_: (i, 0, 0)),
                  pl.BlockSpec((1, d), const),
                  pl.BlockSpec(ws1.shape, const), pl.BlockSpec(ws3.shape, const),
                  pl.BlockSpec(ws2.shape, const)],
        out_specs=[pl.BlockSpec((tm, d), first), pl.BlockSpec((tm, d), second)],
        scratch_shapes=[pltpu.VMEM((cap * nbk, LANES), F32), pltpu.VMEM((cap * nbk, LANES), F32),
                        pltpu.VMEM((cap, d), BF16), pltpu.SMEM((2,), I32), pltpu.SemaphoreType.DMA((2,))],
    )
    return pl.pallas_call(
        functools.partial(_final_kernel, npt=tp // tm, ne=ne),
        grid_spec=grid_spec,
        out_shape=[jax.ShapeDtypeStruct((tp, d), F32), jax.ShapeDtypeStruct((ts, d), F32)],
        compiler_params=_cparams(("arbitrary",)),
        name="final",
    )(*strips, osort, topi, topw, hb, x1, gate_f, npost, ws1, ws3, ws2)


def kernel(x_prompt, x_sample, cache_conv, state_ssm, cache_k, cache_v, c_prompt, c_sample,
           w_ada, b_ada, norm_pre_mix, norm_post_mix, norm_pre_ffn, norm_post_ffn,
           w_in, conv_w, conv_b, dt_bias, a_log, d_skip, gn_w, rel_bias, w_out,
           w_router, router_bias, w1, w3, w2, ws1, ws3, ws2):
    assert w_ada.shape[0] == 1, "single layer"
    bp, lp, d = x_prompt.shape
    bs, ls, _ = x_sample.shape
    assert bp == 1 and ls == CHUNK and lp % ATT_PAST == 0
    assert cache_k.shape[2] == ATT_PAST
    heads = a_log.shape[1]
    inner = heads * SSM_HEAD_DIM
    att_w = rel_bias.shape[1] * ATT_HEAD_DIM
    assert att_w == inner
    bcw = 2 * SSM_GROUPS * SSM_STATE
    ne = w_router.shape[2]
    tp, ts = bp * lp, bs * ls
    nseq = bp + bs

    xp, xs = x_prompt.reshape(tp, d), x_sample.reshape(ts, d)
    seq_np = np.concatenate([np.repeat(np.arange(bp), lp // CHUNK), bp + np.arange(bs)]).astype(np.int32)
    first_np = np.concatenate([[1], (seq_np[1:] != seq_np[:-1])]).astype(np.int32)
    seq_of_chunk, first_of_chunk = jnp.asarray(seq_np), jnp.asarray(first_np)

    c_all = jnp.concatenate([c_prompt, c_sample], axis=0)
    c_pad = jnp.pad(c_all, ((0, -nseq % 8), (0, 0)))
    mod = _ada(c_pad, w_ada[0], b_ada[0])[:nseq].reshape(nseq, 6, d)
    mod_c = mod[seq_of_chunk]
    shift_m, scale_m, gate_m, shift_f, scale_f, gate_f = [mod_c[:, i:i + 1, :] for i in range(6)]

    wi = w_in[0]
    o_z, o_x, o_bc = 0, inner, 2 * inner
    o_dt = inner + inner + bcw
    o_q = o_dt + heads
    o_k, o_v = o_q + att_w, o_q + 2 * att_w
    cols = lambda o, n: wi[:, o:o + n]
    w_main = jnp.concatenate([cols(o_z, inner), cols(o_x, inner), cols(o_q, att_w), cols(o_k, att_w),
                              cols(o_v, att_w), cols(o_bc, bcw)], axis=1).astype(BF16)
    w_dt = jnp.pad(cols(o_dt, heads), ((0, 0), (0, LANES - heads))).astype(BF16)
    proj, dt_raw = _inproj(xp, xs, scale_m, shift_m, norm_pre_mix, w_main, w_dt)
    c_x, c_k, c_v, c_bc = inner, 3 * inner, 4 * inner, 5 * inner

    pad_rows = lambda a: jnp.pad(a, ((0, 0), (8 - (CONV_W - 1), 0), (0, 0)))
    pre = jnp.concatenate([jnp.zeros((bp, CONV_W - 1, inner + bcw), F32), cache_conv[0]], axis=0)
    pre_x, pre_bc = pad_rows(pre[:, :, :inner]), pad_rows(pre[:, :, inner:])
    h0 = jnp.concatenate([jnp.zeros((bp,) + state_ssm.shape[2:], F32), state_ssm[0]], axis=0)
    h0t = h0.transpose(0, 3, 1, 2).reshape(nseq, SSM_STATE, inner)
    lane_pad = lambda v: jnp.pad(v, (0, LANES - heads)).reshape(1, LANES)
    expand = (np.arange(LANES)[:, None] == (np.arange(inner)[None, :] // SSM_HEAD_DIM)).astype(np.float32)
    tri = np.tril(np.ones((CHUNK, CHUNK), np.float32))
    consts = (conv_w[0][:, :inner], conv_w[0][:, inner:],
              conv_b[0][:inner].reshape(1, inner), conv_b[0][inner:].reshape(1, bcw),
              lane_pad(dt_bias[0]), lane_pad(-jnp.exp(a_log[0])),
              jnp.repeat(d_skip[0], SSM_HEAD_DIM).reshape(1, inner), gn_w[0].reshape(1, inner),
              jnp.asarray(expand, BF16), jnp.asarray(tri, BF16))
    y_ssd, st_out = _ssd(proj, dt_raw, seq_of_chunk, first_of_chunk, pre_x, pre_bc, h0t, consts, inner)

    bias2 = _attn_bias(rel_bias[0])
    att_p = _attn_prompt(proj, bias2, tp, att_w)
    att_s = _attn_sample(proj, cache_k[0].reshape(bs * ATT_PAST, att_w), cache_v[0].reshape(bs * ATT_PAST, att_w),
                         bias2, tp, bs, att_w)

    wo = w_out[0].astype(BF16)
    x1, ht, hb, logits = _outproj(y_ssd, att_p, att_s, xp, xs, gate_m, scale_f, shift_f,
                                  norm_post_mix, norm_pre_ffn, wo[:inner], wo[inner:], w_router[0])

    topi_t, topw_t = _route(logits.T, router_bias[0])
    a_sorted, blocks, strips = _dispatch_plan(topi_t, MOE_ROWS, ne, FINAL_ROWS)
    osort = _moe(ht, a_sorted, *blocks, w1[0], w3[0], w2[0])
    y_p, y_s = _final(osort, strips, topi_t.T, topw_t.T, hb, x1, gate_f, norm_post_ffn,
                      ws1[0].astype(BF16), ws3[0].astype(BF16), ws2[0].astype(BF16), tp, FINAL_ROWS, ne)

    tail = lambda rows: jnp.concatenate([rows[..., c_x:c_x + inner], rows[..., c_bc:c_bc + bcw]], axis=-1)
    conv_prompt = tail(proj[tp - (CONV_W - 1):tp])[None, None]
    srows = lambda c0, n: proj[tp:, c0:c0 + n].reshape(bs, ls, n)[:, ls - (CONV_W - 1):, :]
    conv_sample = jnp.concatenate([srows(c_x, inner), srows(c_bc, bcw)], axis=-1)[None]
    st = st_out.reshape(nseq, SSM_STATE, heads, SSM_HEAD_DIM).transpose(0, 2, 3, 1)
    keep = min(ATT_PAST, lp)
    hd = (rel_bias.shape[1], ATT_HEAD_DIM)
    kv = lambda c0, r0, r1, b, l: proj[r0:r1, c0:c0 + att_w].reshape(b, l, *hd)[None]
    return (y_p.reshape(bp, lp, d), y_s.reshape(bs, ls, d),
            conv_prompt, st[:bp][None], kv(c_k, tp - keep, tp, bp, keep), kv(c_v, tp - keep, tp, bp, keep),
            conv_sample, st[bp:][None], kv(c_k, tp, tp + ts, bs, ls), kv(c_v, tp, tp + ts, bs, ls))
```

```python
import functools

import numpy as np
import jax
import jax.numpy as jnp
from jax import lax
from jax.experimental import pallas as pl
from jax.experimental.pallas import tpu as pltpu

F32 = jnp.float32
BF16 = jnp.bfloat16
I32 = jnp.int32
HIGHEST = lax.Precision.HIGHEST

CHUNK = 64
SSM_HEAD_DIM = 64
SSM_GROUPS = 2
SSM_STATE = 128
CONV_W = 4
ATT_HEAD_DIM = 64
LEFT_CHUNKS = 8
ATT_PAST = LEFT_CHUNKS * CHUNK
BAND = ATT_PAST + CHUNK
REL_CLIP = 128
TOP_K = 8
N_EXPERT_GROUPS = 8
TOPK_GROUPS = 4
ROUTED_SCALE = 2.5
EPS = 1e-6
NEG_BIG = -1e30

LANES = 128
SUBLANES = 8
PAIR = 2 * ATT_HEAD_DIM
QPAIR = 2 * CHUNK
KWIN = ATT_PAST + QPAIR
VMEM_LIMIT = 56 * 1024 * 1024
MOE_ROWS = 512
ATT_UNROLL = 4


def _cparams(sem):
    return pltpu.CompilerParams(dimension_semantics=sem, vmem_limit_bytes=VMEM_LIMIT)


def _pick(ns, cands):
    for c in cands:
        if all(n % c == 0 for n in ns):
            return c
    raise ValueError(f"no tile for {ns} in {cands}")


def _silu(x):
    return x * jax.nn.sigmoid(x)


def _rms(x, g):
    ms = jnp.mean(x * x, axis=-1, keepdims=True)
    return x * lax.rsqrt(ms + EPS) * g


def _bdot(a, b):
    return jnp.dot(a, b, preferred_element_type=F32)


def _aligned(x, m):
    return x if isinstance(x, int) else pl.multiple_of(x, m)


def _split3(x):
    p0 = x.astype(BF16)
    r0 = x - p0.astype(F32)
    p1 = r0.astype(BF16)
    p2 = (r0 - p1.astype(F32)).astype(BF16)
    return p0, p1, p2


def _store_token_tiles(ref, base, x):
    rows, d = x.shape
    nb = d // LANES
    for c in range(nb):
        ref[pl.ds(base * nb + c, rows, stride=nb), :] = x[:, c * LANES:(c + 1) * LANES]


def _load_token_tiles(ref, base, rows, nb):
    return [ref[pl.ds(base * nb + c, rows, stride=nb), :] for c in range(nb)]


def _split_rows(npt):
    first = lambda i, *_: (jnp.minimum(i, npt - 1), 0)
    second = lambda i, *_: (jnp.maximum(i - npt, 0), 0)
    return first, second


def _ada_kernel(c_ref, w_ref, b_ref, o_ref):
    a = _silu(c_ref[...])
    o_ref[...] = jnp.dot(a, w_ref[...], precision=HIGHEST, preferred_element_type=F32) + b_ref[...]


def _ada(c_pad, w_ada, b_ada):
    m, d = c_pad.shape
    n = w_ada.shape[1]
    tn = _pick((n,), (1024, 512, 256, 128))
    return pl.pallas_call(
        _ada_kernel,
        grid=(n // tn,),
        in_specs=[pl.BlockSpec((m, d), lambda j: (0, 0)),
                  pl.BlockSpec((d, tn), lambda j: (0, j)),
                  pl.BlockSpec((1, tn), lambda j: (0, j))],
        out_specs=pl.BlockSpec((m, tn), lambda j: (0, j)),
        out_shape=jax.ShapeDtypeStruct((m, n), F32),
        compiler_params=_cparams(("arbitrary",)),
        name="ada",
    )(c_pad, w_ada, b_ada.reshape(1, n))


def _inproj_kernel(xp_ref, xs_ref, sc_ref, sh_ref, g_ref, w_ref, wdt_ref, o_ref, dt_ref, hm_ref, *, npt):
    i = pl.program_id(0)

    def prep(x_ref):
        x = x_ref[...]
        tm, d = x.shape
        y = _rms(x, g_ref[...]).reshape(tm // CHUNK, CHUNK, d)
        h = (y * (1.0 + sc_ref[...]) + sh_ref[...]).reshape(tm, d).astype(BF16)
        hm_ref[...] = h
        dt_ref[...] = _bdot(h, wdt_ref[...])

    @pl.when(pl.program_id(1) == 0)
    def _():
        @pl.when(i < npt)
        def _():
            prep(xp_ref)

        @pl.when(i >= npt)
        def _():
            prep(xs_ref)

    o_ref[...] = _bdot(hm_ref[...], w_ref[...])


def _inproj(xp, xs, scale, shift, g, w_main, w_dt):
    tp, d = xp.shape
    ts = xs.shape[0]
    t = tp + ts
    n = w_main.shape[1]
    tm = _pick((tp, ts), (1024, 512, 256, 128, 64))
    tn = _pick((n,), (512, 256, 128))
    nc = tm // CHUNK
    first, second = _split_rows(tp // tm)
    return pl.pallas_call(
        functools.partial(_inproj_kernel, npt=tp // tm),
        grid=(t // tm, n // tn),
        in_specs=[pl.BlockSpec((tm, d), first),
                  pl.BlockSpec((tm, d), second),
                  pl.BlockSpec((nc, 1, d), lambda i, j: (i, 0, 0)),
                  pl.BlockSpec((nc, 1, d), lambda i, j: (i, 0, 0)),
                  pl.BlockSpec((1, d), lambda i, j: (0, 0)),
                  pl.BlockSpec((d, tn), lambda i, j: (0, j)),
                  pl.BlockSpec((d, LANES), lambda i, j: (0, 0))],
        out_specs=[pl.BlockSpec((tm, tn), lambda i, j: (i, j)),
                   pl.BlockSpec((tm, LANES), lambda i, j: (i, 0))],
        out_shape=[jax.ShapeDtypeStruct((t, n), F32),
                   jax.ShapeDtypeStruct((t, LANES), F32)],
        scratch_shapes=[pltpu.VMEM((tm, d), BF16)],
        compiler_params=_cparams(("arbitrary", "arbitrary")),
        name="inproj",
    )(xp, xs, scale, shift, g, w_main, w_dt)


def _ssd_kernel(seq_ref, first_ref,
                z_ref, xs_ref, bc_ref, dt_ref, prex_ref, prebc_ref, h0_ref,
                cwx_ref, cwbc_ref, cbx_ref, cbbc_ref, dtb_ref, aneg_ref, dsk_ref, gnw_ref,
                e_ref, tri_ref,
                y_ref, st_out_ref,
                xpx_scr, xpbc_scr, st_scr):
    del seq_ref
    c = pl.program_id(0)
    inner = xs_ref.shape[1]
    gw = inner // SSM_GROUPS
    n = SSM_STATE
    pad = 8

    @pl.when(first_ref[c] == 1)
    def _():
        xpx_scr[0:pad, :] = prex_ref[0]
        xpbc_scr[0:pad, :] = prebc_ref[0]
        st_scr[...] = h0_ref[0]

    xpx_scr[pad:pad + CHUNK, :] = xs_ref[...]
    xpbc_scr[pad:pad + CHUNK, :] = bc_ref[...]

    def conv(xp, w_ref, b_ref):
        base = pad - (CONV_W - 1)
        acc = b_ref[...] + xp[base:base + CHUNK, :] * w_ref[0:1, :]
        for k in range(1, CONV_W):
            acc = acc + xp[base + k:base + k + CHUNK, :] * w_ref[k:k + 1, :]
        return _silu(acc)

    xs = conv(xpx_scr, cwx_ref, cbx_ref)
    bc = conv(xpbc_scr, cwbc_ref, cbbc_ref)
    xpx_scr[0:pad, :] = xpx_scr[CHUNK:CHUNK + pad, :]
    xpbc_scr[0:pad, :] = xpbc_scr[CHUNK:CHUNK + pad, :]

    dtv = dt_ref[...] + dtb_ref[...]
    dt = jnp.maximum(dtv, 0.0) + jnp.log(1.0 + jnp.exp(-jnp.abs(dtv)))
    da = dt * aneg_ref[...]
    acs = sum(_bdot(tri_ref[...], p) for p in _split3(da))
    full = sum(_bdot(p, e_ref[...]) for p in _split3(jnp.concatenate([dt, acs], axis=0)))
    dtf = full[0:CHUNK]
    af = full[CHUNK:2 * CHUNK]

    row = lax.broadcasted_iota(I32, (CHUNK, inner), 0)
    lj = lax.broadcasted_iota(I32, (CHUNK, inner), 1) & (SSM_HEAD_DIM - 1)
    aj = jnp.sum(jnp.where(row == lj, af, 0.0), axis=0, keepdims=True)
    lmat = jnp.exp(jnp.where(row >= lj, af - aj, NEG_BIG))
    alast = af[CHUNK - 1:CHUNK, :]
    xdt = xs * dtf
    xw = xdt * jnp.exp(alast - af)
    cdec = jnp.exp(alast)
    eaf = jnp.exp(af)

    lane = lax.broadcasted_iota(I32, (CHUNK, PAIR), 1)
    st = st_scr[...]
    ydiag, yoff, stn = [], [], []
    for g in range(SSM_GROUPS):
        bg = bc[:, g * n:(g + 1) * n].astype(BF16)
        cg = bc[:, (SSM_GROUPS + g) * n:(SSM_GROUPS + g + 1) * n].astype(BF16)
        bb = jnp.concatenate([bg, bg], axis=0)
        cbb = lax.dot_general(cg, bb, (((1,), (1,)), ((), ())), preferred_element_type=F32)
        stg = st[:, g * gw:(g + 1) * gw]
        yoff.append(_bdot(cg, stg.astype(BF16)))
        for p in range(gw // PAIR):
            lo = g * gw + p * PAIR
            m = (cbb * lmat[:, lo:lo + PAIR]).astype(BF16)
            xd = xdt[:, lo:lo + PAIR]
            w = jnp.concatenate([jnp.where(lane < SSM_HEAD_DIM, xd, 0.0),
                                 jnp.where(lane >= SSM_HEAD_DIM, xd, 0.0)], axis=0).astype(BF16)
            ydiag.append(_bdot(m, w))
        upd = lax.dot_general(bg, xw[:, g * gw:(g + 1) * gw].astype(BF16),
                              (((0,), (0,)), ((), ())), preferred_element_type=F32)
        stn.append(cdec[:, g * gw:(g + 1) * gw] * stg + upd)

    y = jnp.concatenate(ydiag, axis=1) + jnp.concatenate(yoff, axis=1) * eaf + dsk_ref[...] * xs
    y = y * _silu(z_ref[...])
    y_ref[...] = _rms(y, gnw_ref[...])
    st_new = jnp.concatenate(stn, axis=1)
    st_scr[...] = st_new
    st_out_ref[0] = st_new


def _ssd(proj, dt_raw, seq_of_chunk, first_of_chunk, pre_x, pre_bc, h0t, consts, inner):
    t = proj.shape[0]
    nch = t // CHUNK
    nseq = h0t.shape[0]
    bcw = pre_bc.shape[-1]
    assert (5 * inner) % bcw == 0
    cmap = lambda blk: (lambda c, s, f: (c, blk))
    smap3 = lambda c, s, f: (s[c], 0, 0)
    const2 = lambda c, s, f: (0, 0)
    grid_spec = pltpu.PrefetchScalarGridSpec(
        num_scalar_prefetch=2,
        grid=(nch,),
        in_specs=[pl.BlockSpec((CHUNK, inner), cmap(0)),
                  pl.BlockSpec((CHUNK, inner), cmap(1)),
                  pl.BlockSpec((CHUNK, bcw), cmap((5 * inner) // bcw)),
                  pl.BlockSpec((CHUNK, LANES), lambda c, s, f: (c, 0)),
                  pl.BlockSpec((1, 8, inner), smap3),
                  pl.BlockSpec((1, 8, bcw), smap3),
                  pl.BlockSpec((1, SSM_STATE, inner), smap3)]
                 + [pl.BlockSpec(a.shape, const2) for a in consts],
        out_specs=[pl.BlockSpec((CHUNK, inner), lambda c, s, f: (c, 0)),
                   pl.BlockSpec((1, SSM_STATE, inner), smap3)],
        scratch_shapes=[pltpu.VMEM((CHUNK + 8, inner), F32),
                        pltpu.VMEM((CHUNK + 8, bcw), F32),
                        pltpu.VMEM((SSM_STATE, inner), F32)],
    )
    return pl.pallas_call(
        _ssd_kernel,
        grid_spec=grid_spec,
        out_shape=[jax.ShapeDtypeStruct((t, inner), F32),
                   jax.ShapeDtypeStruct((nseq, SSM_STATE, inner), F32)],
        compiler_params=_cparams(("arbitrary",)),
        name="ssd",
    )(seq_of_chunk, first_of_chunk, proj, proj, proj, dt_raw, pre_x, pre_bc, h0t, *consts)


def _attn_pairs(q_ref, kwin, vtwin, bias_ref, o_ref, n_steps, n_masked_fn, out_rows):
    n_pairs = q_ref.shape[1] // PAIR
    rowp = lax.broadcasted_iota(I32, (PAIR, QPAIR), 0)
    krow = lax.broadcasted_iota(I32, (KWIN, 2 * QPAIR), 0)

    for jj in range(n_steps):
        n_masked = n_masked_fn(jj)

        def one_pair(hp, jj=jj, n_masked=n_masked):
            lo = pl.multiple_of(hp * PAIR, PAIR)
            q = q_ref[jj * QPAIR:(jj + 1) * QPAIR, pl.ds(lo, PAIR)] * (ATT_HEAD_DIM ** -0.5)
            qt = q.T
            w = jnp.concatenate([jnp.where(rowp < ATT_HEAD_DIM, qt, 0.0),
                                 jnp.where(rowp >= ATT_HEAD_DIM, qt, 0.0)], axis=1).astype(BF16)
            kb = kwin[jj * QPAIR:jj * QPAIR + KWIN, pl.ds(lo, PAIR)]
            s = _bdot(kb, w) + bias_ref[hp]
            if n_masked is not None:
                s = jnp.where(krow < n_masked, NEG_BIG, s)
            mx = jnp.max(s, axis=0, keepdims=True)
            p = jnp.exp(s - mx)
            den = jnp.sum(p, axis=0, keepdims=True)
            vb = vtwin[pl.ds(lo, PAIR), jj * QPAIR:jj * QPAIR + KWIN]
            o2 = _bdot(vb, p.astype(BF16)) / den
            ot = jnp.where(rowp < ATT_HEAD_DIM, o2[:, 0:QPAIR], o2[:, QPAIR:2 * QPAIR])
            o_ref[jj * out_rows:(jj + 1) * out_rows, pl.ds(lo, PAIR)] = ot.T[0:out_rows]

        def body(i, carry, one_pair=one_pair):
            for u in range(ATT_UNROLL):
                one_pair(i * ATT_UNROLL + u)
            return carry

        lax.fori_loop(0, n_pairs // ATT_UNROLL, body, 0)


def _attn_prompt_kernel(q_ref, kp_ref, kc_ref, vp_ref, vc_ref, bias_ref, o_ref, kwin, vtwin):
    i = pl.program_id(0)
    tq = q_ref.shape[0]
    kwin[0:ATT_PAST, :] = kp_ref[...].astype(BF16)
    kwin[ATT_PAST:ATT_PAST + tq, :] = kc_ref[...].astype(BF16)
    vtwin[:, 0:ATT_PAST] = vp_ref[...].T.astype(BF16)
    vtwin[:, ATT_PAST:ATT_PAST + tq] = vc_ref[...].T.astype(BF16)
    _attn_pairs(q_ref, kwin, vtwin, bias_ref, o_ref, tq // QPAIR,
                lambda jj: ATT_PAST - jj * QPAIR - i * tq, QPAIR)


def _attn_prompt(proj, bias2, t_prompt, width):
    tq = ATT_PAST
    assert t_prompt % tq == 0
    qb, kb, vb = 2, 3, 4
    prev = lambda i: jnp.maximum(i - 1, 0)
    return pl.pallas_call(
        _attn_prompt_kernel,
        grid=(t_prompt // tq,),
        in_specs=[pl.BlockSpec((tq, width), lambda i: (i, qb)),
                  pl.BlockSpec((tq, width), lambda i: (prev(i), kb)),
                  pl.BlockSpec((tq, width), lambda i: (i, kb)),
                  pl.BlockSpec((tq, width), lambda i: (prev(i), vb)),
                  pl.BlockSpec((tq, width), lambda i: (i, vb)),
                  pl.BlockSpec(bias2.shape, lambda i: (0, 0, 0))],
        out_specs=pl.BlockSpec((tq, width), lambda i: (i, 0)),
        out_shape=jax.ShapeDtypeStruct((t_prompt, width), F32),
        scratch_shapes=[pltpu.VMEM((ATT_PAST + tq, width), BF16),
                        pltpu.VMEM((width, ATT_PAST + tq), BF16)],
        compiler_params=_cparams(("arbitrary",)),
        name="attn_prompt",
    )(proj, proj, proj, proj, proj, bias2)


def _attn_sample_kernel(q_ref, kc_ref, ks_ref, vc_ref, vs_ref, bias_ref, o_ref, qpad, kwin, vtwin):
    width = q_ref.shape[1]
    qpad[0:CHUNK, :] = q_ref[...]
    qpad[CHUNK:QPAIR, :] = jnp.zeros((CHUNK, width), F32)
    kwin[0:ATT_PAST, :] = kc_ref[...].astype(BF16)
    kwin[ATT_PAST:BAND, :] = ks_ref[...].astype(BF16)
    kwin[BAND:KWIN, :] = jnp.zeros((KWIN - BAND, width), BF16)
    vtwin[:, 0:ATT_PAST] = vc_ref[...].T.astype(BF16)
    vtwin[:, ATT_PAST:KWIN] = jnp.concatenate(
        [vs_ref[...], jnp.zeros((KWIN - BAND, width), F32)], axis=0).T.astype(BF16)
    _attn_pairs(qpad, kwin, vtwin, bias_ref, o_ref, 1, lambda jj: None, CHUNK)


def _attn_sample(proj, cache_k, cache_v, bias2, t_prompt, n_seq, width):
    qb, kb, vb = 2, 3, 4
    c0 = t_prompt // CHUNK
    return pl.pallas_call(
        _attn_sample_kernel,
        grid=(n_seq,),
        in_specs=[pl.BlockSpec((CHUNK, width), lambda b: (c0 + b, qb)),
                  pl.BlockSpec((ATT_PAST, width), lambda b: (b, 0)),
                  pl.BlockSpec((CHUNK, width), lambda b: (c0 + b, kb)),
                  pl.BlockSpec((ATT_PAST, width), lambda b: (b, 0)),
                  pl.BlockSpec((CHUNK, width), lambda b: (c0 + b, vb)),
                  pl.BlockSpec(bias2.shape, lambda b: (0, 0, 0))],
        out_specs=pl.BlockSpec((CHUNK, width), lambda b: (b, 0)),
        out_shape=jax.ShapeDtypeStruct((n_seq * CHUNK, width), F32),
        scratch_shapes=[pltpu.VMEM((QPAIR, width), F32),
                        pltpu.VMEM((KWIN, width), BF16),
                        pltpu.VMEM((width, KWIN), BF16)],
        compiler_params=_cparams(("arbitrary",)),
        name="attn_sample",
    )(proj, cache_k, proj, cache_v, proj, bias2)


def _attn_bias(table):
    h = table.shape[0]
    x = np.arange(BAND + CHUNK - 1)
    rel = np.clip(BAND - 1 - x, -REL_CLIP, REL_CLIP) + REL_CLIP
    u = table[:, rel]
    std = jnp.stack([u[:, CHUNK - 1 - i:CHUNK - 1 - i + BAND] for i in range(CHUNK)], axis=1)
    neg = jnp.full((h, CHUNK, KWIN - BAND), NEG_BIG, F32)
    b = jnp.stack([jnp.concatenate([std, neg], axis=2),
                   jnp.concatenate([neg, std], axis=2)],
                  axis=1)
    b = b.reshape(h // 2, 2, 2, CHUNK, KWIN).transpose(0, 4, 1, 2, 3)
    return b.reshape(h // 2, KWIN, 2 * QPAIR)


def _outproj_kernel(y_ref, ap_ref, as_ref, xp_ref, xs_ref, gm_ref, scf_ref, shf_ref, npost_ref, npre_ref,
                    wo1_ref, wo2_ref, wrh_ref, wrl_ref, x1_ref, ht_ref, hb_ref, lg_ref, *, npt):
    i = pl.program_id(0)

    def body(a_ref, x_ref):
        tm, d = x_ref.shape
        mix = _bdot(y_ref[...].astype(BF16), wo1_ref[...]) + _bdot(a_ref[...].astype(BF16), wo2_ref[...])
        nm = _rms(mix, npost_ref[...]).reshape(tm // CHUNK, CHUNK, d)
        x1 = x_ref[...].reshape(tm // CHUNK, CHUNK, d) + gm_ref[...] * nm
        x1_ref[...] = x1.reshape(tm, d)
        hn = _rms(x1, npre_ref[...])
        hf = (hn * (1.0 + scf_ref[...]) + shf_ref[...]).reshape(tm, d)
        _store_token_tiles(ht_ref, 0, hf)
        h_hi = hf.astype(BF16)
        hb_ref[...] = h_hi
        h_lo = (hf - h_hi.astype(F32)).astype(BF16)
        lg_ref[...] = _bdot(h_hi, wrh_ref[...]) + _bdot(h_lo, wrh_ref[...]) + _bdot(h_hi, wrl_ref[...])

    @pl.when(i < npt)
    def _():
        body(ap_ref, xp_ref)

    @pl.when(i >= npt)
    def _():
        body(as_ref, xs_ref)


def _outproj(y_ssd, att_p, att_s, xp, xs, gate_m, scale_f, shift_f, npost, npre, wo1, wo2, wr):
    tp, d = xp.shape
    ts = xs.shape[0]
    t = tp + ts
    inner = y_ssd.shape[1]
    ne = wr.shape[1]
    wr_hi = wr.astype(BF16)
    wr_lo = (wr - wr_hi.astype(F32)).astype(BF16)
    tm = _pick((tp, ts), (256, 128, 64))
    nc = tm // CHUNK
    first, second = _split_rows(tp // tm)
    row = lambda i: (i, 0)
    tab = lambda i: (i, 0, 0)
    const = lambda i: (0, 0)
    return pl.pallas_call(
        functools.partial(_outproj_kernel, npt=tp // tm),
        grid=(t // tm,),
        in_specs=[pl.BlockSpec((tm, inner), row),
                  pl.BlockSpec((tm, att_p.shape[1]), first), pl.BlockSpec((tm, att_s.shape[1]), second),
                  pl.BlockSpec((tm, d), first), pl.BlockSpec((tm, d), second),
                  pl.BlockSpec((nc, 1, d), tab), pl.BlockSpec((nc, 1, d), tab), pl.BlockSpec((nc, 1, d), tab),
                  pl.BlockSpec((1, d), const), pl.BlockSpec((1, d), const),
                  pl.BlockSpec(wo1.shape, const), pl.BlockSpec(wo2.shape, const),
                  pl.BlockSpec(wr.shape, const), pl.BlockSpec(wr.shape, const)],
        out_specs=[pl.BlockSpec((tm, d), row), pl.BlockSpec((tm * (d // LANES), LANES), row),
                   pl.BlockSpec((tm, d), row), pl.BlockSpec((tm, ne), row)],
        out_shape=[jax.ShapeDtypeStruct((t, d), F32), jax.ShapeDtypeStruct((t * (d // LANES), LANES), F32),
                   jax.ShapeDtypeStruct((t, d), BF16), jax.ShapeDtypeStruct((t, ne), F32)],
        compiler_params=_cparams(("arbitrary",)),
        name="outproj",
    )(y_ssd, att_p, att_s, xp, xs, gate_m, scale_f, shift_f, npost, npre, wo1, wo2, wr_hi, wr_lo)


def _route_kernel(lg_ref, rb_ref, ti_ref, tw_ref):
    ne, tt = lg_ref.shape
    gs = ne // N_EXPERT_GROUPS
    scores = jax.nn.sigmoid(lg_ref[...])
    sel = scores + rb_ref[...]
    g3 = sel.reshape(N_EXPERT_GROUPS, gs, tt)
    i3 = lax.broadcasted_iota(I32, g3.shape, 1)
    m1 = jnp.max(g3, axis=1, keepdims=True)
    first = jnp.min(jnp.where(g3 == m1, i3, gs), axis=1, keepdims=True)
    m2 = jnp.max(jnp.where(i3 == first, -jnp.inf, g3), axis=1, keepdims=True)
    gscore = (m1 + m2).reshape(N_EXPERT_GROUPS, tt)
    gi = lax.broadcasted_iota(I32, gscore.shape, 0)
    gmask = jnp.zeros(gscore.shape, jnp.bool_)
    rem = gscore
    for _ in range(TOPK_GROUPS):
        mg = jnp.max(rem, axis=0, keepdims=True)
        pick = jnp.min(jnp.where(rem == mg, gi, N_EXPERT_GROUPS), axis=0, keepdims=True)
        hit = gi == pick
        gmask = gmask | hit
        rem = jnp.where(hit, -jnp.inf, rem)
    emask = jnp.broadcast_to(gmask.reshape(N_EXPERT_GROUPS, 1, tt), g3.shape).reshape(ne, tt)
    rem = jnp.where(emask, sel, -jnp.inf)
    ei = lax.broadcasted_iota(I32, (ne, tt), 0)
    idx, wts = [], []
    for _ in range(TOP_K):
        me = jnp.max(rem, axis=0, keepdims=True)
        pick = jnp.min(jnp.where(rem == me, ei, ne), axis=0, keepdims=True)
        hit = ei == pick
        idx.append(pick)
        wts.append(jnp.sum(jnp.where(hit, scores, 0.0), axis=0, keepdims=True))
        rem = jnp.where(hit, -jnp.inf, rem)
    w = jnp.concatenate(wts, axis=0)
    ti_ref[...] = jnp.concatenate(idx, axis=0)
    tw_ref[...] = w / jnp.sum(w, axis=0, keepdims=True) * ROUTED_SCALE


def _route(logits_t, router_bias):
    ne, t = logits_t.shape
    tt = _pick((t,), (2176, 2048, 1024, 512, 256, 128))
    return pl.pallas_call(
        _route_kernel,
        grid=(t // tt,),
        in_specs=[pl.BlockSpec((ne, tt), lambda i: (0, i)),
                  pl.BlockSpec((ne, 1), lambda i: (0, 0))],
        out_specs=[pl.BlockSpec((TOP_K, tt), lambda i: (0, i)),
                   pl.BlockSpec((TOP_K, tt), lambda i: (0, i))],
        out_shape=[jax.ShapeDtypeStruct((TOP_K, t), I32),
                   jax.ShapeDtypeStruct((TOP_K, t), F32)],
        compiler_params=_cparams(("arbitrary",)),
        name="route",
    )(logits_t, router_bias.reshape(ne, 1))


def _moe_kernel(be_ref, r0_ref, n_ref, first_ref, a_ref,
                ht_hbm, w1_ref, w3_ref, w2_ref, y8_hbm,
                xbuf_a, xbuf_b, obuf_a, obuf_b, xs_scr, w1b, w3b, w2b, gsem, ssem, *, n_tok, tme):
    del be_ref
    b = pl.program_id(0)
    nb = pl.num_programs(0)
    nbk = xs_scr.shape[1] // LANES
    spare0 = TOP_K * n_tok
    whole = tme * nbk

    def tile(ref, row):
        return ref.at[pl.ds(_aligned(row * nbk, nbk), nbk)]

    def gather_all(xb, s):
        return pltpu.make_async_copy(ht_hbm.at[pl.ds(0, whole)], xb, gsem.at[s])

    def scatter_all(ob, s):
        return pltpu.make_async_copy(ob, y8_hbm.at[pl.ds(0, whole)], ssem.at[s])

    def start_gather(blk, xb, s):
        r0 = r0_ref[blk]
        for r in range(tme):
            pltpu.make_async_copy(tile(ht_hbm, a_ref[r0 + r] >> 3), tile(xb, r), gsem.at[s]).start()

    def start_scatter(blk, n_valid, ob, s):
        r0 = r0_ref[blk]
        for r in range(tme):
            a = a_ref[r0 + r]
            dst = jnp.where(r < n_valid, (a & (TOP_K - 1)) * n_tok + (a >> 3), spare0 + s * tme + r)
            pltpu.make_async_copy(tile(ob, r), tile(y8_hbm, dst), ssem.at[s]).start()

    @pl.when(b == 0)
    def _():
        start_gather(0, xbuf_a, 0)
        for s, ob in enumerate((obuf_a, obuf_b)):
            ob[...] = jnp.zeros_like(ob)
            spare = pltpu.make_async_copy(ob, y8_hbm.at[pl.ds((spare0 + s * tme) * nbk, whole)], ssem.at[s])
            spare.start()
            spare.wait()

    @pl.when(first_ref[b] == 1)
    def _():
        w1b[...] = w1_ref[0].astype(BF16)
        w3b[...] = w3_ref[0].astype(BF16)
        w2b[...] = w2_ref[0].astype(BF16)

    nxt = jnp.minimum(b + 1, nb - 1)
    prv = jnp.maximum(b - 1, 0)
    n_prv = jnp.where(b >= 1, n_ref[prv], 0)

    def step(s, xb_cur, xb_nxt, ob_cur, ob_prv):
        gather_all(xb_cur, s).wait()

        @pl.when(b >= 1)
        def _():
            scatter_all(ob_cur, s).wait()

        start_gather(nxt, xb_nxt, 1 - s)
        for c, blk in enumerate(_load_token_tiles(xb_cur, 0, tme, nbk)):
            xs_scr[:, c * LANES:(c + 1) * LANES] = blk.astype(BF16)
        x = xs_scr[...]
        h = (_silu(_bdot(x, w1b[...])) * _bdot(x, w3b[...])).astype(BF16)
        _store_token_tiles(ob_cur, 0, _bdot(h, w2b[...]))
        start_scatter(prv, n_prv, ob_prv, 1 - s)

        @pl.when(b == nb - 1)
        def _():
            gather_all(xb_nxt, 1 - s).wait()
            scatter_all(ob_prv, 1 - s).wait()
            start_scatter(b, n_ref[b], ob_cur, s)
            scatter_all(ob_cur, s).wait()

    @pl.when(b % 2 == 0)
    def _():
        step(0, xbuf_a, xbuf_b, obuf_a, obuf_b)

    @pl.when(b % 2 == 1)
    def _():
        step(1, xbuf_b, xbuf_a, obuf_b, obuf_a)


def _moe(ht, a_sorted, blk_e, blk_r0, blk_n, blk_first, w1, w3, w2):
    ne, d, de = w1.shape
    nbk = d // LANES
    t = ht.shape[0] // nbk
    nb = blk_e.shape[0]
    tme = MOE_ROWS
    assert (TOP_K * t) % tme == 0 and nbk % SUBLANES == 0
    wmap = lambda b, be, r0, n, f, a: (be[b], 0, 0)
    grid_spec = pltpu.PrefetchScalarGridSpec(
        num_scalar_prefetch=5,
        grid=(nb,),
        in_specs=[pl.BlockSpec(memory_space=pl.ANY),
                  pl.BlockSpec((1, d, de), wmap),
                  pl.BlockSpec((1, d, de), wmap),
                  pl.BlockSpec((1, de, d), wmap)],
        out_specs=pl.BlockSpec(memory_space=pl.ANY),
        scratch_shapes=[pltpu.VMEM((tme * nbk, LANES), F32), pltpu.VMEM((tme * nbk, LANES), F32),
                        pltpu.VMEM((tme * nbk, LANES), F32), pltpu.VMEM((tme * nbk, LANES), F32),
                        pltpu.VMEM((tme, d), BF16),
                        pltpu.VMEM((d, de), BF16), pltpu.VMEM((d, de), BF16), pltpu.VMEM((de, d), BF16),
                        pltpu.SemaphoreType.DMA((2,)), pltpu.SemaphoreType.DMA((2,))],
    )
    return pl.pallas_call(
        functools.partial(_moe_kernel, n_tok=t, tme=tme),
        grid_spec=grid_spec,
        out_shape=jax.ShapeDtypeStruct(((TOP_K * t + 2 * tme) * nbk, LANES), F32),
        compiler_params=_cparams(("arbitrary",)),
        name="moe",
    )(blk_e, blk_r0, blk_n, blk_first, a_sorted, ht, w1, w3, w2)


def _dispatch_plan(topi_t, tme, ne):
    k, t = topi_t.shape
    assert k == TOP_K
    a_cnt = k * t
    shift = int(np.ceil(np.log2(a_cnt)))
    assert ne << shift < 2 ** 31
    a_id = jnp.arange(t, dtype=I32)[None, :] * k + jnp.arange(k, dtype=I32)[:, None]
    keys = (topi_t << shift) + a_id
    a_sorted = jnp.sort(keys.reshape(-1)) & ((1 << shift) - 1)
    a_sorted = jnp.concatenate([a_sorted, jnp.zeros((tme,), I32)])
    counts = jnp.sum(topi_t.reshape(-1, 1) == jnp.arange(ne, dtype=I32)[None, :], axis=0, dtype=I32)
    starts = jnp.cumsum(counts) - counts
    nblk = (counts + tme - 1) // tme
    blk_end = jnp.cumsum(nblk)
    nb = a_cnt // tme + ne
    b = jnp.arange(nb, dtype=I32)
    valid = b < blk_end[-1]
    e_raw = jnp.minimum(jnp.sum(blk_end[None, :] <= b[:, None], axis=1, dtype=I32), ne - 1)
    onehot = (e_raw[:, None] == jnp.arange(ne, dtype=I32)[None, :]).astype(I32)
    pick = lambda v: jnp.sum(onehot * v[None, :], axis=1)
    j = b - pick(blk_end - nblk)
    blk_r0 = jnp.where(valid, pick(starts) + j * tme, 0)
    blk_n = jnp.where(valid, jnp.minimum(tme, pick(counts) - j * tme), 0)
    blk_first = (valid & (j == 0)).astype(I32)
    e_last = jnp.max(jnp.where(valid, e_raw, 0))
    blk_e = jnp.where(valid, e_raw, e_last)
    return a_sorted, blk_e, blk_r0, blk_n, blk_first


def _final_kernel(*refs, npt):
    y8_refs = refs[:TOP_K]
    tw_ref, hb_ref, x1_ref, gf_ref, npost_ref, ws1_ref, ws3_ref, ws2_ref, op_ref, os_ref = refs[TOP_K:]
    i = pl.program_id(0)
    tm, d = x1_ref.shape
    nbk = d // LANES
    tw = tw_ref[...]
    routed = None
    for k in range(TOP_K):
        wk = jnp.broadcast_to(tw[:, k:k + 1], (tm, LANES))
        part = [blk * wk for blk in _load_token_tiles(y8_refs[k], 0, tm, nbk)]
        routed = part if routed is None else [a + p for a, p in zip(routed, part)]
    hb = hb_ref[...]
    shared = _bdot((_silu(_bdot(hb, ws1_ref[...])) * _bdot(hb, ws3_ref[...])).astype(BF16), ws2_ref[...])
    nm = _rms(jnp.concatenate(routed, axis=1) + shared, npost_ref[...]).reshape(tm // CHUNK, CHUNK, d)
    y = (x1_ref[...].reshape(tm // CHUNK, CHUNK, d) + gf_ref[...] * nm).reshape(tm, d)

    @pl.when(i < npt)
    def _():
        op_ref[...] = y

    @pl.when(i >= npt)
    def _():
        os_ref[...] = y


def _final(y8, topw, hb, x1, gate_f, npost, ws1, ws3, ws2, tp):
    t, d = x1.shape
    ts = t - tp
    nbk = d // LANES
    tm = _pick((tp, ts), (128, 64))
    nc = tm // CHUNK
    nt = t // tm
    first, second = _split_rows(tp // tm)
    row = lambda i: (i, 0)
    const = lambda i: (0, 0)
    planes = [pl.BlockSpec((tm * nbk, LANES), functools.partial(lambda i, k: (k * nt + i, 0), k=k))
              for k in range(TOP_K)]
    return pl.pallas_call(
        functools.partial(_final_kernel, npt=tp // tm),
        grid=(nt,),
        in_specs=planes + [pl.BlockSpec((tm, TOP_K), row),
                           pl.BlockSpec((tm, d), row), pl.BlockSpec((tm, d), row),
                           pl.BlockSpec((nc, 1, d), lambda i: (i, 0, 0)),
                           pl.BlockSpec((1, d), const),
                           pl.BlockSpec(ws1.shape, const), pl.BlockSpec(ws3.shape, const),
                           pl.BlockSpec(ws2.shape, const)],
        out_specs=[pl.BlockSpec((tm, d), first), pl.BlockSpec((tm, d), second)],
        out_shape=[jax.ShapeDtypeStruct((tp, d), F32), jax.ShapeDtypeStruct((ts, d), F32)],
        compiler_params=_cparams(("arbitrary",)),
        name="final",
    )(*([y8] * TOP_K), topw, hb, x1, gate_f, npost, ws1, ws3, ws2)


def kernel(x_prompt, x_sample, cache_conv, state_ssm, cache_k, cache_v, c_prompt, c_sample,
           w_ada, b_ada, norm_pre_mix, norm_post_mix, norm_pre_ffn, norm_post_ffn,
           w_in, conv_w, conv_b, dt_bias, a_log, d_skip, gn_w, rel_bias, w_out,
           w_router, router_bias, w1, w3, w2, ws1, ws3, ws2):
    assert w_ada.shape[0] == 1, "single layer"
    bp, lp, d = x_prompt.shape
    bs, ls, _ = x_sample.shape
    assert bp == 1 and ls == CHUNK and lp % ATT_PAST == 0
    assert cache_k.shape[2] == ATT_PAST
    heads = a_log.shape[1]
    inner = heads * SSM_HEAD_DIM
    att_w = rel_bias.shape[1] * ATT_HEAD_DIM
    assert att_w == inner
    bcw = 2 * SSM_GROUPS * SSM_STATE
    ne = w_router.shape[2]
    tp, ts = bp * lp, bs * ls
    nseq = bp + bs

    xp, xs = x_prompt.reshape(tp, d), x_sample.reshape(ts, d)
    seq_np = np.concatenate([np.repeat(np.arange(bp), lp // CHUNK), bp + np.arange(bs)]).astype(np.int32)
    first_np = np.concatenate([[1], (seq_np[1:] != seq_np[:-1])]).astype(np.int32)
    seq_of_chunk, first_of_chunk = jnp.asarray(seq_np), jnp.asarray(first_np)

    c_all = jnp.concatenate([c_prompt, c_sample], axis=0)
    c_pad = jnp.pad(c_all, ((0, -nseq % 8), (0, 0)))
    mod = _ada(c_pad, w_ada[0], b_ada[0])[:nseq].reshape(nseq, 6, d)
    mod_c = mod[seq_of_chunk]
    shift_m, scale_m, gate_m, shift_f, scale_f, gate_f = [mod_c[:, i:i + 1, :] for i in range(6)]

    wi = w_in[0]
    o_z, o_x, o_bc = 0, inner, 2 * inner
    o_dt = inner + inner + bcw
    o_q = o_dt + heads
    o_k, o_v = o_q + att_w, o_q + 2 * att_w
    cols = lambda o, n: wi[:, o:o + n]
    w_main = jnp.concatenate([cols(o_z, inner), cols(o_x, inner), cols(o_q, att_w), cols(o_k, att_w),
                              cols(o_v, att_w), cols(o_bc, bcw)], axis=1).astype(BF16)
    w_dt = jnp.pad(cols(o_dt, heads), ((0, 0), (0, LANES - heads))).astype(BF16)
    proj, dt_raw = _inproj(xp, xs, scale_m, shift_m, norm_pre_mix, w_main, w_dt)
    c_x, c_k, c_v, c_bc = inner, 3 * inner, 4 * inner, 5 * inner

    pad_rows = lambda a: jnp.pad(a, ((0, 0), (8 - (CONV_W - 1), 0), (0, 0)))
    pre = jnp.concatenate([jnp.zeros((bp, CONV_W - 1, inner + bcw), F32), cache_conv[0]], axis=0)
    pre_x, pre_bc = pad_rows(pre[:, :, :inner]), pad_rows(pre[:, :, inner:])
    h0 = jnp.concatenate([jnp.zeros((bp,) + state_ssm.shape[2:], F32), state_ssm[0]], axis=0)
    h0t = h0.transpose(0, 3, 1, 2).reshape(nseq, SSM_STATE, inner)
    lane_pad = lambda v: jnp.pad(v, (0, LANES - heads)).reshape(1, LANES)
    expand = (np.arange(LANES)[:, None] == (np.arange(inner)[None, :] // SSM_HEAD_DIM)).astype(np.float32)
    tri = np.tril(np.ones((CHUNK, CHUNK), np.float32))
    consts = (conv_w[0][:, :inner], conv_w[0][:, inner:],
              conv_b[0][:inner].reshape(1, inner), conv_b[0][inner:].reshape(1, bcw),
              lane_pad(dt_bias[0]), lane_pad(-jnp.exp(a_log[0])),
              jnp.repeat(d_skip[0], SSM_HEAD_DIM).reshape(1, inner), gn_w[0].reshape(1, inner),
              jnp.asarray(expand, BF16), jnp.asarray(tri, BF16))
    y_ssd, st_out = _ssd(proj, dt_raw, seq_of_chunk, first_of_chunk, pre_x, pre_bc, h0t, consts, inner)

    bias2 = _attn_bias(rel_bias[0])
    att_p = _attn_prompt(proj, bias2, tp, att_w)
    att_s = _attn_sample(proj, cache_k[0].reshape(bs * ATT_PAST, att_w), cache_v[0].reshape(bs * ATT_PAST, att_w),
                         bias2, tp, bs, att_w)

    wo = w_out[0].astype(BF16)
    x1, ht, hb, logits = _outproj(y_ssd, att_p, att_s, xp, xs, gate_m, scale_f, shift_f,
                                  norm_post_mix, norm_pre_ffn, wo[:inner], wo[inner:], w_router[0])

    topi_t, topw_t = _route(logits.T, router_bias[0])
    a_sorted, blk_e, blk_r0, blk_n, blk_first = _dispatch_plan(topi_t, MOE_ROWS, ne)
    y8 = _moe(ht, a_sorted, blk_e, blk_r0, blk_n, blk_first, w1[0], w3[0], w2[0])
    y_p, y_s = _final(y8, topw_t.T, hb, x1, gate_f, norm_post_ffn,
                      ws1[0].astype(BF16), ws3[0].astype(BF16), ws2[0].astype(BF16), tp)

    tail = lambda rows: jnp.concatenate([rows[..., c_x:c_x + inner], rows[..., c_bc:c_bc + bcw]], axis=-1)
    conv_prompt = tail(proj[tp - (CONV_W - 1):tp])[None, None]
    srows = lambda c0, n: proj[tp:, c0:c0 + n].reshape(bs, ls, n)[:, ls - (CONV_W - 1):, :]
    conv_sample = jnp.concatenate([srows(c_x, inner), srows(c_bc, bcw)], axis=-1)[None]
    st = st_out.reshape(nseq, SSM_STATE, heads, SSM_HEAD_DIM).transpose(0, 2, 3, 1)
    keep = min(ATT_PAST, lp)
    hd = (rel_bias.shape[1], ATT_HEAD_DIM)
    kv = lambda c0, r0, r1, b, l: proj[r0:r1, c0:c0 + att_w].reshape(b, l, *hd)[None]
    return (y_p.reshape(bp, lp, d), y_s.reshape(bs, ls, d),
            conv_prompt, st[:bp][None], kv(c_k, tp - keep, tp, bp, keep), kv(c_v, tp - keep, tp, bp, keep),
            conv_sample, st[bp:][None], kv(c_k, tp, tp + ts, bs, ls), kv(c_v, tp, tp + ts, bs, ls))
```

```python
import functools

import numpy as np
import jax
import jax.numpy as jnp
from jax import lax
from jax.experimental import pallas as pl
from jax.experimental.pallas import tpu as pltpu

F32 = jnp.float32
BF16 = jnp.bfloat16
I32 = jnp.int32
HIGHEST = lax.Precision.HIGHEST

CHUNK = 64
SSM_HEAD_DIM = 64
SSM_GROUPS = 2
SSM_STATE = 128
CONV_W = 4
ATT_HEAD_DIM = 64
LEFT_CHUNKS = 8
ATT_PAST = LEFT_CHUNKS * CHUNK
BAND = ATT_PAST + CHUNK
REL_CLIP = 128
TOP_K = 8
N_EXPERT_GROUPS = 8
TOPK_GROUPS = 4
ROUTED_SCALE = 2.5
EPS = 1e-6
NEG_BIG = -1e30

LANES = 128
PAIR = 2 * ATT_HEAD_DIM
QPAIR = 2 * CHUNK
KWIN = ATT_PAST + QPAIR
VMEM_LIMIT = 56 * 1024 * 1024
MOE_ROWS = 512
ATT_UNROLL = 4


def _cparams(sem):
    return pltpu.CompilerParams(dimension_semantics=sem, vmem_limit_bytes=VMEM_LIMIT)


def _pick(ns, cands):
    for c in cands:
        if all(n % c == 0 for n in ns):
            return c
    raise ValueError(f"no tile for {ns} in {cands}")


def _silu(x):
    return x * jax.nn.sigmoid(x)


def _rms(x, g):
    ms = jnp.mean(x * x, axis=-1, keepdims=True)
    return x * lax.rsqrt(ms + EPS) * g


def _bdot(a, b):
    return jnp.dot(a, b, preferred_element_type=F32)


def _split3(x):
    p0 = x.astype(BF16)
    r0 = x - p0.astype(F32)
    p1 = r0.astype(BF16)
    p2 = (r0 - p1.astype(F32)).astype(BF16)
    return p0, p1, p2


def _split_rows(npt):
    first = lambda i, *_: (jnp.minimum(i, npt - 1), 0)
    second = lambda i, *_: (jnp.maximum(i - npt, 0), 0)
    return first, second


def _ada_kernel(c_ref, w_ref, b_ref, o_ref):
    a = _silu(c_ref[...])
    o_ref[...] = jnp.dot(a, w_ref[...], precision=HIGHEST, preferred_element_type=F32) + b_ref[...]


def _ada(c_pad, w_ada, b_ada):
    m, d = c_pad.shape
    n = w_ada.shape[1]
    tn = _pick((n,), (1024, 512, 256, 128))
    return pl.pallas_call(
        _ada_kernel,
        grid=(n // tn,),
        in_specs=[pl.BlockSpec((m, d), lambda j: (0, 0)),
                  pl.BlockSpec((d, tn), lambda j: (0, j)),
                  pl.BlockSpec((1, tn), lambda j: (0, j))],
        out_specs=pl.BlockSpec((m, tn), lambda j: (0, j)),
        out_shape=jax.ShapeDtypeStruct((m, n), F32),
        compiler_params=_cparams(("arbitrary",)),
        name="ada",
    )(c_pad, w_ada, b_ada.reshape(1, n))


def _inproj_kernel(xp_ref, xs_ref, sc_ref, sh_ref, g_ref, w_ref, wdt_ref, o_ref, dt_ref, hm_ref, *, npt):
    i = pl.program_id(0)

    def prep(x_ref):
        x = x_ref[...]
        tm, d = x.shape
        y = _rms(x, g_ref[...]).reshape(tm // CHUNK, CHUNK, d)
        h = (y * (1.0 + sc_ref[...]) + sh_ref[...]).reshape(tm, d).astype(BF16)
        hm_ref[...] = h
        dt_ref[...] = _bdot(h, wdt_ref[...])

    @pl.when(pl.program_id(1) == 0)
    def _():
        @pl.when(i < npt)
        def _():
            prep(xp_ref)

        @pl.when(i >= npt)
        def _():
            prep(xs_ref)

    o_ref[...] = _bdot(hm_ref[...], w_ref[...])


def _inproj(xp, xs, scale, shift, g, w_main, w_dt):
    tp, d = xp.shape
    ts = xs.shape[0]
    t = tp + ts
    n = w_main.shape[1]
    tm = _pick((tp, ts), (1024, 512, 256, 128, 64))
    tn = _pick((n,), (512, 256, 128))
    nc = tm // CHUNK
    first, second = _split_rows(tp // tm)
    return pl.pallas_call(
        functools.partial(_inproj_kernel, npt=tp // tm),
        grid=(t // tm, n // tn),
        in_specs=[pl.BlockSpec((tm, d), first),
                  pl.BlockSpec((tm, d), second),
                  pl.BlockSpec((nc, 1, d), lambda i, j: (i, 0, 0)),
                  pl.BlockSpec((nc, 1, d), lambda i, j: (i, 0, 0)),
                  pl.BlockSpec((1, d), lambda i, j: (0, 0)),
                  pl.BlockSpec((d, tn), lambda i, j: (0, j)),
                  pl.BlockSpec((d, LANES), lambda i, j: (0, 0))],
        out_specs=[pl.BlockSpec((tm, tn), lambda i, j: (i, j)),
                   pl.BlockSpec((tm, LANES), lambda i, j: (i, 0))],
        out_shape=[jax.ShapeDtypeStruct((t, n), F32),
                   jax.ShapeDtypeStruct((t, LANES), F32)],
        scratch_shapes=[pltpu.VMEM((tm, d), BF16)],
        compiler_params=_cparams(("arbitrary", "arbitrary")),
        name="inproj",
    )(xp, xs, scale, shift, g, w_main, w_dt)


def _ssd_kernel(seq_ref, first_ref,
                z_ref, xs_ref, bc_ref, dt_ref, prex_ref, prebc_ref, h0_ref,
                cwx_ref, cwbc_ref, cbx_ref, cbbc_ref, dtb_ref, aneg_ref, dsk_ref, gnw_ref,
                e_ref, tri_ref,
                y_ref, st_out_ref,
                xpx_scr, xpbc_scr, st_scr):
    del seq_ref
    c = pl.program_id(0)
    inner = xs_ref.shape[1]
    gw = inner // SSM_GROUPS
    n = SSM_STATE
    pad = 8

    @pl.when(first_ref[c] == 1)
    def _():
        xpx_scr[0:pad, :] = prex_ref[0]
        xpbc_scr[0:pad, :] = prebc_ref[0]
        st_scr[...] = h0_ref[0]

    xpx_scr[pad:pad + CHUNK, :] = xs_ref[...]
    xpbc_scr[pad:pad + CHUNK, :] = bc_ref[...]

    def conv(xp, w_ref, b_ref):
        base = pad - (CONV_W - 1)
        acc = b_ref[...] + xp[base:base + CHUNK, :] * w_ref[0:1, :]
        for k in range(1, CONV_W):
            acc = acc + xp[base + k:base + k + CHUNK, :] * w_ref[k:k + 1, :]
        return _silu(acc)

    xs = conv(xpx_scr, cwx_ref, cbx_ref)
    bc = conv(xpbc_scr, cwbc_ref, cbbc_ref)
    xpx_scr[0:pad, :] = xpx_scr[CHUNK:CHUNK + pad, :]
    xpbc_scr[0:pad, :] = xpbc_scr[CHUNK:CHUNK + pad, :]

    dtv = dt_ref[...] + dtb_ref[...]
    dt = jnp.maximum(dtv, 0.0) + jnp.log(1.0 + jnp.exp(-jnp.abs(dtv)))
    da = dt * aneg_ref[...]
    acs = sum(_bdot(tri_ref[...], p) for p in _split3(da))
    full = sum(_bdot(p, e_ref[...]) for p in _split3(jnp.concatenate([dt, acs], axis=0)))
    dtf = full[0:CHUNK]
    af = full[CHUNK:2 * CHUNK]

    row = lax.broadcasted_iota(I32, (CHUNK, inner), 0)
    lj = lax.broadcasted_iota(I32, (CHUNK, inner), 1) & (SSM_HEAD_DIM - 1)
    aj = jnp.sum(jnp.where(row == lj, af, 0.0), axis=0, keepdims=True)
    lmat = jnp.exp(jnp.where(row >= lj, af - aj, NEG_BIG))
    alast = af[CHUNK - 1:CHUNK, :]
    xdt = xs * dtf
    xw = xdt * jnp.exp(alast - af)
    cdec = jnp.exp(alast)
    eaf = jnp.exp(af)

    lane = lax.broadcasted_iota(I32, (CHUNK, PAIR), 1)
    st = st_scr[...]
    ydiag, yoff, stn = [], [], []
    for g in range(SSM_GROUPS):
        bg = bc[:, g * n:(g + 1) * n].astype(BF16)
        cg = bc[:, (SSM_GROUPS + g) * n:(SSM_GROUPS + g + 1) * n].astype(BF16)
        bb = jnp.concatenate([bg, bg], axis=0)
        cbb = lax.dot_general(cg, bb, (((1,), (1,)), ((), ())), preferred_element_type=F32)
        stg = st[:, g * gw:(g + 1) * gw]
        yoff.append(_bdot(cg, stg.astype(BF16)))
        for p in range(gw // PAIR):
            lo = g * gw + p * PAIR
            m = (cbb * lmat[:, lo:lo + PAIR]).astype(BF16)
            xd = xdt[:, lo:lo + PAIR]
            w = jnp.concatenate([jnp.where(lane < SSM_HEAD_DIM, xd, 0.0),
                                 jnp.where(lane >= SSM_HEAD_DIM, xd, 0.0)], axis=0).astype(BF16)
            ydiag.append(_bdot(m, w))
        upd = lax.dot_general(bg, xw[:, g * gw:(g + 1) * gw].astype(BF16),
                              (((0,), (0,)), ((), ())), preferred_element_type=F32)
        stn.append(cdec[:, g * gw:(g + 1) * gw] * stg + upd)

    y = jnp.concatenate(ydiag, axis=1) + jnp.concatenate(yoff, axis=1) * eaf + dsk_ref[...] * xs
    y = y * _silu(z_ref[...])
    y_ref[...] = _rms(y, gnw_ref[...])
    st_new = jnp.concatenate(stn, axis=1)
    st_scr[...] = st_new
    st_out_ref[0] = st_new


def _ssd(proj, dt_raw, seq_of_chunk, first_of_chunk, pre_x, pre_bc, h0t, consts, inner):
    t = proj.shape[0]
    nch = t // CHUNK
    nseq = h0t.shape[0]
    bcw = pre_bc.shape[-1]
    assert (5 * inner) % bcw == 0
    cmap = lambda blk: (lambda c, s, f: (c, blk))
    smap3 = lambda c, s, f: (s[c], 0, 0)
    const2 = lambda c, s, f: (0, 0)
    grid_spec = pltpu.PrefetchScalarGridSpec(
        num_scalar_prefetch=2,
        grid=(nch,),
        in_specs=[pl.BlockSpec((CHUNK, inner), cmap(0)),
                  pl.BlockSpec((CHUNK, inner), cmap(1)),
                  pl.BlockSpec((CHUNK, bcw), cmap((5 * inner) // bcw)),
                  pl.BlockSpec((CHUNK, LANES), lambda c, s, f: (c, 0)),
                  pl.BlockSpec((1, 8, inner), smap3),
                  pl.BlockSpec((1, 8, bcw), smap3),
                  pl.BlockSpec((1, SSM_STATE, inner), smap3)]
                 + [pl.BlockSpec(a.shape, const2) for a in consts],
        out_specs=[pl.BlockSpec((CHUNK, inner), lambda c, s, f: (c, 0)),
                   pl.BlockSpec((1, SSM_STATE, inner), smap3)],
        scratch_shapes=[pltpu.VMEM((CHUNK + 8, inner), F32),
                        pltpu.VMEM((CHUNK + 8, bcw), F32),
                        pltpu.VMEM((SSM_STATE, inner), F32)],
    )
    return pl.pallas_call(
        _ssd_kernel,
        grid_spec=grid_spec,
        out_shape=[jax.ShapeDtypeStruct((t, inner), F32),
                   jax.ShapeDtypeStruct((nseq, SSM_STATE, inner), F32)],
        compiler_params=_cparams(("arbitrary",)),
        name="ssd",
    )(seq_of_chunk, first_of_chunk, proj, proj, proj, dt_raw, pre_x, pre_bc, h0t, *consts)


def _attn_pairs(q_ref, kwin, vtwin, bias_ref, o_ref, n_steps, n_masked_fn, out_rows):
    n_pairs = q_ref.shape[1] // PAIR
    rowp = lax.broadcasted_iota(I32, (PAIR, QPAIR), 0)
    krow = lax.broadcasted_iota(I32, (KWIN, 2 * QPAIR), 0)

    for jj in range(n_steps):
        n_masked = n_masked_fn(jj)

        def one_pair(hp, jj=jj, n_masked=n_masked):
            lo = pl.multiple_of(hp * PAIR, PAIR)
            q = q_ref[jj * QPAIR:(jj + 1) * QPAIR, pl.ds(lo, PAIR)] * (ATT_HEAD_DIM ** -0.5)
            qt = q.T
            w = jnp.concatenate([jnp.where(rowp < ATT_HEAD_DIM, qt, 0.0),
                                 jnp.where(rowp >= ATT_HEAD_DIM, qt, 0.0)], axis=1).astype(BF16)
            kb = kwin[jj * QPAIR:jj * QPAIR + KWIN, pl.ds(lo, PAIR)]
            s = _bdot(kb, w) + bias_ref[hp]
            if n_masked is not None:
                s = jnp.where(krow < n_masked, NEG_BIG, s)
            mx = jnp.max(s, axis=0, keepdims=True)
            p = jnp.exp(s - mx)
            den = jnp.sum(p, axis=0, keepdims=True)
            vb = vtwin[pl.ds(lo, PAIR), jj * QPAIR:jj * QPAIR + KWIN]
            o2 = _bdot(vb, p.astype(BF16)) / den
            ot = jnp.where(rowp < ATT_HEAD_DIM, o2[:, 0:QPAIR], o2[:, QPAIR:2 * QPAIR])
            o_ref[jj * out_rows:(jj + 1) * out_rows, pl.ds(lo, PAIR)] = ot.T[0:out_rows]

        def body(i, carry, one_pair=one_pair):
            for u in range(ATT_UNROLL):
                one_pair(i * ATT_UNROLL + u)
            return carry

        lax.fori_loop(0, n_pairs // ATT_UNROLL, body, 0)


def _attn_prompt_kernel(q_ref, kp_ref, kc_ref, vp_ref, vc_ref, bias_ref, o_ref, kwin, vtwin):
    i = pl.program_id(0)
    tq = q_ref.shape[0]
    kwin[0:ATT_PAST, :] = kp_ref[...].astype(BF16)
    kwin[ATT_PAST:ATT_PAST + tq, :] = kc_ref[...].astype(BF16)
    vtwin[:, 0:ATT_PAST] = vp_ref[...].T.astype(BF16)
    vtwin[:, ATT_PAST:ATT_PAST + tq] = vc_ref[...].T.astype(BF16)
    _attn_pairs(q_ref, kwin, vtwin, bias_ref, o_ref, tq // QPAIR,
                lambda jj: ATT_PAST - jj * QPAIR - i * tq, QPAIR)


def _attn_prompt(proj, bias2, t_prompt, width):
    tq = ATT_PAST
    assert t_prompt % tq == 0
    qb, kb, vb = 2, 3, 4
    prev = lambda i: jnp.maximum(i - 1, 0)
    return pl.pallas_call(
        _attn_prompt_kernel,
        grid=(t_prompt // tq,),
        in_specs=[pl.BlockSpec((tq, width), lambda i: (i, qb)),
                  pl.BlockSpec((tq, width), lambda i: (prev(i), kb)),
                  pl.BlockSpec((tq, width), lambda i: (i, kb)),
                  pl.BlockSpec((tq, width), lambda i: (prev(i), vb)),
                  pl.BlockSpec((tq, width), lambda i: (i, vb)),
                  pl.BlockSpec(bias2.shape, lambda i: (0, 0, 0))],
        out_specs=pl.BlockSpec((tq, width), lambda i: (i, 0)),
        out_shape=jax.ShapeDtypeStruct((t_prompt, width), F32),
        scratch_shapes=[pltpu.VMEM((ATT_PAST + tq, width), BF16),
                        pltpu.VMEM((width, ATT_PAST + tq), BF16)],
        compiler_params=_cparams(("arbitrary",)),
        name="attn_prompt",
    )(proj, proj, proj, proj, proj, bias2)


def _attn_sample_kernel(q_ref, kc_ref, ks_ref, vc_ref, vs_ref, bias_ref, o_ref, qpad, kwin, vtwin):
    width = q_ref.shape[1]
    qpad[0:CHUNK, :] = q_ref[...]
    qpad[CHUNK:QPAIR, :] = jnp.zeros((CHUNK, width), F32)
    kwin[0:ATT_PAST, :] = kc_ref[...].astype(BF16)
    kwin[ATT_PAST:BAND, :] = ks_ref[...].astype(BF16)
    kwin[BAND:KWIN, :] = jnp.zeros((KWIN - BAND, width), BF16)
    vtwin[:, 0:ATT_PAST] = vc_ref[...].T.astype(BF16)
    vtwin[:, ATT_PAST:KWIN] = jnp.concatenate(
        [vs_ref[...], jnp.zeros((KWIN - BAND, width), F32)], axis=0).T.astype(BF16)
    _attn_pairs(qpad, kwin, vtwin, bias_ref, o_ref, 1, lambda jj: None, CHUNK)


def _attn_sample(proj, cache_k, cache_v, bias2, t_prompt, n_seq, width):
    qb, kb, vb = 2, 3, 4
    c0 = t_prompt // CHUNK
    return pl.pallas_call(
        _attn_sample_kernel,
        grid=(n_seq,),
        in_specs=[pl.BlockSpec((CHUNK, width), lambda b: (c0 + b, qb)),
                  pl.BlockSpec((ATT_PAST, width), lambda b: (b, 0)),
                  pl.BlockSpec((CHUNK, width), lambda b: (c0 + b, kb)),
                  pl.BlockSpec((ATT_PAST, width), lambda b: (b, 0)),
                  pl.BlockSpec((CHUNK, width), lambda b: (c0 + b, vb)),
                  pl.BlockSpec(bias2.shape, lambda b: (0, 0, 0))],
        out_specs=pl.BlockSpec((CHUNK, width), lambda b: (b, 0)),
        out_shape=jax.ShapeDtypeStruct((n_seq * CHUNK, width), F32),
        scratch_shapes=[pltpu.VMEM((QPAIR, width), F32),
                        pltpu.VMEM((KWIN, width), BF16),
                        pltpu.VMEM((width, KWIN), BF16)],
        compiler_params=_cparams(("arbitrary",)),
        name="attn_sample",
    )(proj, cache_k, proj, cache_v, proj, bias2)


def _attn_bias(table):
    h = table.shape[0]
    x = np.arange(BAND + CHUNK - 1)
    rel = np.clip(BAND - 1 - x, -REL_CLIP, REL_CLIP) + REL_CLIP
    u = table[:, rel]
    std = jnp.stack([u[:, CHUNK - 1 - i:CHUNK - 1 - i + BAND] for i in range(CHUNK)], axis=1)
    neg = jnp.full((h, CHUNK, KWIN - BAND), NEG_BIG, F32)
    b = jnp.stack([jnp.concatenate([std, neg], axis=2),
                   jnp.concatenate([neg, std], axis=2)],
                  axis=1)
    b = b.reshape(h // 2, 2, 2, CHUNK, KWIN).transpose(0, 4, 1, 2, 3)
    return b.reshape(h // 2, KWIN, 2 * QPAIR)


def _outproj_kernel(y_ref, ap_ref, as_ref, xp_ref, xs_ref, gm_ref, scf_ref, shf_ref, npost_ref, npre_ref,
                    wo1_ref, wo2_ref, wrh_ref, wrl_ref, x1_ref, hf_ref, hb_ref, lg_ref, *, npt):
    i = pl.program_id(0)

    def body(a_ref, x_ref):
        tm, d = x_ref.shape
        mix = _bdot(y_ref[...].astype(BF16), wo1_ref[...]) + _bdot(a_ref[...].astype(BF16), wo2_ref[...])
        nm = _rms(mix, npost_ref[...]).reshape(tm // CHUNK, CHUNK, d)
        x1 = x_ref[...].reshape(tm // CHUNK, CHUNK, d) + gm_ref[...] * nm
        x1_ref[...] = x1.reshape(tm, d)
        hn = _rms(x1, npre_ref[...])
        hf = (hn * (1.0 + scf_ref[...]) + shf_ref[...]).reshape(tm, d)
        hf_ref[...] = hf
        h_hi = hf.astype(BF16)
        hb_ref[...] = h_hi
        h_lo = (hf - h_hi.astype(F32)).astype(BF16)
        lg_ref[...] = _bdot(h_hi, wrh_ref[...]) + _bdot(h_lo, wrh_ref[...]) + _bdot(h_hi, wrl_ref[...])

    @pl.when(i < npt)
    def _():
        body(ap_ref, xp_ref)

    @pl.when(i >= npt)
    def _():
        body(as_ref, xs_ref)


def _outproj(y_ssd, att_p, att_s, xp, xs, gate_m, scale_f, shift_f, npost, npre, wo1, wo2, wr):
    tp, d = xp.shape
    ts = xs.shape[0]
    t = tp + ts
    inner = y_ssd.shape[1]
    ne = wr.shape[1]
    wr_hi = wr.astype(BF16)
    wr_lo = (wr - wr_hi.astype(F32)).astype(BF16)
    tm = _pick((tp, ts), (256, 128, 64))
    nc = tm // CHUNK
    first, second = _split_rows(tp // tm)
    row = lambda i: (i, 0)
    tab = lambda i: (i, 0, 0)
    const = lambda i: (0, 0)
    return pl.pallas_call(
        functools.partial(_outproj_kernel, npt=tp // tm),
        grid=(t // tm,),
        in_specs=[pl.BlockSpec((tm, inner), row),
                  pl.BlockSpec((tm, att_p.shape[1]), first), pl.BlockSpec((tm, att_s.shape[1]), second),
                  pl.BlockSpec((tm, d), first), pl.BlockSpec((tm, d), second),
                  pl.BlockSpec((nc, 1, d), tab), pl.BlockSpec((nc, 1, d), tab), pl.BlockSpec((nc, 1, d), tab),
                  pl.BlockSpec((1, d), const), pl.BlockSpec((1, d), const),
                  pl.BlockSpec(wo1.shape, const), pl.BlockSpec(wo2.shape, const),
                  pl.BlockSpec(wr.shape, const), pl.BlockSpec(wr.shape, const)],
        out_specs=[pl.BlockSpec((tm, d), row), pl.BlockSpec((tm, d), row),
                   pl.BlockSpec((tm, d), row), pl.BlockSpec((tm, ne), row)],
        out_shape=[jax.ShapeDtypeStruct((t, d), F32), jax.ShapeDtypeStruct((t, d), F32),
                   jax.ShapeDtypeStruct((t, d), BF16), jax.ShapeDtypeStruct((t, ne), F32)],
        compiler_params=_cparams(("arbitrary",)),
        name="outproj",
    )(y_ssd, att_p, att_s, xp, xs, gate_m, scale_f, shift_f, npost, npre, wo1, wo2, wr_hi, wr_lo)


def _route_kernel(lg_ref, rb_ref, ti_ref, tw_ref):
    ne, tt = lg_ref.shape
    gs = ne // N_EXPERT_GROUPS
    scores = jax.nn.sigmoid(lg_ref[...])
    sel = scores + rb_ref[...]
    g3 = sel.reshape(N_EXPERT_GROUPS, gs, tt)
    i3 = lax.broadcasted_iota(I32, g3.shape, 1)
    m1 = jnp.max(g3, axis=1, keepdims=True)
    first = jnp.min(jnp.where(g3 == m1, i3, gs), axis=1, keepdims=True)
    m2 = jnp.max(jnp.where(i3 == first, -jnp.inf, g3), axis=1, keepdims=True)
    gscore = (m1 + m2).reshape(N_EXPERT_GROUPS, tt)
    gi = lax.broadcasted_iota(I32, gscore.shape, 0)
    gmask = jnp.zeros(gscore.shape, jnp.bool_)
    rem = gscore
    for _ in range(TOPK_GROUPS):
        mg = jnp.max(rem, axis=0, keepdims=True)
        pick = jnp.min(jnp.where(rem == mg, gi, N_EXPERT_GROUPS), axis=0, keepdims=True)
        hit = gi == pick
        gmask = gmask | hit
        rem = jnp.where(hit, -jnp.inf, rem)
    emask = jnp.broadcast_to(gmask.reshape(N_EXPERT_GROUPS, 1, tt), g3.shape).reshape(ne, tt)
    rem = jnp.where(emask, sel, -jnp.inf)
    ei = lax.broadcasted_iota(I32, (ne, tt), 0)
    idx, wts = [], []
    for _ in range(TOP_K):
        me = jnp.max(rem, axis=0, keepdims=True)
        pick = jnp.min(jnp.where(rem == me, ei, ne), axis=0, keepdims=True)
        hit = ei == pick
        idx.append(pick)
        wts.append(jnp.sum(jnp.where(hit, scores, 0.0), axis=0, keepdims=True))
        rem = jnp.where(hit, -jnp.inf, rem)
    w = jnp.concatenate(wts, axis=0)
    ti_ref[...] = jnp.concatenate(idx, axis=0)
    tw_ref[...] = w / jnp.sum(w, axis=0, keepdims=True) * ROUTED_SCALE


def _route(logits_t, router_bias):
    ne, t = logits_t.shape
    tt = _pick((t,), (2176, 2048, 1024, 512, 256, 128))
    return pl.pallas_call(
        _route_kernel,
        grid=(t // tt,),
        in_specs=[pl.BlockSpec((ne, tt), lambda i: (0, i)),
                  pl.BlockSpec((ne, 1), lambda i: (0, 0))],
        out_specs=[pl.BlockSpec((TOP_K, tt), lambda i: (0, i)),
                   pl.BlockSpec((TOP_K, tt), lambda i: (0, i))],
        out_shape=[jax.ShapeDtypeStruct((TOP_K, t), I32),
                   jax.ShapeDtypeStruct((TOP_K, t), F32)],
        compiler_params=_cparams(("arbitrary",)),
        name="route",
    )(logits_t, router_bias.reshape(ne, 1))


def _moe_kernel(be_ref, r0_ref, n_ref, first_ref, a_ref,
                hf_hbm, w1_ref, w3_ref, w2_ref, y8_hbm,
                xbuf_a, xbuf_b, obuf_a, obuf_b, w1b, w3b, w2b, gsem, ssem, *, n_tok):
    del be_ref
    b = pl.program_id(0)
    nb = pl.num_programs(0)
    tme = xbuf_a.shape[0]
    spare0 = TOP_K * n_tok

    def row(ref, r):
        return ref.at[pl.ds(r, 1)]

    def gather_all(xb, s):
        return pltpu.make_async_copy(hf_hbm.at[pl.ds(0, tme)], xb, gsem.at[s])

    def scatter_all(ob, s):
        return pltpu.make_async_copy(ob, y8_hbm.at[pl.ds(0, tme)], ssem.at[s])

    def start_gather(blk, xb, s):
        r0 = r0_ref[blk]
        for r in range(tme):
            pltpu.make_async_copy(row(hf_hbm, a_ref[r0 + r] >> 3), row(xb, r), gsem.at[s]).start()

    def start_scatter(blk, n_valid, ob, s):
        r0 = r0_ref[blk]
        for r in range(tme):
            a = a_ref[r0 + r]
            dst = jnp.where(r < n_valid, (a & (TOP_K - 1)) * n_tok + (a >> 3), spare0 + s * tme + r)
            pltpu.make_async_copy(row(ob, r), row(y8_hbm, dst), ssem.at[s]).start()

    @pl.when(b == 0)
    def _():
        start_gather(0, xbuf_a, 0)
        for s, ob in enumerate((obuf_a, obuf_b)):
            ob[...] = jnp.zeros_like(ob)
            spare = pltpu.make_async_copy(ob, y8_hbm.at[pl.ds(spare0 + s * tme, tme)], ssem.at[s])
            spare.start()
            spare.wait()

    @pl.when(first_ref[b] == 1)
    def _():
        w1b[...] = w1_ref[0].astype(BF16)
        w3b[...] = w3_ref[0].astype(BF16)
        w2b[...] = w2_ref[0].astype(BF16)

    nxt = jnp.minimum(b + 1, nb - 1)
    prv = jnp.maximum(b - 1, 0)
    n_prv = jnp.where(b >= 1, n_ref[prv], 0)

    def step(s, xb_cur, xb_nxt, ob_cur, ob_prv):
        gather_all(xb_cur, s).wait()

        @pl.when(b >= 1)
        def _():
            scatter_all(ob_cur, s).wait()

        start_gather(nxt, xb_nxt, 1 - s)
        x = xb_cur[...].astype(BF16)
        h = (_silu(_bdot(x, w1b[...])) * _bdot(x, w3b[...])).astype(BF16)
        ob_cur[...] = _bdot(h, w2b[...])
        start_scatter(prv, n_prv, ob_prv, 1 - s)

        @pl.when(b == nb - 1)
        def _():
            gather_all(xb_nxt, 1 - s).wait()
            scatter_all(ob_prv, 1 - s).wait()
            start_scatter(b, n_ref[b], ob_cur, s)
            scatter_all(ob_cur, s).wait()

    @pl.when(b % 2 == 0)
    def _():
        step(0, xbuf_a, xbuf_b, obuf_a, obuf_b)

    @pl.when(b % 2 == 1)
    def _():
        step(1, xbuf_b, xbuf_a, obuf_b, obuf_a)


def _moe(hf, a_sorted, blk_e, blk_r0, blk_n, blk_first, w1, w3, w2):
    ne, d, de = w1.shape
    t = hf.shape[0]
    nb = blk_e.shape[0]
    tme = MOE_ROWS
    assert (TOP_K * t) % tme == 0
    wmap = lambda b, be, r0, n, f, a: (be[b], 0, 0)
    grid_spec = pltpu.PrefetchScalarGridSpec(
        num_scalar_prefetch=5,
        grid=(nb,),
        in_specs=[pl.BlockSpec(memory_space=pl.ANY),
                  pl.BlockSpec((1, d, de), wmap),
                  pl.BlockSpec((1, d, de), wmap),
                  pl.BlockSpec((1, de, d), wmap)],
        out_specs=pl.BlockSpec(memory_space=pl.ANY),
        scratch_shapes=[pltpu.VMEM((tme, d), F32), pltpu.VMEM((tme, d), F32),
                        pltpu.VMEM((tme, d), F32), pltpu.VMEM((tme, d), F32),
                        pltpu.VMEM((d, de), BF16), pltpu.VMEM((d, de), BF16), pltpu.VMEM((de, d), BF16),
                        pltpu.SemaphoreType.DMA((2,)), pltpu.SemaphoreType.DMA((2,))],
    )
    return pl.pallas_call(
        functools.partial(_moe_kernel, n_tok=t),
        grid_spec=grid_spec,
        out_shape=jax.ShapeDtypeStruct((TOP_K * t + 2 * tme, d), F32),
        compiler_params=_cparams(("arbitrary",)),
        name="moe",
    )(blk_e, blk_r0, blk_n, blk_first, a_sorted, hf, w1, w3, w2)


def _dispatch_plan(topi_t, tme, ne):
    k, t = topi_t.shape
    assert k == TOP_K
    a_cnt = k * t
    shift = int(np.ceil(np.log2(a_cnt)))
    assert ne << shift < 2 ** 31
    a_id = jnp.arange(t, dtype=I32)[None, :] * k + jnp.arange(k, dtype=I32)[:, None]
    keys = (topi_t << shift) + a_id
    a_sorted = jnp.sort(keys.reshape(-1)) & ((1 << shift) - 1)
    a_sorted = jnp.concatenate([a_sorted, jnp.zeros((tme,), I32)])
    counts = jnp.sum(topi_t.reshape(-1, 1) == jnp.arange(ne, dtype=I32)[None, :], axis=0, dtype=I32)
    starts = jnp.cumsum(counts) - counts
    nblk = (counts + tme - 1) // tme
    blk_end = jnp.cumsum(nblk)
    nb = a_cnt // tme + ne
    b = jnp.arange(nb, dtype=I32)
    valid = b < blk_end[-1]
    e_raw = jnp.minimum(jnp.sum(blk_end[None, :] <= b[:, None], axis=1, dtype=I32), ne - 1)
    onehot = (e_raw[:, None] == jnp.arange(ne, dtype=I32)[None, :]).astype(I32)
    pick = lambda v: jnp.sum(onehot * v[None, :], axis=1)
    j = b - pick(blk_end - nblk)
    blk_r0 = jnp.where(valid, pick(starts) + j * tme, 0)
    blk_n = jnp.where(valid, jnp.minimum(tme, pick(counts) - j * tme), 0)
    blk_first = (valid & (j == 0)).astype(I32)
    e_last = jnp.max(jnp.where(valid, e_raw, 0))
    blk_e = jnp.where(valid, e_raw, e_last)
    return a_sorted, blk_e, blk_r0, blk_n, blk_first


def _final_kernel(*refs, npt):
    y8_refs = refs[:TOP_K]
    tw_ref, hb_ref, x1_ref, gf_ref, npost_ref, ws1_ref, ws3_ref, ws2_ref, op_ref, os_ref = refs[TOP_K:]
    i = pl.program_id(0)
    tm, d = x1_ref.shape
    tw = tw_ref[...]
    routed = y8_refs[0][...] * tw[:, 0:1]
    for k in range(1, TOP_K):
        routed = routed + y8_refs[k][...] * tw[:, k:k + 1]
    hb = hb_ref[...]
    shared = _bdot((_silu(_bdot(hb, ws1_ref[...])) * _bdot(hb, ws3_ref[...])).astype(BF16), ws2_ref[...])
    nm = _rms(routed + shared, npost_ref[...]).reshape(tm // CHUNK, CHUNK, d)
    y = (x1_ref[...].reshape(tm // CHUNK, CHUNK, d) + gf_ref[...] * nm).reshape(tm, d)

    @pl.when(i < npt)
    def _():
        op_ref[...] = y

    @pl.when(i >= npt)
    def _():
        os_ref[...] = y


def _final(y8, topw, hb, x1, gate_f, npost, ws1, ws3, ws2, tp):
    t, d = x1.shape
    ts = t - tp
    tm = _pick((tp, ts), (128, 64))
    nc = tm // CHUNK
    nt = t // tm
    first, second = _split_rows(tp // tm)
    row = lambda i: (i, 0)
    const = lambda i: (0, 0)
    planes = [pl.BlockSpec((tm, d), functools.partial(lambda i, k: (k * nt + i, 0), k=k))
              for k in range(TOP_K)]
    return pl.pallas_call(
        functools.partial(_final_kernel, npt=tp // tm),
        grid=(nt,),
        in_specs=planes + [pl.BlockSpec((tm, TOP_K), row),
                           pl.BlockSpec((tm, d), row), pl.BlockSpec((tm, d), row),
                           pl.BlockSpec((nc, 1, d), lambda i: (i, 0, 0)),
                           pl.BlockSpec((1, d), const),
                           pl.BlockSpec(ws1.shape, const), pl.BlockSpec(ws3.shape, const),
                           pl.BlockSpec(ws2.shape, const)],
        out_specs=[pl.BlockSpec((tm, d), first), pl.BlockSpec((tm, d), second)],
        out_shape=[jax.ShapeDtypeStruct((tp, d), F32), jax.ShapeDtypeStruct((ts, d), F32)],
        compiler_params=_cparams(("arbitrary",)),
        name="final",
    )(*([y8] * TOP_K), topw, hb, x1, gate_f, npost, ws1, ws3, ws2)


def kernel(x_prompt, x_sample, cache_conv, state_ssm, cache_k, cache_v, c_prompt, c_sample,
           w_ada, b_ada, norm_pre_mix, norm_post_mix, norm_pre_ffn, norm_post_ffn,
           w_in, conv_w, conv_b, dt_bias, a_log, d_skip, gn_w, rel_bias, w_out,
           w_router, router_bias, w1, w3, w2, ws1, ws3, ws2):
    assert w_ada.shape[0] == 1, "single layer"
    bp, lp, d = x_prompt.shape
    bs, ls, _ = x_sample.shape
    assert bp == 1 and ls == CHUNK and lp % ATT_PAST == 0
    assert cache_k.shape[2] == ATT_PAST
    heads = a_log.shape[1]
    inner = heads * SSM_HEAD_DIM
    att_w = rel_bias.shape[1] * ATT_HEAD_DIM
    assert att_w == inner
    bcw = 2 * SSM_GROUPS * SSM_STATE
    ne = w_router.shape[2]
    tp, ts = bp * lp, bs * ls
    nseq = bp + bs

    xp, xs = x_prompt.reshape(tp, d), x_sample.reshape(ts, d)
    seq_np = np.concatenate([np.repeat(np.arange(bp), lp // CHUNK), bp + np.arange(bs)]).astype(np.int32)
    first_np = np.concatenate([[1], (seq_np[1:] != seq_np[:-1])]).astype(np.int32)
    seq_of_chunk, first_of_chunk = jnp.asarray(seq_np), jnp.asarray(first_np)

    c_all = jnp.concatenate([c_prompt, c_sample], axis=0)
    c_pad = jnp.pad(c_all, ((0, -nseq % 8), (0, 0)))
    mod = _ada(c_pad, w_ada[0], b_ada[0])[:nseq].reshape(nseq, 6, d)
    mod_c = mod[seq_of_chunk]
    shift_m, scale_m, gate_m, shift_f, scale_f, gate_f = [mod_c[:, i:i + 1, :] for i in range(6)]

    wi = w_in[0]
    o_z, o_x, o_bc = 0, inner, 2 * inner
    o_dt = inner + inner + bcw
    o_q = o_dt + heads
    o_k, o_v = o_q + att_w, o_q + 2 * att_w
    cols = lambda o, n: wi[:, o:o + n]
    w_main = jnp.concatenate([cols(o_z, inner), cols(o_x, inner), cols(o_q, att_w), cols(o_k, att_w),
                              cols(o_v, att_w), cols(o_bc, bcw)], axis=1).astype(BF16)
    w_dt = jnp.pad(cols(o_dt, heads), ((0, 0), (0, LANES - heads))).astype(BF16)
    proj, dt_raw = _inproj(xp, xs, scale_m, shift_m, norm_pre_mix, w_main, w_dt)
    c_x, c_k, c_v, c_bc = inner, 3 * inner, 4 * inner, 5 * inner

    pad_rows = lambda a: jnp.pad(a, ((0, 0), (8 - (CONV_W - 1), 0), (0, 0)))
    pre = jnp.concatenate([jnp.zeros((bp, CONV_W - 1, inner + bcw), F32), cache_conv[0]], axis=0)
    pre_x, pre_bc = pad_rows(pre[:, :, :inner]), pad_rows(pre[:, :, inner:])
    h0 = jnp.concatenate([jnp.zeros((bp,) + state_ssm.shape[2:], F32), state_ssm[0]], axis=0)
    h0t = h0.transpose(0, 3, 1, 2).reshape(nseq, SSM_STATE, inner)
    lane_pad = lambda v: jnp.pad(v, (0, LANES - heads)).reshape(1, LANES)
    expand = (np.arange(LANES)[:, None] == (np.arange(inner)[None, :] // SSM_HEAD_DIM)).astype(np.float32)
    tri = np.tril(np.ones((CHUNK, CHUNK), np.float32))
    consts = (conv_w[0][:, :inner], conv_w[0][:, inner:],
              conv_b[0][:inner].reshape(1, inner), conv_b[0][inner:].reshape(1, bcw),
              lane_pad(dt_bias[0]), lane_pad(-jnp.exp(a_log[0])),
              jnp.repeat(d_skip[0], SSM_HEAD_DIM).reshape(1, inner), gn_w[0].reshape(1, inner),
              jnp.asarray(expand, BF16), jnp.asarray(tri, BF16))
    y_ssd, st_out = _ssd(proj, dt_raw, seq_of_chunk, first_of_chunk, pre_x, pre_bc, h0t, consts, inner)

    bias2 = _attn_bias(rel_bias[0])
    att_p = _attn_prompt(proj, bias2, tp, att_w)
    att_s = _attn_sample(proj, cache_k[0].reshape(bs * ATT_PAST, att_w), cache_v[0].reshape(bs * ATT_PAST, att_w),
                         bias2, tp, bs, att_w)

    wo = w_out[0].astype(BF16)
    x1, hf, hb, logits = _outproj(y_ssd, att_p, att_s, xp, xs, gate_m, scale_f, shift_f,
                                  norm_post_mix, norm_pre_ffn, wo[:inner], wo[inner:], w_router[0])

    topi_t, topw_t = _route(logits.T, router_bias[0])
    a_sorted, blk_e, blk_r0, blk_n, blk_first = _dispatch_plan(topi_t, MOE_ROWS, ne)
    y8 = _moe(hf, a_sorted, blk_e, blk_r0, blk_n, blk_first, w1[0], w3[0], w2[0])
    y_p, y_s = _final(y8, topw_t.T, hb, x1, gate_f, norm_post_ffn,
                      ws1[0].astype(BF16), ws3[0].astype(BF16), ws2[0].astype(BF16), tp)

    tail = lambda rows: jnp.concatenate([rows[..., c_x:c_x + inner], rows[..., c_bc:c_bc + bcw]], axis=-1)
    conv_prompt = tail(proj[tp - (CONV_W - 1):tp])[None, None]
    srows = lambda c0, n: proj[tp:, c0:c0 + n].reshape(bs, ls, n)[:, ls - (CONV_W - 1):, :]
    conv_sample = jnp.concatenate([srows(c_x, inner), srows(c_bc, bcw)], axis=-1)[None]
    st = st_out.reshape(nseq, SSM_STATE, heads, SSM_HEAD_DIM).transpose(0, 2, 3, 1)
    keep = min(ATT_PAST, lp)
    hd = (rel_bias.shape[1], ATT_HEAD_DIM)
    kv = lambda c0, r0, r1, b, l: proj[r0:r1, c0:c0 + att_w].reshape(b, l, *hd)[None]
    return (y_p.reshape(bp, lp, d), y_s.reshape(bs, ls, d),
            conv_prompt, st[:bp][None], kv(c_k, tp - keep, tp, bp, keep), kv(c_v, tp - keep, tp, bp, keep),
            conv_sample, st[bp:][None], kv(c_k, tp, tp + ts, bs, ls), kv(c_v, tp, tp + ts, bs, ls))
```

```python
import functools

import numpy as np
import jax
import jax.numpy as jnp
from jax import lax
from jax.experimental import pallas as pl
from jax.experimental.pallas import tpu as pltpu

F32 = jnp.float32
BF16 = jnp.bfloat16
I32 = jnp.int32
HIGHEST = lax.Precision.HIGHEST

CHUNK = 64
SSM_HEAD_DIM = 64
SSM_GROUPS = 2
SSM_STATE = 128
CONV_W = 4
ATT_HEAD_DIM = 64
LEFT_CHUNKS = 8
ATT_PAST = LEFT_CHUNKS * CHUNK
BAND = ATT_PAST + CHUNK
REL_CLIP = 128
TOP_K = 8
N_EXPERT_GROUPS = 8
TOPK_GROUPS = 4
ROUTED_SCALE = 2.5
EPS = 1e-6
NEG_BIG = -1e30

LANES = 128
PAIR = 2 * ATT_HEAD_DIM
QPAIR = 2 * CHUNK
KWIN = ATT_PAST + QPAIR
VMEM_LIMIT = 56 * 1024 * 1024
MOE_ROWS = 512
ATT_UNROLL = 8


def _cparams(sem):
    return pltpu.CompilerParams(dimension_semantics=sem, vmem_limit_bytes=VMEM_LIMIT)


def _pick(ns, cands):
    for c in cands:
        if all(n % c == 0 for n in ns):
            return c
    raise ValueError(f"no tile for {ns} in {cands}")


def _silu(x):
    return x * jax.nn.sigmoid(x)


def _rms(x, g):
    ms = jnp.mean(x * x, axis=-1, keepdims=True)
    return x * lax.rsqrt(ms + EPS) * g


def _bdot(a, b):
    return jnp.dot(a, b, preferred_element_type=F32)


def _split3(x):
    p0 = x.astype(BF16)
    r0 = x - p0.astype(F32)
    p1 = r0.astype(BF16)
    p2 = (r0 - p1.astype(F32)).astype(BF16)
    return p0, p1, p2


def _split_rows(npt):
    first = lambda i, *_: (jnp.minimum(i, npt - 1), 0)
    second = lambda i, *_: (jnp.maximum(i - npt, 0), 0)
    return first, second


def _ada_kernel(c_ref, w_ref, b_ref, o_ref):
    a = _silu(c_ref[...])
    o_ref[...] = jnp.dot(a, w_ref[...], precision=HIGHEST, preferred_element_type=F32) + b_ref[...]


def _ada(c_pad, w_ada, b_ada):
    m, d = c_pad.shape
    n = w_ada.shape[1]
    tn = _pick((n,), (1024, 512, 256, 128))
    return pl.pallas_call(
        _ada_kernel,
        grid=(n // tn,),
        in_specs=[pl.BlockSpec((m, d), lambda j: (0, 0)),
                  pl.BlockSpec((d, tn), lambda j: (0, j)),
                  pl.BlockSpec((1, tn), lambda j: (0, j))],
        out_specs=pl.BlockSpec((m, tn), lambda j: (0, j)),
        out_shape=jax.ShapeDtypeStruct((m, n), F32),
        compiler_params=_cparams(("arbitrary",)),
        name="ada",
    )(c_pad, w_ada, b_ada.reshape(1, n))


def _inproj_kernel(xp_ref, xs_ref, sc_ref, sh_ref, g_ref, w_ref, wdt_ref, o_ref, dt_ref, hm_ref, *, npt):
    i = pl.program_id(0)

    def prep(x_ref):
        x = x_ref[...]
        tm, d = x.shape
        y = _rms(x, g_ref[...]).reshape(tm // CHUNK, CHUNK, d)
        h = (y * (1.0 + sc_ref[...]) + sh_ref[...]).reshape(tm, d).astype(BF16)
        hm_ref[...] = h
        dt_ref[...] = _bdot(h, wdt_ref[...])

    @pl.when(pl.program_id(1) == 0)
    def _():
        @pl.when(i < npt)
        def _():
            prep(xp_ref)

        @pl.when(i >= npt)
        def _():
            prep(xs_ref)

    o_ref[...] = _bdot(hm_ref[...], w_ref[...])


def _inproj(xp, xs, scale, shift, g, w_main, w_dt):
    tp, d = xp.shape
    ts = xs.shape[0]
    t = tp + ts
    n = w_main.shape[1]
    tm = _pick((tp, ts), (1024, 512, 256, 128, 64))
    tn = _pick((n,), (512, 256, 128))
    nc = tm // CHUNK
    first, second = _split_rows(tp // tm)
    return pl.pallas_call(
        functools.partial(_inproj_kernel, npt=tp // tm),
        grid=(t // tm, n // tn),
        in_specs=[pl.BlockSpec((tm, d), first),
                  pl.BlockSpec((tm, d), second),
                  pl.BlockSpec((nc, 1, d), lambda i, j: (i, 0, 0)),
                  pl.BlockSpec((nc, 1, d), lambda i, j: (i, 0, 0)),
                  pl.BlockSpec((1, d), lambda i, j: (0, 0)),
                  pl.BlockSpec((d, tn), lambda i, j: (0, j)),
                  pl.BlockSpec((d, LANES), lambda i, j: (0, 0))],
        out_specs=[pl.BlockSpec((tm, tn), lambda i, j: (i, j)),
                   pl.BlockSpec((tm, LANES), lambda i, j: (i, 0))],
        out_shape=[jax.ShapeDtypeStruct((t, n), F32),
                   jax.ShapeDtypeStruct((t, LANES), F32)],
        scratch_shapes=[pltpu.VMEM((tm, d), BF16)],
        compiler_params=_cparams(("arbitrary", "arbitrary")),
        name="inproj",
    )(xp, xs, scale, shift, g, w_main, w_dt)


def _ssd_kernel(seq_ref, first_ref,
                z_ref, xs_ref, bc_ref, dt_ref, prex_ref, prebc_ref, h0_ref,
                cwx_ref, cwbc_ref, cbx_ref, cbbc_ref, dtb_ref, aneg_ref, dsk_ref, gnw_ref,
                e_ref, tri_ref,
                y_ref, st_out_ref,
                xpx_scr, xpbc_scr, st_scr):
    del seq_ref
    c = pl.program_id(0)
    inner = xs_ref.shape[1]
    gw = inner // SSM_GROUPS
    n = SSM_STATE
    pad = 8

    @pl.when(first_ref[c] == 1)
    def _():
        xpx_scr[0:pad, :] = prex_ref[0]
        xpbc_scr[0:pad, :] = prebc_ref[0]
        st_scr[...] = h0_ref[0]

    xpx_scr[pad:pad + CHUNK, :] = xs_ref[...]
    xpbc_scr[pad:pad + CHUNK, :] = bc_ref[...]

    def conv(xp, w_ref, b_ref):
        base = pad - (CONV_W - 1)
        acc = b_ref[...] + xp[base:base + CHUNK, :] * w_ref[0:1, :]
        for k in range(1, CONV_W):
            acc = acc + xp[base + k:base + k + CHUNK, :] * w_ref[k:k + 1, :]
        return _silu(acc)

    xs = conv(xpx_scr, cwx_ref, cbx_ref)
    bc = conv(xpbc_scr, cwbc_ref, cbbc_ref)
    xpx_scr[0:pad, :] = xpx_scr[CHUNK:CHUNK + pad, :]
    xpbc_scr[0:pad, :] = xpbc_scr[CHUNK:CHUNK + pad, :]

    dtv = dt_ref[...] + dtb_ref[...]
    dt = jnp.maximum(dtv, 0.0) + jnp.log(1.0 + jnp.exp(-jnp.abs(dtv)))
    da = dt * aneg_ref[...]
    acs = sum(_bdot(tri_ref[...], p) for p in _split3(da))
    full = sum(_bdot(p, e_ref[...]) for p in _split3(jnp.concatenate([dt, acs], axis=0)))
    dtf = full[0:CHUNK]
    af = full[CHUNK:2 * CHUNK]

    row = lax.broadcasted_iota(I32, (CHUNK, inner), 0)
    lj = lax.broadcasted_iota(I32, (CHUNK, inner), 1) & (SSM_HEAD_DIM - 1)
    aj = jnp.sum(jnp.where(row == lj, af, 0.0), axis=0, keepdims=True)
    lmat = jnp.exp(jnp.where(row >= lj, af - aj, NEG_BIG))
    alast = af[CHUNK - 1:CHUNK, :]
    xdt = xs * dtf
    xw = xdt * jnp.exp(alast - af)
    cdec = jnp.exp(alast)
    eaf = jnp.exp(af)

    lane = lax.broadcasted_iota(I32, (CHUNK, PAIR), 1)
    st = st_scr[...]
    ydiag, yoff, stn = [], [], []
    for g in range(SSM_GROUPS):
        bg = bc[:, g * n:(g + 1) * n].astype(BF16)
        cg = bc[:, (SSM_GROUPS + g) * n:(SSM_GROUPS + g + 1) * n].astype(BF16)
        bb = jnp.concatenate([bg, bg], axis=0)
        cbb = lax.dot_general(cg, bb, (((1,), (1,)), ((), ())), preferred_element_type=F32)
        stg = st[:, g * gw:(g + 1) * gw]
        yoff.append(_bdot(cg, stg.astype(BF16)))
        for p in range(gw // PAIR):
            lo = g * gw + p * PAIR
            m = (cbb * lmat[:, lo:lo + PAIR]).astype(BF16)
            xd = xdt[:, lo:lo + PAIR]
            w = jnp.concatenate([jnp.where(lane < SSM_HEAD_DIM, xd, 0.0),
                                 jnp.where(lane >= SSM_HEAD_DIM, xd, 0.0)], axis=0).astype(BF16)
            ydiag.append(_bdot(m, w))
        upd = lax.dot_general(bg, xw[:, g * gw:(g + 1) * gw].astype(BF16),
                              (((0,), (0,)), ((), ())), preferred_element_type=F32)
        stn.append(cdec[:, g * gw:(g + 1) * gw] * stg + upd)

    y = jnp.concatenate(ydiag, axis=1) + jnp.concatenate(yoff, axis=1) * eaf + dsk_ref[...] * xs
    y = y * _silu(z_ref[...])
    y_ref[...] = _rms(y, gnw_ref[...])
    st_new = jnp.concatenate(stn, axis=1)
    st_scr[...] = st_new
    st_out_ref[0] = st_new


def _ssd(proj, dt_raw, seq_of_chunk, first_of_chunk, pre_x, pre_bc, h0t, consts, inner):
    t = proj.shape[0]
    nch = t // CHUNK
    nseq = h0t.shape[0]
    bcw = pre_bc.shape[-1]
    assert (5 * inner) % bcw == 0
    cmap = lambda blk: (lambda c, s, f: (c, blk))
    smap3 = lambda c, s, f: (s[c], 0, 0)
    const2 = lambda c, s, f: (0, 0)
    grid_spec = pltpu.PrefetchScalarGridSpec(
        num_scalar_prefetch=2,
        grid=(nch,),
        in_specs=[pl.BlockSpec((CHUNK, inner), cmap(0)),
                  pl.BlockSpec((CHUNK, inner), cmap(1)),
                  pl.BlockSpec((CHUNK, bcw), cmap((5 * inner) // bcw)),
                  pl.BlockSpec((CHUNK, LANES), lambda c, s, f: (c, 0)),
                  pl.BlockSpec((1, 8, inner), smap3),
                  pl.BlockSpec((1, 8, bcw), smap3),
                  pl.BlockSpec((1, SSM_STATE, inner), smap3)]
                 + [pl.BlockSpec(a.shape, const2) for a in consts],
        out_specs=[pl.BlockSpec((CHUNK, inner), lambda c, s, f: (c, 0)),
                   pl.BlockSpec((1, SSM_STATE, inner), smap3)],
        scratch_shapes=[pltpu.VMEM((CHUNK + 8, inner), F32),
                        pltpu.VMEM((CHUNK + 8, bcw), F32),
                        pltpu.VMEM((SSM_STATE, inner), F32)],
    )
    return pl.pallas_call(
        _ssd_kernel,
        grid_spec=grid_spec,
        out_shape=[jax.ShapeDtypeStruct((t, inner), F32),
                   jax.ShapeDtypeStruct((nseq, SSM_STATE, inner), F32)],
        compiler_params=_cparams(("arbitrary",)),
        name="ssd",
    )(seq_of_chunk, first_of_chunk, proj, proj, proj, dt_raw, pre_x, pre_bc, h0t, *consts)


def _attn_pairs(q_ref, kwin, vtwin, bias_ref, o_ref, n_steps, n_masked_fn, out_rows):
    n_pairs = q_ref.shape[1] // PAIR
    rowp = lax.broadcasted_iota(I32, (PAIR, QPAIR), 0)
    krow = lax.broadcasted_iota(I32, (KWIN, 2 * QPAIR), 0)

    for jj in range(n_steps):
        n_masked = n_masked_fn(jj)

        def one_pair(hp, jj=jj, n_masked=n_masked):
            lo = hp * PAIR if isinstance(hp, int) else pl.multiple_of(hp * PAIR, PAIR)
            q = q_ref[jj * QPAIR:(jj + 1) * QPAIR, pl.ds(lo, PAIR)] * (ATT_HEAD_DIM ** -0.5)
            qt = q.T
            w = jnp.concatenate([jnp.where(rowp < ATT_HEAD_DIM, qt, 0.0),
                                 jnp.where(rowp >= ATT_HEAD_DIM, qt, 0.0)], axis=1).astype(BF16)
            kb = kwin[jj * QPAIR:jj * QPAIR + KWIN, pl.ds(lo, PAIR)]
            s = _bdot(kb, w) + bias_ref[hp]
            if n_masked is not None and n_masked > 0:
                s = jnp.where(krow < n_masked, NEG_BIG, s)
            mx = jnp.max(s, axis=0, keepdims=True)
            p = jnp.exp(s - mx)
            den = jnp.sum(p, axis=0, keepdims=True)
            vb = vtwin[pl.ds(lo, PAIR), jj * QPAIR:jj * QPAIR + KWIN]
            o2 = _bdot(vb, p.astype(BF16)) / den
            ot = jnp.where(rowp < ATT_HEAD_DIM, o2[:, 0:QPAIR], o2[:, QPAIR:2 * QPAIR])
            o_ref[jj * out_rows:(jj + 1) * out_rows, pl.ds(lo, PAIR)] = ot.T[0:out_rows]

        def body(i, carry, one_pair=one_pair):
            for u in range(ATT_UNROLL):
                one_pair(i * ATT_UNROLL + u)
            return carry

        if n_pairs == ATT_UNROLL:
            body(0, 0)
        else:
            lax.fori_loop(0, n_pairs // ATT_UNROLL, body, 0)


def _attn_prompt_kernel(q_ref, kp_ref, kc_ref, vp_ref, vc_ref, bias_ref, o_ref, kwin, vtwin):
    i = pl.program_id(0)
    tq = q_ref.shape[0]
    kwin[0:ATT_PAST, :] = kp_ref[...].astype(BF16)
    kwin[ATT_PAST:ATT_PAST + tq, :] = kc_ref[...].astype(BF16)
    vtwin[:, 0:ATT_PAST] = vp_ref[...].T.astype(BF16)
    vtwin[:, ATT_PAST:ATT_PAST + tq] = vc_ref[...].T.astype(BF16)
    @pl.when(i == 0)
    def _():
        _attn_pairs(q_ref, kwin, vtwin, bias_ref, o_ref, tq // QPAIR, lambda jj: ATT_PAST - jj * QPAIR, QPAIR)

    @pl.when(i > 0)
    def _():
        _attn_pairs(q_ref, kwin, vtwin, bias_ref, o_ref, tq // QPAIR, lambda jj: None, QPAIR)


def _attn_prompt(proj, bias2, t_prompt, width):
    tq = ATT_PAST
    assert t_prompt % tq == 0
    qb, kb, vb = 2, 3, 4
    prev = lambda i: jnp.maximum(i - 1, 0)
    return pl.pallas_call(
        _attn_prompt_kernel,
        grid=(t_prompt // tq,),
        in_specs=[pl.BlockSpec((tq, width), lambda i: (i, qb)),
                  pl.BlockSpec((tq, width), lambda i: (prev(i), kb)),
                  pl.BlockSpec((tq, width), lambda i: (i, kb)),
                  pl.BlockSpec((tq, width), lambda i: (prev(i), vb)),
                  pl.BlockSpec((tq, width), lambda i: (i, vb)),
                  pl.BlockSpec(bias2.shape, lambda i: (0, 0, 0))],
        out_specs=pl.BlockSpec((tq, width), lambda i: (i, 0)),
        out_shape=jax.ShapeDtypeStruct((t_prompt, width), F32),
        scratch_shapes=[pltpu.VMEM((ATT_PAST + tq, width), BF16),
                        pltpu.VMEM((width, ATT_PAST + tq), BF16)],
        compiler_params=_cparams(("arbitrary",)),
        name="attn_prompt",
    )(proj, proj, proj, proj, proj, bias2)


def _attn_sample_kernel(q_ref, kc_ref, ks_ref, vc_ref, vs_ref, bias_ref, o_ref, qpad, kwin, vtwin):
    width = q_ref.shape[1]
    qpad[0:CHUNK, :] = q_ref[...]
    qpad[CHUNK:QPAIR, :] = jnp.zeros((CHUNK, width), F32)
    kwin[0:ATT_PAST, :] = kc_ref[...].astype(BF16)
    kwin[ATT_PAST:BAND, :] = ks_ref[...].astype(BF16)
    kwin[BAND:KWIN, :] = jnp.zeros((KWIN - BAND, width), BF16)
    vtwin[:, 0:ATT_PAST] = vc_ref[...].T.astype(BF16)
    vtwin[:, ATT_PAST:KWIN] = jnp.concatenate(
        [vs_ref[...], jnp.zeros((KWIN - BAND, width), F32)], axis=0).T.astype(BF16)
    _attn_pairs(qpad, kwin, vtwin, bias_ref, o_ref, 1, lambda jj: None, CHUNK)


def _attn_sample(proj, cache_k, cache_v, bias2, t_prompt, n_seq, width):
    qb, kb, vb = 2, 3, 4
    c0 = t_prompt // CHUNK
    return pl.pallas_call(
        _attn_sample_kernel,
        grid=(n_seq,),
        in_specs=[pl.BlockSpec((CHUNK, width), lambda b: (c0 + b, qb)),
                  pl.BlockSpec((ATT_PAST, width), lambda b: (b, 0)),
                  pl.BlockSpec((CHUNK, width), lambda b: (c0 + b, kb)),
                  pl.BlockSpec((ATT_PAST, width), lambda b: (b, 0)),
                  pl.BlockSpec((CHUNK, width), lambda b: (c0 + b, vb)),
                  pl.BlockSpec(bias2.shape, lambda b: (0, 0, 0))],
        out_specs=pl.BlockSpec((CHUNK, width), lambda b: (b, 0)),
        out_shape=jax.ShapeDtypeStruct((n_seq * CHUNK, width), F32),
        scratch_shapes=[pltpu.VMEM((QPAIR, width), F32),
                        pltpu.VMEM((KWIN, width), BF16),
                        pltpu.VMEM((width, KWIN), BF16)],
        compiler_params=_cparams(("arbitrary",)),
        name="attn_sample",
    )(proj, cache_k, proj, cache_v, proj, bias2)


def _attn_bias(table):
    h = table.shape[0]
    x = np.arange(BAND + CHUNK - 1)
    rel = np.clip(BAND - 1 - x, -REL_CLIP, REL_CLIP) + REL_CLIP
    u = table[:, rel]
    std = jnp.stack([u[:, CHUNK - 1 - i:CHUNK - 1 - i + BAND] for i in range(CHUNK)], axis=1)
    neg = jnp.full((h, CHUNK, KWIN - BAND), NEG_BIG, F32)
    b = jnp.stack([jnp.concatenate([std, neg], axis=2),
                   jnp.concatenate([neg, std], axis=2)],
                  axis=1)
    b = b.reshape(h // 2, 2, 2, CHUNK, KWIN).transpose(0, 4, 1, 2, 3)
    return b.reshape(h // 2, KWIN, 2 * QPAIR)


def _outproj_kernel(y_ref, ap_ref, as_ref, xp_ref, xs_ref, gm_ref, scf_ref, shf_ref, npost_ref, npre_ref,
                    wo1_ref, wo2_ref, wrh_ref, wrl_ref, x1_ref, hf_ref, hb_ref, lg_ref, *, npt):
    i = pl.program_id(0)

    def body(a_ref, x_ref):
        tm, d = x_ref.shape
        mix = _bdot(y_ref[...].astype(BF16), wo1_ref[...]) + _bdot(a_ref[...].astype(BF16), wo2_ref[...])
        nm = _rms(mix, npost_ref[...]).reshape(tm // CHUNK, CHUNK, d)
        x1 = x_ref[...].reshape(tm // CHUNK, CHUNK, d) + gm_ref[...] * nm
        x1_ref[...] = x1.reshape(tm, d)
        hn = _rms(x1, npre_ref[...])
        hf = (hn * (1.0 + scf_ref[...]) + shf_ref[...]).reshape(tm, d)
        hf_ref[...] = hf
        h_hi = hf.astype(BF16)
        hb_ref[...] = h_hi
        h_lo = (hf - h_hi.astype(F32)).astype(BF16)
        lg_ref[...] = _bdot(h_hi, wrh_ref[...]) + _bdot(h_lo, wrh_ref[...]) + _bdot(h_hi, wrl_ref[...])

    @pl.when(i < npt)
    def _():
        body(ap_ref, xp_ref)

    @pl.when(i >= npt)
    def _():
        body(as_ref, xs_ref)


def _outproj(y_ssd, att_p, att_s, xp, xs, gate_m, scale_f, shift_f, npost, npre, wo1, wo2, wr):
    tp, d = xp.shape
    ts = xs.shape[0]
    t = tp + ts
    inner = y_ssd.shape[1]
    ne = wr.shape[1]
    wr_hi = wr.astype(BF16)
    wr_lo = (wr - wr_hi.astype(F32)).astype(BF16)
    tm = _pick((tp, ts), (256, 128, 64))
    nc = tm // CHUNK
    first, second = _split_rows(tp // tm)
    row = lambda i: (i, 0)
    tab = lambda i: (i, 0, 0)
    const = lambda i: (0, 0)
    return pl.pallas_call(
        functools.partial(_outproj_kernel, npt=tp // tm),
        grid=(t // tm,),
        in_specs=[pl.BlockSpec((tm, inner), row),
                  pl.BlockSpec((tm, att_p.shape[1]), first), pl.BlockSpec((tm, att_s.shape[1]), second),
                  pl.BlockSpec((tm, d), first), pl.BlockSpec((tm, d), second),
                  pl.BlockSpec((nc, 1, d), tab), pl.BlockSpec((nc, 1, d), tab), pl.BlockSpec((nc, 1, d), tab),
                  pl.BlockSpec((1, d), const), pl.BlockSpec((1, d), const),
                  pl.BlockSpec(wo1.shape, const), pl.BlockSpec(wo2.shape, const),
                  pl.BlockSpec(wr.shape, const), pl.BlockSpec(wr.shape, const)],
        out_specs=[pl.BlockSpec((tm, d), row), pl.BlockSpec((tm, d), row),
                   pl.BlockSpec((tm, d), row), pl.BlockSpec((tm, ne), row)],
        out_shape=[jax.ShapeDtypeStruct((t, d), F32), jax.ShapeDtypeStruct((t, d), F32),
                   jax.ShapeDtypeStruct((t, d), BF16), jax.ShapeDtypeStruct((t, ne), F32)],
        compiler_params=_cparams(("arbitrary",)),
        name="outproj",
    )(y_ssd, att_p, att_s, xp, xs, gate_m, scale_f, shift_f, npost, npre, wo1, wo2, wr_hi, wr_lo)


def _route_kernel(lg_ref, rb_ref, ti_ref, tw_ref):
    ne, tt = lg_ref.shape
    gs = ne // N_EXPERT_GROUPS
    scores = jax.nn.sigmoid(lg_ref[...])
    sel = scores + rb_ref[...]
    g3 = sel.reshape(N_EXPERT_GROUPS, gs, tt)
    i3 = lax.broadcasted_iota(I32, g3.shape, 1)
    m1 = jnp.max(g3, axis=1, keepdims=True)
    first = jnp.min(jnp.where(g3 == m1, i3, gs), axis=1, keepdims=True)
    m2 = jnp.max(jnp.where(i3 == first, -jnp.inf, g3), axis=1, keepdims=True)
    gscore = (m1 + m2).reshape(N_EXPERT_GROUPS, tt)
    gi = lax.broadcasted_iota(I32, gscore.shape, 0)
    gmask = jnp.zeros(gscore.shape, jnp.bool_)
    rem = gscore
    for _ in range(TOPK_GROUPS):
        mg = jnp.max(rem, axis=0, keepdims=True)
        pick = jnp.min(jnp.where(rem == mg, gi, N_EXPERT_GROUPS), axis=0, keepdims=True)
        hit = gi == pick
        gmask = gmask | hit
        rem = jnp.where(hit, -jnp.inf, rem)
    emask = jnp.broadcast_to(gmask.reshape(N_EXPERT_GROUPS, 1, tt), g3.shape).reshape(ne, tt)
    rem = jnp.where(emask, sel, -jnp.inf)
    ei = lax.broadcasted_iota(I32, (ne, tt), 0)
    idx, wts = [], []
    for _ in range(TOP_K):
        me = jnp.max(rem, axis=0, keepdims=True)
        pick = jnp.min(jnp.where(rem == me, ei, ne), axis=0, keepdims=True)
        hit = ei == pick
        idx.append(pick)
        wts.append(jnp.sum(jnp.where(hit, scores, 0.0), axis=0, keepdims=True))
        rem = jnp.where(hit, -jnp.inf, rem)
    w = jnp.concatenate(wts, axis=0)
    ti_ref[...] = jnp.concatenate(idx, axis=0)
    tw_ref[...] = w / jnp.sum(w, axis=0, keepdims=True) * ROUTED_SCALE


def _route(logits_t, router_bias):
    ne, t = logits_t.shape
    tt = _pick((t,), (2176, 2048, 1024, 512, 256, 128))
    return pl.pallas_call(
        _route_kernel,
        grid=(t // tt,),
        in_specs=[pl.BlockSpec((ne, tt), lambda i: (0, i)),
                  pl.BlockSpec((ne, 1), lambda i: (0, 0))],
        out_specs=[pl.BlockSpec((TOP_K, tt), lambda i: (0, i)),
                   pl.BlockSpec((TOP_K, tt), lambda i: (0, i))],
        out_shape=[jax.ShapeDtypeStruct((TOP_K, t), I32),
                   jax.ShapeDtypeStruct((TOP_K, t), F32)],
        compiler_params=_cparams(("arbitrary",)),
        name="route",
    )(logits_t, router_bias.reshape(ne, 1))


def _moe_kernel(be_ref, r0_ref, n_ref, first_ref, a_ref,
                hf_hbm, w1_ref, w3_ref, w2_ref, y8_hbm,
                xbuf_a, xbuf_b, obuf_a, obuf_b, w1b, w3b, w2b, gsem, ssem, *, n_tok):
    del be_ref
    b = pl.program_id(0)
    nb = pl.num_programs(0)
    tme = xbuf_a.shape[0]
    spare0 = TOP_K * n_tok

    def row(ref, r):
        return ref.at[pl.ds(r, 1)]

    def gather_all(xb, s):
        return pltpu.make_async_copy(hf_hbm.at[pl.ds(0, tme)], xb, gsem.at[s])

    def scatter_all(ob, s):
        return pltpu.make_async_copy(ob, y8_hbm.at[pl.ds(0, tme)], ssem.at[s])

    def start_gather(blk, xb, s):
        r0 = r0_ref[blk]
        for r in range(tme):
            pltpu.make_async_copy(row(hf_hbm, a_ref[r0 + r] >> 3), row(xb, r), gsem.at[s]).start()

    def start_scatter(blk, n_valid, ob, s):
        r0 = r0_ref[blk]
        for r in range(tme):
            a = a_ref[r0 + r]
            dst = jnp.where(r < n_valid, (a & (TOP_K - 1)) * n_tok + (a >> 3), spare0 + s * tme + r)
            pltpu.make_async_copy(row(ob, r), row(y8_hbm, dst), ssem.at[s]).start()

    @pl.when(b == 0)
    def _():
        start_gather(0, xbuf_a, 0)
        for s, ob in enumerate((obuf_a, obuf_b)):
            ob[...] = jnp.zeros_like(ob)
            spare = pltpu.make_async_copy(ob, y8_hbm.at[pl.ds(spare0 + s * tme, tme)], ssem.at[s])
            spare.start()
            spare.wait()

    @pl.when(first_ref[b] == 1)
    def _():
        w1b[...] = w1_ref[0].astype(BF16)
        w3b[...] = w3_ref[0].astype(BF16)
        w2b[...] = w2_ref[0].astype(BF16)

    nxt = jnp.minimum(b + 1, nb - 1)
    prv = jnp.maximum(b - 1, 0)
    n_prv = jnp.where(b >= 1, n_ref[prv], 0)

    def step(s, xb_cur, xb_nxt, ob_cur, ob_prv):
        gather_all(xb_cur, s).wait()

        @pl.when(b >= 1)
        def _():
            scatter_all(ob_cur, s).wait()

        start_gather(nxt, xb_nxt, 1 - s)
        x = xb_cur[...].astype(BF16)
        h = (_silu(_bdot(x, w1b[...])) * _bdot(x, w3b[...])).astype(BF16)
        ob_cur[...] = _bdot(h, w2b[...])
        start_scatter(prv, n_prv, ob_prv, 1 - s)

        @pl.when(b == nb - 1)
        def _():
            gather_all(xb_nxt, 1 - s).wait()
            scatter_all(ob_prv, 1 - s).wait()
            start_scatter(b, n_ref[b], ob_cur, s)
            scatter_all(ob_cur, s).wait()

    @pl.when(b % 2 == 0)
    def _():
        step(0, xbuf_a, xbuf_b, obuf_a, obuf_b)

    @pl.when(b % 2 == 1)
    def _():
        step(1, xbuf_b, xbuf_a, obuf_b, obuf_a)


def _moe(hf, a_sorted, blk_e, blk_r0, blk_n, blk_first, w1, w3, w2):
    ne, d, de = w1.shape
    t = hf.shape[0]
    nb = blk_e.shape[0]
    tme = MOE_ROWS
    assert (TOP_K * t) % tme == 0
    wmap = lambda b, be, r0, n, f, a: (be[b], 0, 0)
    grid_spec = pltpu.PrefetchScalarGridSpec(
        num_scalar_prefetch=5,
        grid=(nb,),
        in_specs=[pl.BlockSpec(memory_space=pl.ANY),
                  pl.BlockSpec((1, d, de), wmap),
                  pl.BlockSpec((1, d, de), wmap),
                  pl.BlockSpec((1, de, d), wmap)],
        out_specs=pl.BlockSpec(memory_space=pl.ANY),
        scratch_shapes=[pltpu.VMEM((tme, d), F32), pltpu.VMEM((tme, d), F32),
                        pltpu.VMEM((tme, d), F32), pltpu.VMEM((tme, d), F32),
                        pltpu.VMEM((d, de), BF16), pltpu.VMEM((d, de), BF16), pltpu.VMEM((de, d), BF16),
                        pltpu.SemaphoreType.DMA((2,)), pltpu.SemaphoreType.DMA((2,))],
    )
    return pl.pallas_call(
        functools.partial(_moe_kernel, n_tok=t),
        grid_spec=grid_spec,
        out_shape=jax.ShapeDtypeStruct((TOP_K * t + 2 * tme, d), F32),
        compiler_params=_cparams(("arbitrary",)),
        name="moe",
    )(blk_e, blk_r0, blk_n, blk_first, a_sorted, hf, w1, w3, w2)


def _dispatch_plan(topi_t, tme, ne):
    k, t = topi_t.shape
    assert k == TOP_K
    a_cnt = k * t
    shift = int(np.ceil(np.log2(a_cnt)))
    assert ne << shift < 2 ** 31
    a_id = jnp.arange(t, dtype=I32)[None, :] * k + jnp.arange(k, dtype=I32)[:, None]
    keys = (topi_t << shift) + a_id
    a_sorted = jnp.sort(keys.reshape(-1)) & ((1 << shift) - 1)
    a_sorted = jnp.concatenate([a_sorted, jnp.zeros((tme,), I32)])
    counts = jnp.sum(topi_t.reshape(-1, 1) == jnp.arange(ne, dtype=I32)[None, :], axis=0, dtype=I32)
    starts = jnp.cumsum(counts) - counts
    nblk = (counts + tme - 1) // tme
    blk_end = jnp.cumsum(nblk)
    nb = a_cnt // tme + ne
    b = jnp.arange(nb, dtype=I32)
    valid = b < blk_end[-1]
    e_raw = jnp.minimum(jnp.sum(blk_end[None, :] <= b[:, None], axis=1, dtype=I32), ne - 1)
    onehot = (e_raw[:, None] == jnp.arange(ne, dtype=I32)[None, :]).astype(I32)
    pick = lambda v: jnp.sum(onehot * v[None, :], axis=1)
    j = b - pick(blk_end - nblk)
    blk_r0 = jnp.where(valid, pick(starts) + j * tme, 0)
    blk_n = jnp.where(valid, jnp.minimum(tme, pick(counts) - j * tme), 0)
    blk_first = (valid & (j == 0)).astype(I32)
    e_last = jnp.max(jnp.where(valid, e_raw, 0))
    blk_e = jnp.where(valid, e_raw, e_last)
    return a_sorted, blk_e, blk_r0, blk_n, blk_first


def _final_kernel(*refs, npt):
    y8_refs = refs[:TOP_K]
    tw_ref, hb_ref, x1_ref, gf_ref, npost_ref, ws1_ref, ws3_ref, ws2_ref, op_ref, os_ref = refs[TOP_K:]
    i = pl.program_id(0)
    tm, d = x1_ref.shape
    tw = tw_ref[...]
    routed = y8_refs[0][...] * tw[:, 0:1]
    for k in range(1, TOP_K):
        routed = routed + y8_refs[k][...] * tw[:, k:k + 1]
    hb = hb_ref[...]
    shared = _bdot((_silu(_bdot(hb, ws1_ref[...])) * _bdot(hb, ws3_ref[...])).astype(BF16), ws2_ref[...])
    nm = _rms(routed + shared, npost_ref[...]).reshape(tm // CHUNK, CHUNK, d)
    y = (x1_ref[...].reshape(tm // CHUNK, CHUNK, d) + gf_ref[...] * nm).reshape(tm, d)

    @pl.when(i < npt)
    def _():
        op_ref[...] = y

    @pl.when(i >= npt)
    def _():
        os_ref[...] = y


def _final(y8, topw, hb, x1, gate_f, npost, ws1, ws3, ws2, tp):
    t, d = x1.shape
    ts = t - tp
    tm = _pick((tp, ts), (128, 64))
    nc = tm // CHUNK
    nt = t // tm
    first, second = _split_rows(tp // tm)
    row = lambda i: (i, 0)
    const = lambda i: (0, 0)
    planes = [pl.BlockSpec((tm, d), functools.partial(lambda i, k: (k * nt + i, 0), k=k))
              for k in range(TOP_K)]
    return pl.pallas_call(
        functools.partial(_final_kernel, npt=tp // tm),
        grid=(nt,),
        in_specs=planes + [pl.BlockSpec((tm, TOP_K), row),
                           pl.BlockSpec((tm, d), row), pl.BlockSpec((tm, d), row),
                           pl.BlockSpec((nc, 1, d), lambda i: (i, 0, 0)),
                           pl.BlockSpec((1, d), const),
                           pl.BlockSpec(ws1.shape, const), pl.BlockSpec(ws3.shape, const),
                           pl.BlockSpec(ws2.shape, const)],
        out_specs=[pl.BlockSpec((tm, d), first), pl.BlockSpec((tm, d), second)],
        out_shape=[jax.ShapeDtypeStruct((tp, d), F32), jax.ShapeDtypeStruct((ts, d), F32)],
        compiler_params=_cparams(("arbitrary",)),
        name="final",
    )(*([y8] * TOP_K), topw, hb, x1, gate_f, npost, ws1, ws3, ws2)


def kernel(x_prompt, x_sample, cache_conv, state_ssm, cache_k, cache_v, c_prompt, c_sample,
           w_ada, b_ada, norm_pre_mix, norm_post_mix, norm_pre_ffn, norm_post_ffn,
           w_in, conv_w, conv_b, dt_bias, a_log, d_skip, gn_w, rel_bias, w_out,
           w_router, router_bias, w1, w3, w2, ws1, ws3, ws2):
    assert w_ada.shape[0] == 1, "single layer"
    bp, lp, d = x_prompt.shape
    bs, ls, _ = x_sample.shape
    assert bp == 1 and ls == CHUNK and lp % ATT_PAST == 0
    assert cache_k.shape[2] == ATT_PAST
    heads = a_log.shape[1]
    inner = heads * SSM_HEAD_DIM
    att_w = rel_bias.shape[1] * ATT_HEAD_DIM
    assert att_w == inner
    bcw = 2 * SSM_GROUPS * SSM_STATE
    ne = w_router.shape[2]
    tp, ts = bp * lp, bs * ls
    nseq = bp + bs

    xp, xs = x_prompt.reshape(tp, d), x_sample.reshape(ts, d)
    seq_np = np.concatenate([np.repeat(np.arange(bp), lp // CHUNK), bp + np.arange(bs)]).astype(np.int32)
    first_np = np.concatenate([[1], (seq_np[1:] != seq_np[:-1])]).astype(np.int32)
    seq_of_chunk, first_of_chunk = jnp.asarray(seq_np), jnp.asarray(first_np)

    c_all = jnp.concatenate([c_prompt, c_sample], axis=0)
    c_pad = jnp.pad(c_all, ((0, -nseq % 8), (0, 0)))
    mod = _ada(c_pad, w_ada[0], b_ada[0])[:nseq].reshape(nseq, 6, d)
    mod_c = mod[seq_of_chunk]
    shift_m, scale_m, gate_m, shift_f, scale_f, gate_f = [mod_c[:, i:i + 1, :] for i in range(6)]

    wi = w_in[0]
    o_z, o_x, o_bc = 0, inner, 2 * inner
    o_dt = inner + inner + bcw
    o_q = o_dt + heads
    o_k, o_v = o_q + att_w, o_q + 2 * att_w
    cols = lambda o, n: wi[:, o:o + n]
    w_main = jnp.concatenate([cols(o_z, inner), cols(o_x, inner), cols(o_q, att_w), cols(o_k, att_w),
                              cols(o_v, att_w), cols(o_bc, bcw)], axis=1).astype(BF16)
    w_dt = jnp.pad(cols(o_dt, heads), ((0, 0), (0, LANES - heads))).astype(BF16)
    proj, dt_raw = _inproj(xp, xs, scale_m, shift_m, norm_pre_mix, w_main, w_dt)
    c_x, c_k, c_v, c_bc = inner, 3 * inner, 4 * inner, 5 * inner

    pad_rows = lambda a: jnp.pad(a, ((0, 0), (8 - (CONV_W - 1), 0), (0, 0)))
    pre = jnp.concatenate([jnp.zeros((bp, CONV_W - 1, inner + bcw), F32), cache_conv[0]], axis=0)
    pre_x, pre_bc = pad_rows(pre[:, :, :inner]), pad_rows(pre[:, :, inner:])
    h0 = jnp.concatenate([jnp.zeros((bp,) + state_ssm.shape[2:], F32), state_ssm[0]], axis=0)
    h0t = h0.transpose(0, 3, 1, 2).reshape(nseq, SSM_STATE, inner)
    lane_pad = lambda v: jnp.pad(v, (0, LANES - heads)).reshape(1, LANES)
    expand = (np.arange(LANES)[:, None] == (np.arange(inner)[None, :] // SSM_HEAD_DIM)).astype(np.float32)
    tri = np.tril(np.ones((CHUNK, CHUNK), np.float32))
    consts = (conv_w[0][:, :inner], conv_w[0][:, inner:],
              conv_b[0][:inner].reshape(1, inner), conv_b[0][inner:].reshape(1, bcw),
              lane_pad(dt_bias[0]), lane_pad(-jnp.exp(a_log[0])),
              jnp.repeat(d_skip[0], SSM_HEAD_DIM).reshape(1, inner), gn_w[0].reshape(1, inner),
              jnp.asarray(expand, BF16), jnp.asarray(tri, BF16))
    y_ssd, st_out = _ssd(proj, dt_raw, seq_of_chunk, first_of_chunk, pre_x, pre_bc, h0t, consts, inner)

    bias2 = _attn_bias(rel_bias[0])
    att_p = _attn_prompt(proj, bias2, tp, att_w)
    att_s = _attn_sample(proj, cache_k[0].reshape(bs * ATT_PAST, att_w), cache_v[0].reshape(bs * ATT_PAST, att_w),
                         bias2, tp, bs, att_w)

    wo = w_out[0].astype(BF16)
    x1, hf, hb, logits = _outproj(y_ssd, att_p, att_s, xp, xs, gate_m, scale_f, shift_f,
                                  norm_post_mix, norm_pre_ffn, wo[:inner], wo[inner:], w_router[0])

    topi_t, topw_t = _route(logits.T, router_bias[0])
    a_sorted, blk_e, blk_r0, blk_n, blk_first = _dispatch_plan(topi_t, MOE_ROWS, ne)
    y8 = _moe(hf, a_sorted, blk_e, blk_r0, blk_n, blk_first, w1[0], w3[0], w2[0])
    y_p, y_s = _final(y8, topw_t.T, hb, x1, gate_f, norm_post_ffn,
                      ws1[0].astype(BF16), ws3[0].astype(BF16), ws2[0].astype(BF16), tp)

    tail = lambda rows: jnp.concatenate([rows[..., c_x:c_x + inner], rows[..., c_bc:c_bc + bcw]], axis=-1)
    conv_prompt = tail(proj[tp - (CONV_W - 1):tp])[None, None]
    srows = lambda c0, n: proj[tp:, c0:c0 + n].reshape(bs, ls, n)[:, ls - (CONV_W - 1):, :]
    conv_sample = jnp.concatenate([srows(c_x, inner), srows(c_bc, bcw)], axis=-1)[None]
    st = st_out.reshape(nseq, SSM_STATE, heads, SSM_HEAD_DIM).transpose(0, 2, 3, 1)
    keep = min(ATT_PAST, lp)
    hd = (rel_bias.shape[1], ATT_HEAD_DIM)
    kv = lambda c0, r0, r1, b, l: proj[r0:r1, c0:c0 + att_w].reshape(b, l, *hd)[None]
    return (y_p.reshape(bp, lp, d), y_s.reshape(bs, ls, d),
            conv_prompt, st[:bp][None], kv(c_k, tp - keep, tp, bp, keep), kv(c_v, tp - keep, tp, bp, keep),
            conv_sample, st[bp:][None], kv(c_k, tp, tp + ts, bs, ls), kv(c_v, tp, tp + ts, bs, ls))
```

```python
import functools

import numpy as np
import jax
import jax.numpy as jnp
from jax import lax
from jax.experimental import pallas as pl
from jax.experimental.pallas import tpu as pltpu

F32 = jnp.float32
BF16 = jnp.bfloat16
I32 = jnp.int32
HIGHEST = lax.Precision.HIGHEST

CHUNK = 64
SSM_HEAD_DIM = 64
SSM_GROUPS = 2
SSM_STATE = 128
CONV_W = 4
ATT_HEAD_DIM = 64
LEFT_CHUNKS = 8
ATT_PAST = LEFT_CHUNKS * CHUNK
BAND = ATT_PAST + CHUNK
REL_CLIP = 128
TOP_K = 8
N_EXPERT_GROUPS = 8
TOPK_GROUPS = 4
ROUTED_SCALE = 2.5
EPS = 1e-6
NEG_BIG = -1e30

LANES = 128
PAIR = 2 * ATT_HEAD_DIM
QPAIR = 2 * CHUNK
KWIN = ATT_PAST + QPAIR
VMEM_LIMIT = 56 * 1024 * 1024
MOE_ROWS = 512
ATT_UNROLL = 8


def _cparams(sem):
    return pltpu.CompilerParams(dimension_semantics=sem, vmem_limit_bytes=VMEM_LIMIT)


def _pick(ns, cands):
    for c in cands:
        if all(n % c == 0 for n in ns):
            return c
    raise ValueError(f"no tile for {ns} in {cands}")


def _silu(x):
    return x * jax.nn.sigmoid(x)


def _rms(x, g):
    ms = jnp.mean(x * x, axis=-1, keepdims=True)
    return x * lax.rsqrt(ms + EPS) * g


def _bdot(a, b):
    return jnp.dot(a, b, preferred_element_type=F32)


def _split3(x):
    p0 = x.astype(BF16)
    r0 = x - p0.astype(F32)
    p1 = r0.astype(BF16)
    p2 = (r0 - p1.astype(F32)).astype(BF16)
    return p0, p1, p2


def _split_rows(npt):
    first = lambda i, *_: (jnp.minimum(i, npt - 1), 0)
    second = lambda i, *_: (jnp.maximum(i - npt, 0), 0)
    return first, second


def _ada_kernel(c_ref, w_ref, b_ref, o_ref):
    a = _silu(c_ref[...])
    o_ref[...] = jnp.dot(a, w_ref[...], precision=HIGHEST, preferred_element_type=F32) + b_ref[...]


def _ada(c_pad, w_ada, b_ada):
    m, d = c_pad.shape
    n = w_ada.shape[1]
    tn = _pick((n,), (1024, 512, 256, 128))
    return pl.pallas_call(
        _ada_kernel,
        grid=(n // tn,),
        in_specs=[pl.BlockSpec((m, d), lambda j: (0, 0)),
                  pl.BlockSpec((d, tn), lambda j: (0, j)),
                  pl.BlockSpec((1, tn), lambda j: (0, j))],
        out_specs=pl.BlockSpec((m, tn), lambda j: (0, j)),
        out_shape=jax.ShapeDtypeStruct((m, n), F32),
        compiler_params=_cparams(("arbitrary",)),
        name="ada",
    )(c_pad, w_ada, b_ada.reshape(1, n))


def _inproj_kernel(xp_ref, xs_ref, sc_ref, sh_ref, g_ref, w_ref, wdt_ref, o_ref, dt_ref, hm_ref, *, npt):
    i = pl.program_id(0)

    def prep(x_ref):
        x = x_ref[...]
        tm, d = x.shape
        y = _rms(x, g_ref[...]).reshape(tm // CHUNK, CHUNK, d)
        h = (y * (1.0 + sc_ref[...]) + sh_ref[...]).reshape(tm, d).astype(BF16)
        hm_ref[...] = h
        dt_ref[...] = _bdot(h, wdt_ref[...])

    @pl.when(pl.program_id(1) == 0)
    def _():
        @pl.when(i < npt)
        def _():
            prep(xp_ref)

        @pl.when(i >= npt)
        def _():
            prep(xs_ref)

    o_ref[...] = _bdot(hm_ref[...], w_ref[...])


def _inproj(xp, xs, scale, shift, g, w_main, w_dt):
    tp, d = xp.shape
    ts = xs.shape[0]
    t = tp + ts
    n = w_main.shape[1]
    tm = _pick((tp, ts), (1024, 512, 256, 128, 64))
    tn = _pick((n,), (512, 256, 128))
    nc = tm // CHUNK
    first, second = _split_rows(tp // tm)
    return pl.pallas_call(
        functools.partial(_inproj_kernel, npt=tp // tm),
        grid=(t // tm, n // tn),
        in_specs=[pl.BlockSpec((tm, d), first),
                  pl.BlockSpec((tm, d), second),
                  pl.BlockSpec((nc, 1, d), lambda i, j: (i, 0, 0)),
                  pl.BlockSpec((nc, 1, d), lambda i, j: (i, 0, 0)),
                  pl.BlockSpec((1, d), lambda i, j: (0, 0)),
                  pl.BlockSpec((d, tn), lambda i, j: (0, j)),
                  pl.BlockSpec((d, LANES), lambda i, j: (0, 0))],
        out_specs=[pl.BlockSpec((tm, tn), lambda i, j: (i, j)),
                   pl.BlockSpec((tm, LANES), lambda i, j: (i, 0))],
        out_shape=[jax.ShapeDtypeStruct((t, n), F32),
                   jax.ShapeDtypeStruct((t, LANES), F32)],
        scratch_shapes=[pltpu.VMEM((tm, d), BF16)],
        compiler_params=_cparams(("arbitrary", "arbitrary")),
        name="inproj",
    )(xp, xs, scale, shift, g, w_main, w_dt)


def _ssd_kernel(seq_ref, first_ref,
                z_ref, xs_ref, bc_ref, dt_ref, prex_ref, prebc_ref, h0_ref,
                cwx_ref, cwbc_ref, cbx_ref, cbbc_ref, dtb_ref, aneg_ref, dsk_ref, gnw_ref,
                e_ref, tri_ref,
                y_ref, st_out_ref,
                xpx_scr, xpbc_scr, st_scr):
    del seq_ref
    c = pl.program_id(0)
    inner = xs_ref.shape[1]
    gw = inner // SSM_GROUPS
    n = SSM_STATE
    pad = 8

    @pl.when(first_ref[c] == 1)
    def _():
        xpx_scr[0:pad, :] = prex_ref[0]
        xpbc_scr[0:pad, :] = prebc_ref[0]
        st_scr[...] = h0_ref[0]

    xpx_scr[pad:pad + CHUNK, :] = xs_ref[...]
    xpbc_scr[pad:pad + CHUNK, :] = bc_ref[...]

    def conv(xp, w_ref, b_ref):
        base = pad - (CONV_W - 1)
        acc = b_ref[...] + xp[base:base + CHUNK, :] * w_ref[0:1, :]
        for k in range(1, CONV_W):
            acc = acc + xp[base + k:base + k + CHUNK, :] * w_ref[k:k + 1, :]
        return _silu(acc)

    xs = conv(xpx_scr, cwx_ref, cbx_ref)
    bc = conv(xpbc_scr, cwbc_ref, cbbc_ref)
    xpx_scr[0:pad, :] = xpx_scr[CHUNK:CHUNK + pad, :]
    xpbc_scr[0:pad, :] = xpbc_scr[CHUNK:CHUNK + pad, :]

    dtv = dt_ref[...] + dtb_ref[...]
    dt = jnp.maximum(dtv, 0.0) + jnp.log(1.0 + jnp.exp(-jnp.abs(dtv)))
    da = dt * aneg_ref[...]
    acs = sum(_bdot(tri_ref[...], p) for p in _split3(da))
    full = sum(_bdot(p, e_ref[...]) for p in _split3(jnp.concatenate([dt, acs], axis=0)))
    dtf = full[0:CHUNK]
    af = full[CHUNK:2 * CHUNK]

    row = lax.broadcasted_iota(I32, (CHUNK, inner), 0)
    lj = lax.broadcasted_iota(I32, (CHUNK, inner), 1) & (SSM_HEAD_DIM - 1)
    aj = jnp.sum(jnp.where(row == lj, af, 0.0), axis=0, keepdims=True)
    lmat = jnp.exp(jnp.where(row >= lj, af - aj, NEG_BIG))
    alast = af[CHUNK - 1:CHUNK, :]
    xdt = xs * dtf
    xw = xdt * jnp.exp(alast - af)
    cdec = jnp.exp(alast)
    eaf = jnp.exp(af)

    lane = lax.broadcasted_iota(I32, (CHUNK, PAIR), 1)
    st = st_scr[...]
    ydiag, yoff, stn = [], [], []
    for g in range(SSM_GROUPS):
        bg = bc[:, g * n:(g + 1) * n].astype(BF16)
        cg = bc[:, (SSM_GROUPS + g) * n:(SSM_GROUPS + g + 1) * n].astype(BF16)
        bb = jnp.concatenate([bg, bg], axis=0)
        cbb = lax.dot_general(cg, bb, (((1,), (1,)), ((), ())), preferred_element_type=F32)
        stg = st[:, g * gw:(g + 1) * gw]
        yoff.append(_bdot(cg, stg.astype(BF16)))
        for p in range(gw // PAIR):
            lo = g * gw + p * PAIR
            m = (cbb * lmat[:, lo:lo + PAIR]).astype(BF16)
            xd = xdt[:, lo:lo + PAIR]
            w = jnp.concatenate([jnp.where(lane < SSM_HEAD_DIM, xd, 0.0),
                                 jnp.where(lane >= SSM_HEAD_DIM, xd, 0.0)], axis=0).astype(BF16)
            ydiag.append(_bdot(m, w))
        upd = lax.dot_general(bg, xw[:, g * gw:(g + 1) * gw].astype(BF16),
                              (((0,), (0,)), ((), ())), preferred_element_type=F32)
        stn.append(cdec[:, g * gw:(g + 1) * gw] * stg + upd)

    y = jnp.concatenate(ydiag, axis=1) + jnp.concatenate(yoff, axis=1) * eaf + dsk_ref[...] * xs
    y = y * _silu(z_ref[...])
    y_ref[...] = _rms(y, gnw_ref[...])
    st_new = jnp.concatenate(stn, axis=1)
    st_scr[...] = st_new
    st_out_ref[0] = st_new


def _ssd(proj, dt_raw, seq_of_chunk, first_of_chunk, pre_x, pre_bc, h0t, consts, inner):
    t = proj.shape[0]
    nch = t // CHUNK
    nseq = h0t.shape[0]
    bcw = pre_bc.shape[-1]
    assert (5 * inner) % bcw == 0
    cmap = lambda blk: (lambda c, s, f: (c, blk))
    smap3 = lambda c, s, f: (s[c], 0, 0)
    const2 = lambda c, s, f: (0, 0)
    grid_spec = pltpu.PrefetchScalarGridSpec(
        num_scalar_prefetch=2,
        grid=(nch,),
        in_specs=[pl.BlockSpec((CHUNK, inner), cmap(0)),
                  pl.BlockSpec((CHUNK, inner), cmap(1)),
                  pl.BlockSpec((CHUNK, bcw), cmap((5 * inner) // bcw)),
                  pl.BlockSpec((CHUNK, LANES), lambda c, s, f: (c, 0)),
                  pl.BlockSpec((1, 8, inner), smap3),
                  pl.BlockSpec((1, 8, bcw), smap3),
                  pl.BlockSpec((1, SSM_STATE, inner), smap3)]
                 + [pl.BlockSpec(a.shape, const2) for a in consts],
        out_specs=[pl.BlockSpec((CHUNK, inner), lambda c, s, f: (c, 0)),
                   pl.BlockSpec((1, SSM_STATE, inner), smap3)],
        scratch_shapes=[pltpu.VMEM((CHUNK + 8, inner), F32),
                        pltpu.VMEM((CHUNK + 8, bcw), F32),
                        pltpu.VMEM((SSM_STATE, inner), F32)],
    )
    return pl.pallas_call(
        _ssd_kernel,
        grid_spec=grid_spec,
        out_shape=[jax.ShapeDtypeStruct((t, inner), F32),
                   jax.ShapeDtypeStruct((nseq, SSM_STATE, inner), F32)],
        compiler_params=_cparams(("arbitrary",)),
        name="ssd",
    )(seq_of_chunk, first_of_chunk, proj, proj, proj, dt_raw, pre_x, pre_bc, h0t, *consts)


def _attn_pairs(q_ref, kwin, vtwin, bias_ref, o_ref, n_steps, n_masked_fn, out_rows):
    n_pairs = q_ref.shape[1] // PAIR
    rowp = lax.broadcasted_iota(I32, (PAIR, QPAIR), 0)
    krow = lax.broadcasted_iota(I32, (KWIN, 2 * QPAIR), 0)

    for jj in range(n_steps):
        n_masked = n_masked_fn(jj)

        def one_pair(hp, jj=jj, n_masked=n_masked):
            lo = hp * PAIR if isinstance(hp, int) else pl.multiple_of(hp * PAIR, PAIR)
            q = q_ref[jj * QPAIR:(jj + 1) * QPAIR, pl.ds(lo, PAIR)] * (ATT_HEAD_DIM ** -0.5)
            qt = q.T
            w = jnp.concatenate([jnp.where(rowp < ATT_HEAD_DIM, qt, 0.0),
                                 jnp.where(rowp >= ATT_HEAD_DIM, qt, 0.0)], axis=1).astype(BF16)
            kb = kwin[jj * QPAIR:jj * QPAIR + KWIN, pl.ds(lo, PAIR)]
            s = _bdot(kb, w) + bias_ref[hp]
            if n_masked is not None and n_masked > 0:
                s = jnp.where(krow < n_masked, NEG_BIG, s)
            mx = jnp.max(s, axis=0, keepdims=True)
            p = jnp.exp(s - mx)
            den = jnp.sum(p, axis=0, keepdims=True)
            vb = vtwin[pl.ds(lo, PAIR), jj * QPAIR:jj * QPAIR + KWIN]
            o2 = _bdot(vb, p.astype(BF16)) / den
            ot = jnp.where(rowp < ATT_HEAD_DIM, o2[:, 0:QPAIR], o2[:, QPAIR:2 * QPAIR])
            o_ref[jj * out_rows:(jj + 1) * out_rows, pl.ds(lo, PAIR)] = ot.T[0:out_rows]

        def body(i, carry, one_pair=one_pair):
            for u in range(ATT_UNROLL):
                one_pair(i * ATT_UNROLL + u)
            return carry

        if n_pairs == ATT_UNROLL:
            body(0, 0)
        else:
            lax.fori_loop(0, n_pairs // ATT_UNROLL, body, 0)


def _attn_prompt_kernel(q_ref, kp_ref, kc_ref, vp_ref, vc_ref, bias_ref, o_ref, kwin, vtwin):
    i = pl.program_id(0)
    tq = q_ref.shape[0]
    kwin[0:ATT_PAST, :] = kp_ref[...].astype(BF16)
    kwin[ATT_PAST:ATT_PAST + tq, :] = kc_ref[...].astype(BF16)
    vtwin[:, 0:ATT_PAST] = vp_ref[...].T.astype(BF16)
    vtwin[:, ATT_PAST:ATT_PAST + tq] = vc_ref[...].T.astype(BF16)
    @pl.when(i == 0)
    def _():
        _attn_pairs(q_ref, kwin, vtwin, bias_ref, o_ref, tq // QPAIR, lambda jj: ATT_PAST - jj * QPAIR, QPAIR)

    @pl.when(i > 0)
    def _():
        _attn_pairs(q_ref, kwin, vtwin, bias_ref, o_ref, tq // QPAIR, lambda jj: None, QPAIR)


def _attn_prompt(proj, bias2, t_prompt, width):
    tq = ATT_PAST
    assert t_prompt % tq == 0
    qb, kb, vb = 2, 3, 4
    prev = lambda i: jnp.maximum(i - 1, 0)
    return pl.pallas_call(
        _attn_prompt_kernel,
        grid=(t_prompt // tq,),
        in_specs=[pl.BlockSpec((tq, width), lambda i: (i, qb)),
                  pl.BlockSpec((tq, width), lambda i: (prev(i), kb)),
                  pl.BlockSpec((tq, width), lambda i: (i, kb)),
                  pl.BlockSpec((tq, width), lambda i: (prev(i), vb)),
                  pl.BlockSpec((tq, width), lambda i: (i, vb)),
                  pl.BlockSpec(bias2.shape, lambda i: (0, 0, 0))],
        out_specs=pl.BlockSpec((tq, width), lambda i: (i, 0)),
        out_shape=jax.ShapeDtypeStruct((t_prompt, width), F32),
        scratch_shapes=[pltpu.VMEM((ATT_PAST + tq, width), BF16),
                        pltpu.VMEM((width, ATT_PAST + tq), BF16)],
        compiler_params=_cparams(("arbitrary",)),
        name="attn_prompt",
    )(proj, proj, proj, proj, proj, bias2)


def _attn_sample_kernel(q_ref, kc_ref, ks_ref, vc_ref, vs_ref, bias_ref, o_ref, qpad, kwin, vtwin):
    width = q_ref.shape[1]
    qpad[0:CHUNK, :] = q_ref[...]
    qpad[CHUNK:QPAIR, :] = jnp.zeros((CHUNK, width), F32)
    kwin[0:ATT_PAST, :] = kc_ref[...].astype(BF16)
    kwin[ATT_PAST:BAND, :] = ks_ref[...].astype(BF16)
    kwin[BAND:KWIN, :] = jnp.zeros((KWIN - BAND, width), BF16)
    vtwin[:, 0:ATT_PAST] = vc_ref[...].T.astype(BF16)
    vtwin[:, ATT_PAST:KWIN] = jnp.concatenate(
        [vs_ref[...], jnp.zeros((KWIN - BAND, width), F32)], axis=0).T.astype(BF16)
    _attn_pairs(qpad, kwin, vtwin, bias_ref, o_ref, 1, lambda jj: None, CHUNK)


def _attn_sample(proj, cache_k, cache_v, bias2, t_prompt, n_seq, width):
    qb, kb, vb = 2, 3, 4
    c0 = t_prompt // CHUNK
    return pl.pallas_call(
        _attn_sample_kernel,
        grid=(n_seq,),
        in_specs=[pl.BlockSpec((CHUNK, width), lambda b: (c0 + b, qb)),
                  pl.BlockSpec((ATT_PAST, width), lambda b: (b, 0)),
                  pl.BlockSpec((CHUNK, width), lambda b: (c0 + b, kb)),
                  pl.BlockSpec((ATT_PAST, width), lambda b: (b, 0)),
                  pl.BlockSpec((CHUNK, width), lambda b: (c0 + b, vb)),
                  pl.BlockSpec(bias2.shape, lambda b: (0, 0, 0))],
        out_specs=pl.BlockSpec((CHUNK, width), lambda b: (b, 0)),
        out_shape=jax.ShapeDtypeStruct((n_seq * CHUNK, width), F32),
        scratch_shapes=[pltpu.VMEM((QPAIR, width), F32),
                        pltpu.VMEM((KWIN, width), BF16),
                        pltpu.VMEM((width, KWIN), BF16)],
        compiler_params=_cparams(("arbitrary",)),
        name="attn_sample",
    )(proj, cache_k, proj, cache_v, proj, bias2)


def _attn_bias(table):
    h = table.shape[0]
    x = np.arange(BAND + CHUNK - 1)
    rel = np.clip(BAND - 1 - x, -REL_CLIP, REL_CLIP) + REL_CLIP
    u = table[:, rel]
    std = jnp.stack([u[:, CHUNK - 1 - i:CHUNK - 1 - i + BAND] for i in range(CHUNK)], axis=1)
    neg = jnp.full((h, CHUNK, KWIN - BAND), NEG_BIG, F32)
    b = jnp.stack([jnp.concatenate([std, neg], axis=2),
                   jnp.concatenate([neg, std], axis=2)],
                  axis=1)
    b = b.reshape(h // 2, 2, 2, CHUNK, KWIN).transpose(0, 4, 1, 2, 3)
    return b.reshape(h // 2, KWIN, 2 * QPAIR)


def _outproj_kernel(y_ref, ap_ref, as_ref, xp_ref, xs_ref, gm_ref, scf_ref, shf_ref, npost_ref, npre_ref,
                    wo1_ref, wo2_ref, wrh_ref, wrl_ref, x1_ref, hf_ref, hb_ref, lg_ref, *, npt):
    i = pl.program_id(0)

    def body(a_ref, x_ref):
        tm, d = x_ref.shape
        mix = _bdot(y_ref[...].astype(BF16), wo1_ref[...]) + _bdot(a_ref[...].astype(BF16), wo2_ref[...])
        nm = _rms(mix, npost_ref[...]).reshape(tm // CHUNK, CHUNK, d)
        x1 = x_ref[...].reshape(tm // CHUNK, CHUNK, d) + gm_ref[...] * nm
        x1_ref[...] = x1.reshape(tm, d)
        hn = _rms(x1, npre_ref[...])
        hf = (hn * (1.0 + scf_ref[...]) + shf_ref[...]).reshape(tm, d)
        hf_ref[...] = hf
        h_hi = hf.astype(BF16)
        hb_ref[...] = h_hi
        h_lo = (hf - h_hi.astype(F32)).astype(BF16)
        lg_ref[...] = _bdot(h_hi, wrh_ref[...]) + _bdot(h_lo, wrh_ref[...]) + _bdot(h_hi, wrl_ref[...])

    @pl.when(i < npt)
    def _():
        body(ap_ref, xp_ref)

    @pl.when(i >= npt)
    def _():
        body(as_ref, xs_ref)


def _outproj(y_ssd, att_p, att_s, xp, xs, gate_m, scale_f, shift_f, npost, npre, wo1, wo2, wr):
    tp, d = xp.shape
    ts = xs.shape[0]
    t = tp + ts
    inner = y_ssd.shape[1]
    ne = wr.shape[1]
    wr_hi = wr.astype(BF16)
    wr_lo = (wr - wr_hi.astype(F32)).astype(BF16)
    tm = _pick((tp, ts), (256, 128, 64))
    nc = tm // CHUNK
    first, second = _split_rows(tp // tm)
    row = lambda i: (i, 0)
    tab = lambda i: (i, 0, 0)
    const = lambda i: (0, 0)
    return pl.pallas_call(
        functools.partial(_outproj_kernel, npt=tp // tm),
        grid=(t // tm,),
        in_specs=[pl.BlockSpec((tm, inner), row),
                  pl.BlockSpec((tm, att_p.shape[1]), first), pl.BlockSpec((tm, att_s.shape[1]), second),
                  pl.BlockSpec((tm, d), first), pl.BlockSpec((tm, d), second),
                  pl.BlockSpec((nc, 1, d), tab), pl.BlockSpec((nc, 1, d), tab), pl.BlockSpec((nc, 1, d), tab),
                  pl.BlockSpec((1, d), const), pl.BlockSpec((1, d), const),
                  pl.BlockSpec(wo1.shape, const), pl.BlockSpec(wo2.shape, const),
                  pl.BlockSpec(wr.shape, const), pl.BlockSpec(wr.shape, const)],
        out_specs=[pl.BlockSpec((tm, d), row), pl.BlockSpec((tm, d), row),
                   pl.BlockSpec((tm, d), row), pl.BlockSpec((tm, ne), row)],
        out_shape=[jax.ShapeDtypeStruct((t, d), F32), jax.ShapeDtypeStruct((t, d), F32),
                   jax.ShapeDtypeStruct((t, d), BF16), jax.ShapeDtypeStruct((t, ne), F32)],
        compiler_params=_cparams(("arbitrary",)),
        name="outproj",
    )(y_ssd, att_p, att_s, xp, xs, gate_m, scale_f, shift_f, npost, npre, wo1, wo2, wr_hi, wr_lo)


def _route_kernel(lg_ref, rb_ref, ti_ref, tw_ref):
    ne, tt = lg_ref.shape
    gs = ne // N_EXPERT_GROUPS
    scores = jax.nn.sigmoid(lg_ref[...])
    sel = scores + rb_ref[...]
    g3 = sel.reshape(N_EXPERT_GROUPS, gs, tt)
    i3 = lax.broadcasted_iota(I32, g3.shape, 1)
    m1 = jnp.max(g3, axis=1, keepdims=True)
    first = jnp.min(jnp.where(g3 == m1, i3, gs), axis=1, keepdims=True)
    m2 = jnp.max(jnp.where(i3 == first, -jnp.inf, g3), axis=1, keepdims=True)
    gscore = (m1 + m2).reshape(N_EXPERT_GROUPS, tt)
    gi = lax.broadcasted_iota(I32, gscore.shape, 0)
    gmask = jnp.zeros(gscore.shape, jnp.bool_)
    rem = gscore
    for _ in range(TOPK_GROUPS):
        mg = jnp.max(rem, axis=0, keepdims=True)
        pick = jnp.min(jnp.where(rem == mg, gi, N_EXPERT_GROUPS), axis=0, keepdims=True)
        hit = gi == pick
        gmask = gmask | hit
        rem = jnp.where(hit, -jnp.inf, rem)
    emask = jnp.broadcast_to(gmask.reshape(N_EXPERT_GROUPS, 1, tt), g3.shape).reshape(ne, tt)
    rem = jnp.where(emask, sel, -jnp.inf)
    ei = lax.broadcasted_iota(I32, (ne, tt), 0)
    idx, wts = [], []
    for _ in range(TOP_K):
        me = jnp.max(rem, axis=0, keepdims=True)
        pick = jnp.min(jnp.where(rem == me, ei, ne), axis=0, keepdims=True)
        hit = ei == pick
        idx.append(pick)
        wts.append(jnp.sum(jnp.where(hit, scores, 0.0), axis=0, keepdims=True))
        rem = jnp.where(hit, -jnp.inf, rem)
    w = jnp.concatenate(wts, axis=0)
    ti_ref[...] = jnp.concatenate(idx, axis=0)
    tw_ref[...] = w / jnp.sum(w, axis=0, keepdims=True) * ROUTED_SCALE


def _route(logits_t, router_bias):
    ne, t = logits_t.shape
    tt = _pick((t,), (2176, 2048, 1024, 512, 256, 128))
    return pl.pallas_call(
        _route_kernel,
        grid=(t // tt,),
        in_specs=[pl.BlockSpec((ne, tt), lambda i: (0, i)),
                  pl.BlockSpec((ne, 1), lambda i: (0, 0))],
        out_specs=[pl.BlockSpec((TOP_K, tt), lambda i: (0, i)),
                   pl.BlockSpec((TOP_K, tt), lambda i: (0, i))],
        out_shape=[jax.ShapeDtypeStruct((TOP_K, t), I32),
                   jax.ShapeDtypeStruct((TOP_K, t), F32)],
        compiler_params=_cparams(("arbitrary",)),
        name="route",
    )(logits_t, router_bias.reshape(ne, 1))


def _moe_kernel(be_ref, r0_ref, first_ref, a_ref,
                hf_hbm, w1_ref, w3_ref, w2_ref, o_ref,
                xbuf_a, xbuf_b, w1b, w3b, w2b, gsem):
    del be_ref
    b = pl.program_id(0)
    nb = pl.num_programs(0)
    tme = xbuf_a.shape[0]

    def gather_all(xb, s):
        return pltpu.make_async_copy(hf_hbm.at[pl.ds(0, tme)], xb, gsem.at[s])

    def start_gather(blk, xb, s):
        r0 = r0_ref[blk]
        for r in range(tme):
            tok = a_ref[r0 + r] >> 3
            pltpu.make_async_copy(hf_hbm.at[pl.ds(tok, 1)], xb.at[pl.ds(r, 1)], gsem.at[s]).start()

    @pl.when(b == 0)
    def _():
        start_gather(0, xbuf_a, 0)

    @pl.when(first_ref[b] == 1)
    def _():
        w1b[...] = w1_ref[0].astype(BF16)
        w3b[...] = w3_ref[0].astype(BF16)
        w2b[...] = w2_ref[0].astype(BF16)

    nxt = jnp.minimum(b + 1, nb - 1)

    def step(s, xb_cur, xb_nxt):
        gather_all(xb_cur, s).wait()
        start_gather(nxt, xb_nxt, 1 - s)
        x = xb_cur[...].astype(BF16)
        h = (_silu(_bdot(x, w1b[...])) * _bdot(x, w3b[...])).astype(BF16)
        o_ref[...] = _bdot(h, w2b[...])

        @pl.when(b == nb - 1)
        def _():
            gather_all(xb_nxt, 1 - s).wait()

    @pl.when(b % 2 == 0)
    def _():
        step(0, xbuf_a, xbuf_b)

    @pl.when(b % 2 == 1)
    def _():
        step(1, xbuf_b, xbuf_a)


def _moe(hf, a_sorted, blk_e, blk_r0, blk_first, w1, w3, w2):
    ne, d, de = w1.shape
    nb = blk_e.shape[0]
    tme = MOE_ROWS
    wmap = lambda b, be, r0, f, a: (be[b], 0, 0)
    grid_spec = pltpu.PrefetchScalarGridSpec(
        num_scalar_prefetch=4,
        grid=(nb,),
        in_specs=[pl.BlockSpec(memory_space=pl.ANY),
                  pl.BlockSpec((1, d, de), wmap),
                  pl.BlockSpec((1, d, de), wmap),
                  pl.BlockSpec((1, de, d), wmap)],
        out_specs=pl.BlockSpec((tme, d), lambda b, be, r0, f, a: (b, 0)),
        scratch_shapes=[pltpu.VMEM((tme, d), F32), pltpu.VMEM((tme, d), F32),
                        pltpu.VMEM((d, de), BF16), pltpu.VMEM((d, de), BF16), pltpu.VMEM((de, d), BF16),
                        pltpu.SemaphoreType.DMA((2,))],
    )
    return pl.pallas_call(
        _moe_kernel,
        grid_spec=grid_spec,
        out_shape=jax.ShapeDtypeStruct((nb * tme, d), F32),
        compiler_params=_cparams(("arbitrary",)),
        name="moe",
    )(blk_e, blk_r0, blk_first, a_sorted, hf, w1, w3, w2)


def _dispatch_plan(topi_t, tme, ne):
    k, t = topi_t.shape
    assert k == TOP_K
    a_cnt = k * t
    shift = int(np.ceil(np.log2(a_cnt)))
    assert ne << shift < 2 ** 31
    a_id = jnp.arange(t, dtype=I32)[None, :] * k + jnp.arange(k, dtype=I32)[:, None]
    keys = (topi_t << shift) + a_id
    a_sorted = jnp.sort(keys.reshape(-1)) & ((1 << shift) - 1)
    a_sorted = jnp.concatenate([a_sorted, jnp.zeros((tme,), I32)])
    is_e = topi_t[:, :, None] == jnp.arange(ne, dtype=I32)[None, None, :]
    chose = jnp.sum(is_e, axis=0, dtype=I32)
    rank = jnp.cumsum(chose, axis=0) - chose
    counts = jnp.sum(chose, axis=0)
    starts = jnp.cumsum(counts) - counts
    nblk = (counts + tme - 1) // tme
    blk_end = jnp.cumsum(nblk)
    nb = a_cnt // tme + ne
    b = jnp.arange(nb, dtype=I32)
    valid = b < blk_end[-1]
    e_raw = jnp.minimum(jnp.sum(blk_end[None, :] <= b[:, None], axis=1, dtype=I32), ne - 1)
    onehot = (e_raw[:, None] == jnp.arange(ne, dtype=I32)[None, :]).astype(I32)
    pick = lambda v: jnp.sum(onehot * v[None, :], axis=1)
    j = b - pick(blk_end - nblk)
    blk_r0 = jnp.where(valid, pick(starts) + j * tme, 0)
    blk_first = (valid & (j == 0)).astype(I32)
    e_last = jnp.max(jnp.where(valid, e_raw, 0))
    blk_e = jnp.where(valid, e_raw, e_last)
    out_row = ((blk_end - nblk) * tme)[None, :] + rank
    pos = jnp.sum(jnp.where(is_e, out_row[None], 0), axis=2).T.reshape(-1)
    return a_sorted, blk_e, blk_r0, blk_first, pos


def _final_kernel(pos_ref, eo_hbm, tw_ref, hb_ref, x1_ref, gf_ref, npost_ref, ws1_ref, ws3_ref, ws2_ref,
                  op_ref, os_ref, rows_a, rows_b, sem, *, npt):
    i = pl.program_id(0)
    nt = pl.num_programs(0)
    tm, d = x1_ref.shape

    def gather_all(buf, s):
        return pltpu.make_async_copy(eo_hbm.at[pl.ds(0, TOP_K * tm)], buf, sem.at[s])

    def start_gather(tile, buf, s):
        base = tile * (TOP_K * tm)
        for t in range(tm):
            for k in range(TOP_K):
                src = eo_hbm.at[pl.ds(pos_ref[base + t * TOP_K + k], 1)]
                pltpu.make_async_copy(src, buf.at[pl.ds(k * tm + t, 1)], sem.at[s]).start()

    @pl.when(i == 0)
    def _():
        start_gather(0, rows_a, 0)

    nxt = jnp.minimum(i + 1, nt - 1)

    def step(s, cur, other):
        gather_all(cur, s).wait()
        start_gather(nxt, other, 1 - s)
        tw = tw_ref[...]
        routed = cur[0:tm, :] * tw[:, 0:1]
        for k in range(1, TOP_K):
            routed = routed + cur[k * tm:(k + 1) * tm, :] * tw[:, k:k + 1]
        hb = hb_ref[...]
        shared = _bdot((_silu(_bdot(hb, ws1_ref[...])) * _bdot(hb, ws3_ref[...])).astype(BF16), ws2_ref[...])
        nm = _rms(routed + shared, npost_ref[...]).reshape(tm // CHUNK, CHUNK, d)
        y = (x1_ref[...].reshape(tm // CHUNK, CHUNK, d) + gf_ref[...] * nm).reshape(tm, d)

        @pl.when(i < npt)
        def _():
            op_ref[...] = y

        @pl.when(i >= npt)
        def _():
            os_ref[...] = y

        @pl.when(i == nt - 1)
        def _():
            gather_all(other, 1 - s).wait()

    @pl.when(i % 2 == 0)
    def _():
        step(0, rows_a, rows_b)

    @pl.when(i % 2 == 1)
    def _():
        step(1, rows_b, rows_a)


def _final(eo, pos, topw, hb, x1, gate_f, npost, ws1, ws3, ws2, tp):
    t, d = x1.shape
    ts = t - tp
    tm = _pick((tp, ts), (128, 64))
    nc = tm // CHUNK
    nt = t // tm
    first, second = _split_rows(tp // tm)
    row = lambda i, *_: (i, 0)
    const = lambda i, *_: (0, 0)
    grid_spec = pltpu.PrefetchScalarGridSpec(
        num_scalar_prefetch=1,
        grid=(nt,),
        in_specs=[pl.BlockSpec(memory_space=pl.ANY),
                  pl.BlockSpec((tm, TOP_K), row),
                  pl.BlockSpec((tm, d), row), pl.BlockSpec((tm, d), row),
                  pl.BlockSpec((nc, 1, d), lambda i, *_: (i, 0, 0)),
                  pl.BlockSpec((1, d), const),
                  pl.BlockSpec(ws1.shape, const), pl.BlockSpec(ws3.shape, const),
                  pl.BlockSpec(ws2.shape, const)],
        out_specs=[pl.BlockSpec((tm, d), first), pl.BlockSpec((tm, d), second)],
        scratch_shapes=[pltpu.VMEM((TOP_K * tm, d), F32), pltpu.VMEM((TOP_K * tm, d), F32),
                        pltpu.SemaphoreType.DMA((2,))],
    )
    return pl.pallas_call(
        functools.partial(_final_kernel, npt=tp // tm),
        grid_spec=grid_spec,
        out_shape=[jax.ShapeDtypeStruct((tp, d), F32), jax.ShapeDtypeStruct((ts, d), F32)],
        compiler_params=_cparams(("arbitrary",)),
        name="final",
    )(pos, eo, topw, hb, x1, gate_f, npost, ws1, ws3, ws2)


def kernel(x_prompt, x_sample, cache_conv, state_ssm, cache_k, cache_v, c_prompt, c_sample,
           w_ada, b_ada, norm_pre_mix, norm_post_mix, norm_pre_ffn, norm_post_ffn,
           w_in, conv_w, conv_b, dt_bias, a_log, d_skip, gn_w, rel_bias, w_out,
           w_router, router_bias, w1, w3, w2, ws1, ws3, ws2):
    assert w_ada.shape[0] == 1, "single layer"
    bp, lp, d = x_prompt.shape
    bs, ls, _ = x_sample.shape
    assert bp == 1 and ls == CHUNK and lp % ATT_PAST == 0
    assert cache_k.shape[2] == ATT_PAST
    heads = a_log.shape[1]
    inner = heads * SSM_HEAD_DIM
    att_w = rel_bias.shape[1] * ATT_HEAD_DIM
    assert att_w == inner
    bcw = 2 * SSM_GROUPS * SSM_STATE
    ne = w_router.shape[2]
    tp, ts = bp * lp, bs * ls
    nseq = bp + bs

    xp, xs = x_prompt.reshape(tp, d), x_sample.reshape(ts, d)
    seq_np = np.concatenate([np.repeat(np.arange(bp), lp // CHUNK), bp + np.arange(bs)]).astype(np.int32)
    first_np = np.concatenate([[1], (seq_np[1:] != seq_np[:-1])]).astype(np.int32)
    seq_of_chunk, first_of_chunk = jnp.asarray(seq_np), jnp.asarray(first_np)

    c_all = jnp.concatenate([c_prompt, c_sample], axis=0)
    c_pad = jnp.pad(c_all, ((0, -nseq % 8), (0, 0)))
    mod = _ada(c_pad, w_ada[0], b_ada[0])[:nseq].reshape(nseq, 6, d)
    mod_c = mod[seq_of_chunk]
    shift_m, scale_m, gate_m, shift_f, scale_f, gate_f = [mod_c[:, i:i + 1, :] for i in range(6)]

    wi = w_in[0]
    o_z, o_x, o_bc = 0, inner, 2 * inner
    o_dt = inner + inner + bcw
    o_q = o_dt + heads
    o_k, o_v = o_q + att_w, o_q + 2 * att_w
    cols = lambda o, n: wi[:, o:o + n]
    w_main = jnp.concatenate([cols(o_z, inner), cols(o_x, inner), cols(o_q, att_w), cols(o_k, att_w),
                              cols(o_v, att_w), cols(o_bc, bcw)], axis=1).astype(BF16)
    w_dt = jnp.pad(cols(o_dt, heads), ((0, 0), (0, LANES - heads))).astype(BF16)
    proj, dt_raw = _inproj(xp, xs, scale_m, shift_m, norm_pre_mix, w_main, w_dt)
    c_x, c_k, c_v, c_bc = inner, 3 * inner, 4 * inner, 5 * inner

    pad_rows = lambda a: jnp.pad(a, ((0, 0), (8 - (CONV_W - 1), 0), (0, 0)))
    pre = jnp.concatenate([jnp.zeros((bp, CONV_W - 1, inner + bcw), F32), cache_conv[0]], axis=0)
    pre_x, pre_bc = pad_rows(pre[:, :, :inner]), pad_rows(pre[:, :, inner:])
    h0 = jnp.concatenate([jnp.zeros((bp,) + state_ssm.shape[2:], F32), state_ssm[0]], axis=0)
    h0t = h0.transpose(0, 3, 1, 2).reshape(nseq, SSM_STATE, inner)
    lane_pad = lambda v: jnp.pad(v, (0, LANES - heads)).reshape(1, LANES)
    expand = (np.arange(LANES)[:, None] == (np.arange(inner)[None, :] // SSM_HEAD_DIM)).astype(np.float32)
    tri = np.tril(np.ones((CHUNK, CHUNK), np.float32))
    consts = (conv_w[0][:, :inner], conv_w[0][:, inner:],
              conv_b[0][:inner].reshape(1, inner), conv_b[0][inner:].reshape(1, bcw),
              lane_pad(dt_bias[0]), lane_pad(-jnp.exp(a_log[0])),
              jnp.repeat(d_skip[0], SSM_HEAD_DIM).reshape(1, inner), gn_w[0].reshape(1, inner),
              jnp.asarray(expand, BF16), jnp.asarray(tri, BF16))
    y_ssd, st_out = _ssd(proj, dt_raw, seq_of_chunk, first_of_chunk, pre_x, pre_bc, h0t, consts, inner)

    bias2 = _attn_bias(rel_bias[0])
    att_p = _attn_prompt(proj, bias2, tp, att_w)
    att_s = _attn_sample(proj, cache_k[0].reshape(bs * ATT_PAST, att_w), cache_v[0].reshape(bs * ATT_PAST, att_w),
                         bias2, tp, bs, att_w)

    wo = w_out[0].astype(BF16)
    x1, hf, hb, logits = _outproj(y_ssd, att_p, att_s, xp, xs, gate_m, scale_f, shift_f,
                                  norm_post_mix, norm_pre_ffn, wo[:inner], wo[inner:], w_router[0])

    topi_t, topw_t = _route(logits.T, router_bias[0])
    a_sorted, blk_e, blk_r0, blk_first, pos = _dispatch_plan(topi_t, MOE_ROWS, ne)
    eo = _moe(hf, a_sorted, blk_e, blk_r0, blk_first, w1[0], w3[0], w2[0])
    y_p, y_s = _final(eo, pos, topw_t.T, hb, x1, gate_f, norm_post_ffn,
                      ws1[0].astype(BF16), ws3[0].astype(BF16), ws2[0].astype(BF16), tp)

    tail = lambda rows: jnp.concatenate([rows[..., c_x:c_x + inner], rows[..., c_bc:c_bc + bcw]], axis=-1)
    conv_prompt = tail(proj[tp - (CONV_W - 1):tp])[None, None]
    srows = lambda c0, n: proj[tp:, c0:c0 + n].reshape(bs, ls, n)[:, ls - (CONV_W - 1):, :]
    conv_sample = jnp.concatenate([srows(c_x, inner), srows(c_bc, bcw)], axis=-1)[None]
    st = st_out.reshape(nseq, SSM_STATE, heads, SSM_HEAD_DIM).transpose(0, 2, 3, 1)
    keep = min(ATT_PAST, lp)
    hd = (rel_bias.shape[1], ATT_HEAD_DIM)
    kv = lambda c0, r0, r1, b, l: proj[r0:r1, c0:c0 + att_w].reshape(b, l, *hd)[None]
    return (y_p.reshape(bp, lp, d), y_s.reshape(bs, ls, d),
            conv_prompt, st[:bp][None], kv(c_k, tp - keep, tp, bp, keep), kv(c_v, tp - keep, tp, bp, keep),
            conv_sample, st[bp:][None], kv(c_k, tp, tp + ts, bs, ls), kv(c_v, tp, tp + ts, bs, ls))
```

```python
import functools

import numpy as np
import jax
import jax.numpy as jnp
from jax import lax
from jax.experimental import pallas as pl
from jax.experimental.pallas import tpu as pltpu

F32 = jnp.float32
BF16 = jnp.bfloat16
I32 = jnp.int32
HIGHEST = lax.Precision.HIGHEST

CHUNK = 64
SSM_HEAD_DIM = 64
SSM_GROUPS = 2
SSM_STATE = 128
CONV_W = 4
ATT_HEAD_DIM = 64
LEFT_CHUNKS = 8
ATT_PAST = LEFT_CHUNKS * CHUNK
BAND = ATT_PAST + CHUNK
REL_CLIP = 128
TOP_K = 8
N_EXPERT_GROUPS = 8
TOPK_GROUPS = 4
ROUTED_SCALE = 2.5
EPS = 1e-6
NEG_BIG = -1e30

LANES = 128
PAIR = 2 * ATT_HEAD_DIM
QPAIR = 2 * CHUNK
KWIN = ATT_PAST + QPAIR
VMEM_LIMIT = 56 * 1024 * 1024
MOE_ROWS = 512
ATT_UNROLL = 8


def _cparams(sem):
    return pltpu.CompilerParams(dimension_semantics=sem, vmem_limit_bytes=VMEM_LIMIT)


def _pick(ns, cands):
    for c in cands:
        if all(n % c == 0 for n in ns):
            return c
    raise ValueError(f"no tile for {ns} in {cands}")


def _silu(x):
    return x * jax.nn.sigmoid(x)


def _rms(x, g):
    ms = jnp.mean(x * x, axis=-1, keepdims=True)
    return x * lax.rsqrt(ms + EPS) * g


def _bdot(a, b):
    return jnp.dot(a, b, preferred_element_type=F32)


def _split3(x):
    p0 = x.astype(BF16)
    r0 = x - p0.astype(F32)
    p1 = r0.astype(BF16)
    p2 = (r0 - p1.astype(F32)).astype(BF16)
    return p0, p1, p2


def _split_rows(npt):
    first = lambda i, *_: (jnp.minimum(i, npt - 1), 0)
    second = lambda i, *_: (jnp.maximum(i - npt, 0), 0)
    return first, second


def _ada_kernel(c_ref, w_ref, b_ref, o_ref):
    a = _silu(c_ref[...])
    o_ref[...] = jnp.dot(a, w_ref[...], precision=HIGHEST, preferred_element_type=F32) + b_ref[...]


def _ada(c_pad, w_ada, b_ada):
    m, d = c_pad.shape
    n = w_ada.shape[1]
    tn = _pick((n,), (1024, 512, 256, 128))
    return pl.pallas_call(
        _ada_kernel,
        grid=(n // tn,),
        in_specs=[pl.BlockSpec((m, d), lambda j: (0, 0)),
                  pl.BlockSpec((d, tn), lambda j: (0, j)),
                  pl.BlockSpec((1, tn), lambda j: (0, j))],
        out_specs=pl.BlockSpec((m, tn), lambda j: (0, j)),
        out_shape=jax.ShapeDtypeStruct((m, n), F32),
        compiler_params=_cparams(("arbitrary",)),
        name="ada",
    )(c_pad, w_ada, b_ada.reshape(1, n))


def _inproj_kernel(xp_ref, xs_ref, sc_ref, sh_ref, g_ref, w_ref, wdt_ref, o_ref, dt_ref, hm_ref, *, npt):
    i = pl.program_id(0)

    def prep(x_ref):
        x = x_ref[...]
        tm, d = x.shape
        y = _rms(x, g_ref[...]).reshape(tm // CHUNK, CHUNK, d)
        h = (y * (1.0 + sc_ref[...]) + sh_ref[...]).reshape(tm, d).astype(BF16)
        hm_ref[...] = h
        dt_ref[...] = _bdot(h, wdt_ref[...])

    @pl.when(pl.program_id(1) == 0)
    def _():
        @pl.when(i < npt)
        def _():
            prep(xp_ref)

        @pl.when(i >= npt)
        def _():
            prep(xs_ref)

    o_ref[...] = _bdot(hm_ref[...], w_ref[...])


def _inproj(xp, xs, scale, shift, g, w_main, w_dt):
    tp, d = xp.shape
    ts = xs.shape[0]
    t = tp + ts
    n = w_main.shape[1]
    tm = _pick((tp, ts), (1024, 512, 256, 128, 64))
    tn = _pick((n,), (512, 256, 128))
    nc = tm // CHUNK
    first, second = _split_rows(tp // tm)
    return pl.pallas_call(
        functools.partial(_inproj_kernel, npt=tp // tm),
        grid=(t // tm, n // tn),
        in_specs=[pl.BlockSpec((tm, d), first),
                  pl.BlockSpec((tm, d), second),
                  pl.BlockSpec((nc, 1, d), lambda i, j: (i, 0, 0)),
                  pl.BlockSpec((nc, 1, d), lambda i, j: (i, 0, 0)),
                  pl.BlockSpec((1, d), lambda i, j: (0, 0)),
                  pl.BlockSpec((d, tn), lambda i, j: (0, j)),
                  pl.BlockSpec((d, LANES), lambda i, j: (0, 0))],
        out_specs=[pl.BlockSpec((tm, tn), lambda i, j: (i, j)),
                   pl.BlockSpec((tm, LANES), lambda i, j: (i, 0))],
        out_shape=[jax.ShapeDtypeStruct((t, n), F32),
                   jax.ShapeDtypeStruct((t, LANES), F32)],
        scratch_shapes=[pltpu.VMEM((tm, d), BF16)],
        compiler_params=_cparams(("arbitrary", "arbitrary")),
        name="inproj",
    )(xp, xs, scale, shift, g, w_main, w_dt)


def _ssd_kernel(seq_ref, first_ref,
                z_ref, xs_ref, bc_ref, dt_ref, prex_ref, prebc_ref, h0_ref,
                cwx_ref, cwbc_ref, cbx_ref, cbbc_ref, dtb_ref, aneg_ref, dsk_ref, gnw_ref,
                e_ref, tri_ref,
                y_ref, st_out_ref,
                xpx_scr, xpbc_scr, st_scr):
    del seq_ref
    c = pl.program_id(0)
    inner = xs_ref.shape[1]
    gw = inner // SSM_GROUPS
    n = SSM_STATE
    pad = 8

    @pl.when(first_ref[c] == 1)
    def _():
        xpx_scr[0:pad, :] = prex_ref[0]
        xpbc_scr[0:pad, :] = prebc_ref[0]
        st_scr[...] = h0_ref[0]

    xpx_scr[pad:pad + CHUNK, :] = xs_ref[...]
    xpbc_scr[pad:pad + CHUNK, :] = bc_ref[...]

    def conv(xp, w_ref, b_ref):
        base = pad - (CONV_W - 1)
        acc = b_ref[...] + xp[base:base + CHUNK, :] * w_ref[0:1, :]
        for k in range(1, CONV_W):
            acc = acc + xp[base + k:base + k + CHUNK, :] * w_ref[k:k + 1, :]
        return _silu(acc)

    xs = conv(xpx_scr, cwx_ref, cbx_ref)
    bc = conv(xpbc_scr, cwbc_ref, cbbc_ref)
    xpx_scr[0:pad, :] = xpx_scr[CHUNK:CHUNK + pad, :]
    xpbc_scr[0:pad, :] = xpbc_scr[CHUNK:CHUNK + pad, :]

    dtv = dt_ref[...] + dtb_ref[...]
    dt = jnp.maximum(dtv, 0.0) + jnp.log(1.0 + jnp.exp(-jnp.abs(dtv)))
    da = dt * aneg_ref[...]
    acs = sum(_bdot(tri_ref[...], p) for p in _split3(da))
    full = sum(_bdot(p, e_ref[...]) for p in _split3(jnp.concatenate([dt, acs], axis=0)))
    dtf = full[0:CHUNK]
    af = full[CHUNK:2 * CHUNK]

    row = lax.broadcasted_iota(I32, (CHUNK, inner), 0)
    lj = lax.broadcasted_iota(I32, (CHUNK, inner), 1) & (SSM_HEAD_DIM - 1)
    aj = jnp.sum(jnp.where(row == lj, af, 0.0), axis=0, keepdims=True)
    lmat = jnp.exp(jnp.where(row >= lj, af - aj, NEG_BIG))
    alast = af[CHUNK - 1:CHUNK, :]
    xdt = xs * dtf
    xw = xdt * jnp.exp(alast - af)
    cdec = jnp.exp(alast)
    eaf = jnp.exp(af)

    lane = lax.broadcasted_iota(I32, (CHUNK, PAIR), 1)
    st = st_scr[...]
    ydiag, yoff, stn = [], [], []
    for g in range(SSM_GROUPS):
        bg = bc[:, g * n:(g + 1) * n].astype(BF16)
        cg = bc[:, (SSM_GROUPS + g) * n:(SSM_GROUPS + g + 1) * n].astype(BF16)
        bb = jnp.concatenate([bg, bg], axis=0)
        cbb = lax.dot_general(cg, bb, (((1,), (1,)), ((), ())), preferred_element_type=F32)
        stg = st[:, g * gw:(g + 1) * gw]
        yoff.append(_bdot(cg, stg.astype(BF16)))
        for p in range(gw // PAIR):
            lo = g * gw + p * PAIR
            m = (cbb * lmat[:, lo:lo + PAIR]).astype(BF16)
            xd = xdt[:, lo:lo + PAIR]
            w = jnp.concatenate([jnp.where(lane < SSM_HEAD_DIM, xd, 0.0),
                                 jnp.where(lane >= SSM_HEAD_DIM, xd, 0.0)], axis=0).astype(BF16)
            ydiag.append(_bdot(m, w))
        upd = lax.dot_general(bg, xw[:, g * gw:(g + 1) * gw].astype(BF16),
                              (((0,), (0,)), ((), ())), preferred_element_type=F32)
        stn.append(cdec[:, g * gw:(g + 1) * gw] * stg + upd)

    y = jnp.concatenate(ydiag, axis=1) + jnp.concatenate(yoff, axis=1) * eaf + dsk_ref[...] * xs
    y = y * _silu(z_ref[...])
    y_ref[...] = _rms(y, gnw_ref[...])
    st_new = jnp.concatenate(stn, axis=1)
    st_scr[...] = st_new
    st_out_ref[0] = st_new


def _ssd(proj, dt_raw, seq_of_chunk, first_of_chunk, pre_x, pre_bc, h0t, consts, inner):
    t = proj.shape[0]
    nch = t // CHUNK
    nseq = h0t.shape[0]
    bcw = pre_bc.shape[-1]
    assert (5 * inner) % bcw == 0
    cmap = lambda blk: (lambda c, s, f: (c, blk))
    smap3 = lambda c, s, f: (s[c], 0, 0)
    const2 = lambda c, s, f: (0, 0)
    grid_spec = pltpu.PrefetchScalarGridSpec(
        num_scalar_prefetch=2,
        grid=(nch,),
        in_specs=[pl.BlockSpec((CHUNK, inner), cmap(0)),
                  pl.BlockSpec((CHUNK, inner), cmap(1)),
                  pl.BlockSpec((CHUNK, bcw), cmap((5 * inner) // bcw)),
                  pl.BlockSpec((CHUNK, LANES), lambda c, s, f: (c, 0)),
                  pl.BlockSpec((1, 8, inner), smap3),
                  pl.BlockSpec((1, 8, bcw), smap3),
                  pl.BlockSpec((1, SSM_STATE, inner), smap3)]
                 + [pl.BlockSpec(a.shape, const2) for a in consts],
        out_specs=[pl.BlockSpec((CHUNK, inner), lambda c, s, f: (c, 0)),
                   pl.BlockSpec((1, SSM_STATE, inner), smap3)],
        scratch_shapes=[pltpu.VMEM((CHUNK + 8, inner), F32),
                        pltpu.VMEM((CHUNK + 8, bcw), F32),
                        pltpu.VMEM((SSM_STATE, inner), F32)],
    )
    return pl.pallas_call(
        _ssd_kernel,
        grid_spec=grid_spec,
        out_shape=[jax.ShapeDtypeStruct((t, inner), F32),
                   jax.ShapeDtypeStruct((nseq, SSM_STATE, inner), F32)],
        compiler_params=_cparams(("arbitrary",)),
        name="ssd",
    )(seq_of_chunk, first_of_chunk, proj, proj, proj, dt_raw, pre_x, pre_bc, h0t, *consts)


def _attn_pairs(q_ref, kwin, vtwin, bias_ref, o_ref, n_steps, n_masked_fn, out_rows):
    n_pairs = q_ref.shape[1] // PAIR
    rowp = lax.broadcasted_iota(I32, (PAIR, QPAIR), 0)
    krow = lax.broadcasted_iota(I32, (KWIN, 2 * QPAIR), 0)

    for jj in range(n_steps):
        n_masked = n_masked_fn(jj)

        def one_pair(hp, jj=jj, n_masked=n_masked):
            lo = hp * PAIR if isinstance(hp, int) else pl.multiple_of(hp * PAIR, PAIR)
            q = q_ref[jj * QPAIR:(jj + 1) * QPAIR, pl.ds(lo, PAIR)] * (ATT_HEAD_DIM ** -0.5)
            qt = q.T
            w = jnp.concatenate([jnp.where(rowp < ATT_HEAD_DIM, qt, 0.0),
                                 jnp.where(rowp >= ATT_HEAD_DIM, qt, 0.0)], axis=1).astype(BF16)
            kb = kwin[jj * QPAIR:jj * QPAIR + KWIN, pl.ds(lo, PAIR)]
            s = _bdot(kb, w) + bias_ref[hp]
            if n_masked is not None and n_masked > 0:
                s = jnp.where(krow < n_masked, NEG_BIG, s)
            mx = jnp.max(s, axis=0, keepdims=True)
            p = jnp.exp(s - mx)
            den = jnp.sum(p, axis=0, keepdims=True)
            vb = vtwin[pl.ds(lo, PAIR), jj * QPAIR:jj * QPAIR + KWIN]
            o2 = _bdot(vb, p.astype(BF16)) / den
            ot = jnp.where(rowp < ATT_HEAD_DIM, o2[:, 0:QPAIR], o2[:, QPAIR:2 * QPAIR])
            o_ref[jj * out_rows:(jj + 1) * out_rows, pl.ds(lo, PAIR)] = ot.T[0:out_rows]

        def body(i, carry, one_pair=one_pair):
            for u in range(ATT_UNROLL):
                one_pair(i * ATT_UNROLL + u)
            return carry

        if n_pairs == ATT_UNROLL:
            body(0, 0)
        else:
            lax.fori_loop(0, n_pairs // ATT_UNROLL, body, 0)


def _attn_prompt_kernel(q_ref, kp_ref, kc_ref, vp_ref, vc_ref, bias_ref, o_ref, kwin, vtwin):
    i = pl.program_id(0)
    tq = q_ref.shape[0]
    kwin[0:ATT_PAST, :] = kp_ref[...].astype(BF16)
    kwin[ATT_PAST:ATT_PAST + tq, :] = kc_ref[...].astype(BF16)
    vtwin[:, 0:ATT_PAST] = vp_ref[...].T.astype(BF16)
    vtwin[:, ATT_PAST:ATT_PAST + tq] = vc_ref[...].T.astype(BF16)
    @pl.when(i == 0)
    def _():
        _attn_pairs(q_ref, kwin, vtwin, bias_ref, o_ref, tq // QPAIR, lambda jj: ATT_PAST - jj * QPAIR, QPAIR)

    @pl.when(i > 0)
    def _():
        _attn_pairs(q_ref, kwin, vtwin, bias_ref, o_ref, tq // QPAIR, lambda jj: None, QPAIR)


def _attn_prompt(proj, bias2, t_prompt, width):
    tq = ATT_PAST
    assert t_prompt % tq == 0
    qb, kb, vb = 2, 3, 4
    prev = lambda i: jnp.maximum(i - 1, 0)
    return pl.pallas_call(
        _attn_prompt_kernel,
        grid=(t_prompt // tq,),
        in_specs=[pl.BlockSpec((tq, width), lambda i: (i, qb)),
                  pl.BlockSpec((tq, width), lambda i: (prev(i), kb)),
                  pl.BlockSpec((tq, width), lambda i: (i, kb)),
                  pl.BlockSpec((tq, width), lambda i: (prev(i), vb)),
                  pl.BlockSpec((tq, width), lambda i: (i, vb)),
                  pl.BlockSpec(bias2.shape, lambda i: (0, 0, 0))],
        out_specs=pl.BlockSpec((tq, width), lambda i: (i, 0)),
        out_shape=jax.ShapeDtypeStruct((t_prompt, width), F32),
        scratch_shapes=[pltpu.VMEM((ATT_PAST + tq, width), BF16),
                        pltpu.VMEM((width, ATT_PAST + tq), BF16)],
        compiler_params=_cparams(("arbitrary",)),
        name="attn_prompt",
    )(proj, proj, proj, proj, proj, bias2)


def _attn_sample_kernel(q_ref, kc_ref, ks_ref, vc_ref, vs_ref, bias_ref, o_ref, qpad, kwin, vtwin):
    width = q_ref.shape[1]
    qpad[0:CHUNK, :] = q_ref[...]
    qpad[CHUNK:QPAIR, :] = jnp.zeros((CHUNK, width), F32)
    kwin[0:ATT_PAST, :] = kc_ref[...].astype(BF16)
    kwin[ATT_PAST:BAND, :] = ks_ref[...].astype(BF16)
    kwin[BAND:KWIN, :] = jnp.zeros((KWIN - BAND, width), BF16)
    vtwin[:, 0:ATT_PAST] = vc_ref[...].T.astype(BF16)
    vtwin[:, ATT_PAST:KWIN] = jnp.concatenate(
        [vs_ref[...], jnp.zeros((KWIN - BAND, width), F32)], axis=0).T.astype(BF16)
    _attn_pairs(qpad, kwin, vtwin, bias_ref, o_ref, 1, lambda jj: None, CHUNK)


def _attn_sample(proj, cache_k, cache_v, bias2, t_prompt, n_seq, width):
    qb, kb, vb = 2, 3, 4
    c0 = t_prompt // CHUNK
    return pl.pallas_call(
        _attn_sample_kernel,
        grid=(n_seq,),
        in_specs=[pl.BlockSpec((CHUNK, width), lambda b: (c0 + b, qb)),
                  pl.BlockSpec((ATT_PAST, width), lambda b: (b, 0)),
                  pl.BlockSpec((CHUNK, width), lambda b: (c0 + b, kb)),
                  pl.BlockSpec((ATT_PAST, width), lambda b: (b, 0)),
                  pl.BlockSpec((CHUNK, width), lambda b: (c0 + b, vb)),
                  pl.BlockSpec(bias2.shape, lambda b: (0, 0, 0))],
        out_specs=pl.BlockSpec((CHUNK, width), lambda b: (b, 0)),
        out_shape=jax.ShapeDtypeStruct((n_seq * CHUNK, width), F32),
        scratch_shapes=[pltpu.VMEM((QPAIR, width), F32),
                        pltpu.VMEM((KWIN, width), BF16),
                        pltpu.VMEM((width, KWIN), BF16)],
        compiler_params=_cparams(("arbitrary",)),
        name="attn_sample",
    )(proj, cache_k, proj, cache_v, proj, bias2)


def _attn_bias(table):
    h = table.shape[0]
    x = np.arange(BAND + CHUNK - 1)
    rel = np.clip(BAND - 1 - x, -REL_CLIP, REL_CLIP) + REL_CLIP
    u = table[:, rel]
    std = jnp.stack([u[:, CHUNK - 1 - i:CHUNK - 1 - i + BAND] for i in range(CHUNK)], axis=1)
    neg = jnp.full((h, CHUNK, KWIN - BAND), NEG_BIG, F32)
    b = jnp.stack([jnp.concatenate([std, neg], axis=2),
                   jnp.concatenate([neg, std], axis=2)],
                  axis=1)
    b = b.reshape(h // 2, 2, 2, CHUNK, KWIN).transpose(0, 4, 1, 2, 3)
    return b.reshape(h // 2, KWIN, 2 * QPAIR)


def _outproj_kernel(y_ref, ap_ref, as_ref, xp_ref, xs_ref, gm_ref, scf_ref, shf_ref, npost_ref, npre_ref,
                    wo1_ref, wo2_ref, wrh_ref, wrl_ref, x1_ref, hf_ref, hb_ref, lg_ref, *, npt):
    i = pl.program_id(0)

    def body(a_ref, x_ref):
        tm, d = x_ref.shape
        mix = _bdot(y_ref[...].astype(BF16), wo1_ref[...]) + _bdot(a_ref[...].astype(BF16), wo2_ref[...])
        nm = _rms(mix, npost_ref[...]).reshape(tm // CHUNK, CHUNK, d)
        x1 = x_ref[...].reshape(tm // CHUNK, CHUNK, d) + gm_ref[...] * nm
        x1_ref[...] = x1.reshape(tm, d)
        hn = _rms(x1, npre_ref[...])
        hf = (hn * (1.0 + scf_ref[...]) + shf_ref[...]).reshape(tm, d)
        hf_ref[...] = hf
        h_hi = hf.astype(BF16)
        hb_ref[...] = h_hi
        h_lo = (hf - h_hi.astype(F32)).astype(BF16)
        lg_ref[...] = _bdot(h_hi, wrh_ref[...]) + _bdot(h_lo, wrh_ref[...]) + _bdot(h_hi, wrl_ref[...])

    @pl.when(i < npt)
    def _():
        body(ap_ref, xp_ref)

    @pl.when(i >= npt)
    def _():
        body(as_ref, xs_ref)


def _outproj(y_ssd, att_p, att_s, xp, xs, gate_m, scale_f, shift_f, npost, npre, wo1, wo2, wr):
    tp, d = xp.shape
    ts = xs.shape[0]
    t = tp + ts
    inner = y_ssd.shape[1]
    ne = wr.shape[1]
    wr_hi = wr.astype(BF16)
    wr_lo = (wr - wr_hi.astype(F32)).astype(BF16)
    tm = _pick((tp, ts), (256, 128, 64))
    nc = tm // CHUNK
    first, second = _split_rows(tp // tm)
    row = lambda i: (i, 0)
    tab = lambda i: (i, 0, 0)
    const = lambda i: (0, 0)
    return pl.pallas_call(
        functools.partial(_outproj_kernel, npt=tp // tm),
        grid=(t // tm,),
        in_specs=[pl.BlockSpec((tm, inner), row),
                  pl.BlockSpec((tm, att_p.shape[1]), first), pl.BlockSpec((tm, att_s.shape[1]), second),
                  pl.BlockSpec((tm, d), first), pl.BlockSpec((tm, d), second),
                  pl.BlockSpec((nc, 1, d), tab), pl.BlockSpec((nc, 1, d), tab), pl.BlockSpec((nc, 1, d), tab),
                  pl.BlockSpec((1, d), const), pl.BlockSpec((1, d), const),
                  pl.BlockSpec(wo1.shape, const), pl.BlockSpec(wo2.shape, const),
                  pl.BlockSpec(wr.shape, const), pl.BlockSpec(wr.shape, const)],
        out_specs=[pl.BlockSpec((tm, d), row), pl.BlockSpec((tm, d), row),
                   pl.BlockSpec((tm, d), row), pl.BlockSpec((tm, ne), row)],
        out_shape=[jax.ShapeDtypeStruct((t, d), F32), jax.ShapeDtypeStruct((t, d), F32),
                   jax.ShapeDtypeStruct((t, d), BF16), jax.ShapeDtypeStruct((t, ne), F32)],
        compiler_params=_cparams(("arbitrary",)),
        name="outproj",
    )(y_ssd, att_p, att_s, xp, xs, gate_m, scale_f, shift_f, npost, npre, wo1, wo2, wr_hi, wr_lo)


def _route_kernel(lg_ref, rb_ref, ti_ref, tw_ref):
    ne, tt = lg_ref.shape
    gs = ne // N_EXPERT_GROUPS
    scores = jax.nn.sigmoid(lg_ref[...])
    sel = scores + rb_ref[...]
    g3 = sel.reshape(N_EXPERT_GROUPS, gs, tt)
    i3 = lax.broadcasted_iota(I32, g3.shape, 1)
    m1 = jnp.max(g3, axis=1, keepdims=True)
    first = jnp.min(jnp.where(g3 == m1, i3, gs), axis=1, keepdims=True)
    m2 = jnp.max(jnp.where(i3 == first, -jnp.inf, g3), axis=1, keepdims=True)
    gscore = (m1 + m2).reshape(N_EXPERT_GROUPS, tt)
    gi = lax.broadcasted_iota(I32, gscore.shape, 0)
    gmask = jnp.zeros(gscore.shape, jnp.bool_)
    rem = gscore
    for _ in range(TOPK_GROUPS):
        mg = jnp.max(rem, axis=0, keepdims=True)
        pick = jnp.min(jnp.where(rem == mg, gi, N_EXPERT_GROUPS), axis=0, keepdims=True)
        hit = gi == pick
        gmask = gmask | hit
        rem = jnp.where(hit, -jnp.inf, rem)
    emask = jnp.broadcast_to(gmask.reshape(N_EXPERT_GROUPS, 1, tt), g3.shape).reshape(ne, tt)
    rem = jnp.where(emask, sel, -jnp.inf)
    ei = lax.broadcasted_iota(I32, (ne, tt), 0)
    idx, wts = [], []
    for _ in range(TOP_K):
        me = jnp.max(rem, axis=0, keepdims=True)
        pick = jnp.min(jnp.where(rem == me, ei, ne), axis=0, keepdims=True)
        hit = ei == pick
        idx.append(pick)
        wts.append(jnp.sum(jnp.where(hit, scores, 0.0), axis=0, keepdims=True))
        rem = jnp.where(hit, -jnp.inf, rem)
    w = jnp.concatenate(wts, axis=0)
    ti_ref[...] = jnp.concatenate(idx, axis=0)
    tw_ref[...] = w / jnp.sum(w, axis=0, keepdims=True) * ROUTED_SCALE


def _route(logits_t, router_bias):
    ne, t = logits_t.shape
    tt = _pick((t,), (2176, 2048, 1024, 512, 256, 128))
    return pl.pallas_call(
        _route_kernel,
        grid=(t // tt,),
        in_specs=[pl.BlockSpec((ne, tt), lambda i: (0, i)),
                  pl.BlockSpec((ne, 1), lambda i: (0, 0))],
        out_specs=[pl.BlockSpec((TOP_K, tt), lambda i: (0, i)),
                   pl.BlockSpec((TOP_K, tt), lambda i: (0, i))],
        out_shape=[jax.ShapeDtypeStruct((TOP_K, t), I32),
                   jax.ShapeDtypeStruct((TOP_K, t), F32)],
        compiler_params=_cparams(("arbitrary",)),
        name="route",
    )(logits_t, router_bias.reshape(ne, 1))


def _moe_kernel(be_ref, r0_ref, first_ref, a_ref,
                hf_hbm, w1_ref, w3_ref, w2_ref, o_ref,
                xbuf_a, xbuf_b, w1b, w3b, w2b, gsem):
    del be_ref
    b = pl.program_id(0)
    nb = pl.num_programs(0)
    tme = xbuf_a.shape[0]

    def gather_all(xb, s):
        return pltpu.make_async_copy(hf_hbm.at[pl.ds(0, tme)], xb, gsem.at[s])

    def start_gather(blk, xb, s):
        r0 = r0_ref[blk]
        for r in range(tme):
            tok = a_ref[r0 + r] >> 3
            pltpu.make_async_copy(hf_hbm.at[pl.ds(tok, 1)], xb.at[pl.ds(r, 1)], gsem.at[s]).start()

    @pl.when(b == 0)
    def _():
        start_gather(0, xbuf_a, 0)

    @pl.when(first_ref[b] == 1)
    def _():
        w1b[...] = w1_ref[0].astype(BF16)
        w3b[...] = w3_ref[0].astype(BF16)
        w2b[...] = w2_ref[0].astype(BF16)

    nxt = jnp.minimum(b + 1, nb - 1)

    def step(s, xb_cur, xb_nxt):
        gather_all(xb_cur, s).wait()
        start_gather(nxt, xb_nxt, 1 - s)
        x = xb_cur[...].astype(BF16)
        h = (_silu(_bdot(x, w1b[...])) * _bdot(x, w3b[...])).astype(BF16)
        o_ref[...] = _bdot(h, w2b[...])

        @pl.when(b == nb - 1)
        def _():
            gather_all(xb_nxt, 1 - s).wait()

    @pl.when(b % 2 == 0)
    def _():
        step(0, xbuf_a, xbuf_b)

    @pl.when(b % 2 == 1)
    def _():
        step(1, xbuf_b, xbuf_a)


def _moe(hf, a_sorted, blk_e, blk_r0, blk_first, w1, w3, w2):
    ne, d, de = w1.shape
    nb = blk_e.shape[0]
    tme = MOE_ROWS
    wmap = lambda b, be, r0, f, a: (be[b], 0, 0)
    grid_spec = pltpu.PrefetchScalarGridSpec(
        num_scalar_prefetch=4,
        grid=(nb,),
        in_specs=[pl.BlockSpec(memory_space=pl.ANY),
                  pl.BlockSpec((1, d, de), wmap),
                  pl.BlockSpec((1, d, de), wmap),
                  pl.BlockSpec((1, de, d), wmap)],
        out_specs=pl.BlockSpec((tme, d), lambda b, be, r0, f, a: (b, 0)),
        scratch_shapes=[pltpu.VMEM((tme, d), F32), pltpu.VMEM((tme, d), F32),
                        pltpu.VMEM((d, de), BF16), pltpu.VMEM((d, de), BF16), pltpu.VMEM((de, d), BF16),
                        pltpu.SemaphoreType.DMA((2,))],
    )
    return pl.pallas_call(
        _moe_kernel,
        grid_spec=grid_spec,
        out_shape=jax.ShapeDtypeStruct((nb * tme, d), F32),
        compiler_params=_cparams(("arbitrary",)),
        name="moe",
    )(blk_e, blk_r0, blk_first, a_sorted, hf, w1, w3, w2)


def _dispatch_plan(topi_t, tme, ne):
    k, t = topi_t.shape
    assert k == TOP_K
    a_cnt = k * t
    shift = int(np.ceil(np.log2(a_cnt)))
    assert ne << shift < 2 ** 31
    a_id = jnp.arange(t, dtype=I32)[None, :] * k + jnp.arange(k, dtype=I32)[:, None]
    keys = (topi_t << shift) + a_id
    a_sorted = jnp.sort(keys.reshape(-1)) & ((1 << shift) - 1)
    a_sorted = jnp.concatenate([a_sorted, jnp.zeros((tme,), I32)])
    is_e = topi_t[:, :, None] == jnp.arange(ne, dtype=I32)[None, None, :]
    chose = jnp.sum(is_e, axis=0, dtype=I32)
    rank = jnp.cumsum(chose, axis=0) - chose
    counts = jnp.sum(chose, axis=0)
    starts = jnp.cumsum(counts) - counts
    nblk = (counts + tme - 1) // tme
    blk_end = jnp.cumsum(nblk)
    nb = a_cnt // tme + ne
    b = jnp.arange(nb, dtype=I32)
    valid = b < blk_end[-1]
    e_raw = jnp.minimum(jnp.sum(blk_end[None, :] <= b[:, None], axis=1, dtype=I32), ne - 1)
    onehot = (e_raw[:, None] == jnp.arange(ne, dtype=I32)[None, :]).astype(I32)
    pick = lambda v: jnp.sum(onehot * v[None, :], axis=1)
    j = b - pick(blk_end - nblk)
    blk_r0 = jnp.where(valid, pick(starts) + j * tme, 0)
    blk_first = (valid & (j == 0)).astype(I32)
    e_last = jnp.max(jnp.where(valid, e_raw, 0))
    blk_e = jnp.where(valid, e_raw, e_last)
    out_row = ((blk_end - nblk) * tme)[None, :] + rank
    pos = jnp.sum(jnp.where(is_e, out_row[None], 0), axis=2).T.reshape(-1)
    return a_sorted, blk_e, blk_r0, blk_first, pos


def _shared_kernel(hb_ref, ws1_ref, ws3_ref, ws2_ref, o_ref):
    hb = hb_ref[...]
    h = (_silu(_bdot(hb, ws1_ref[...])) * _bdot(hb, ws3_ref[...])).astype(BF16)
    o_ref[...] = _bdot(h, ws2_ref[...])


def _shared(hb, ws1, ws3, ws2):
    t, d = hb.shape
    tm = _pick((t,), (1024, 512, 256, 128))
    const = lambda i: (0, 0)
    return pl.pallas_call(
        _shared_kernel,
        grid=(t // tm,),
        in_specs=[pl.BlockSpec((tm, d), lambda i: (i, 0)),
                  pl.BlockSpec(ws1.shape, const), pl.BlockSpec(ws3.shape, const), pl.BlockSpec(ws2.shape, const)],
        out_specs=pl.BlockSpec((tm, d), lambda i: (i, 0)),
        out_shape=jax.ShapeDtypeStruct((t, d), F32),
        compiler_params=_cparams(("arbitrary",)),
        name="shared",
    )(hb, ws1, ws3, ws2)


def _final_kernel(pos_ref, eo_hbm, tw_ref, sh_ref, x1_ref, gf_ref, npost_ref,
                  op_ref, os_ref, rows_a, rows_b, sem, *, npt):
    i = pl.program_id(0)
    nt = pl.num_programs(0)
    tm, d = x1_ref.shape

    def gather_all(buf, s):
        return pltpu.make_async_copy(eo_hbm.at[pl.ds(0, TOP_K * tm)], buf, sem.at[s])

    def start_gather(tile, buf, s):
        base = tile * (TOP_K * tm)
        for t in range(tm):
            for k in range(TOP_K):
                src = eo_hbm.at[pl.ds(pos_ref[base + t * TOP_K + k], 1)]
                pltpu.make_async_copy(src, buf.at[pl.ds(k * tm + t, 1)], sem.at[s]).start()

    @pl.when(i == 0)
    def _():
        start_gather(0, rows_a, 0)

    nxt = jnp.minimum(i + 1, nt - 1)

    def step(s, cur, other):
        gather_all(cur, s).wait()
        start_gather(nxt, other, 1 - s)
        tw = tw_ref[...]
        routed = cur[0:tm, :] * tw[:, 0:1]
        for k in range(1, TOP_K):
            routed = routed + cur[k * tm:(k + 1) * tm, :] * tw[:, k:k + 1]
        nm = _rms(routed + sh_ref[...], npost_ref[...]).reshape(tm // CHUNK, CHUNK, d)
        y = (x1_ref[...].reshape(tm // CHUNK, CHUNK, d) + gf_ref[...] * nm).reshape(tm, d)

        @pl.when(i < npt)
        def _():
            op_ref[...] = y

        @pl.when(i >= npt)
        def _():
            os_ref[...] = y

        @pl.when(i == nt - 1)
        def _():
            gather_all(other, 1 - s).wait()

    @pl.when(i % 2 == 0)
    def _():
        step(0, rows_a, rows_b)

    @pl.when(i % 2 == 1)
    def _():
        step(1, rows_b, rows_a)


def _final(eo, pos, topw, shared, x1, gate_f, npost, tp):
    t, d = x1.shape
    ts = t - tp
    tm = _pick((tp, ts), (128, 64))
    nc = tm // CHUNK
    nt = t // tm
    first, second = _split_rows(tp // tm)
    row = lambda i, *_: (i, 0)
    const = lambda i, *_: (0, 0)
    grid_spec = pltpu.PrefetchScalarGridSpec(
        num_scalar_prefetch=1,
        grid=(nt,),
        in_specs=[pl.BlockSpec(memory_space=pl.ANY),
                  pl.BlockSpec((tm, TOP_K), row),
                  pl.BlockSpec((tm, d), row), pl.BlockSpec((tm, d), row),
                  pl.BlockSpec((nc, 1, d), lambda i, *_: (i, 0, 0)),
                  pl.BlockSpec((1, d), const)],
        out_specs=[pl.BlockSpec((tm, d), first), pl.BlockSpec((tm, d), second)],
        scratch_shapes=[pltpu.VMEM((TOP_K * tm, d), F32), pltpu.VMEM((TOP_K * tm, d), F32),
                        pltpu.SemaphoreType.DMA((2,))],
    )
    return pl.pallas_call(
        functools.partial(_final_kernel, npt=tp // tm),
        grid_spec=grid_spec,
        out_shape=[jax.ShapeDtypeStruct((tp, d), F32), jax.ShapeDtypeStruct((ts, d), F32)],
        compiler_params=_cparams(("arbitrary",)),
        name="final",
    )(pos, eo, topw, shared, x1, gate_f, npost)


def kernel(x_prompt, x_sample, cache_conv, state_ssm, cache_k, cache_v, c_prompt, c_sample,
           w_ada, b_ada, norm_pre_mix, norm_post_mix, norm_pre_ffn, norm_post_ffn,
           w_in, conv_w, conv_b, dt_bias, a_log, d_skip, gn_w, rel_bias, w_out,
           w_router, router_bias, w1, w3, w2, ws1, ws3, ws2):
    assert w_ada.shape[0] == 1, "single layer"
    bp, lp, d = x_prompt.shape
    bs, ls, _ = x_sample.shape
    assert bp == 1 and ls == CHUNK and lp % ATT_PAST == 0
    assert cache_k.shape[2] == ATT_PAST
    heads = a_log.shape[1]
    inner = heads * SSM_HEAD_DIM
    att_w = rel_bias.shape[1] * ATT_HEAD_DIM
    assert att_w == inner
    bcw = 2 * SSM_GROUPS * SSM_STATE
    ne = w_router.shape[2]
    tp, ts = bp * lp, bs * ls
    nseq = bp + bs

    xp, xs = x_prompt.reshape(tp, d), x_sample.reshape(ts, d)
    seq_np = np.concatenate([np.repeat(np.arange(bp), lp // CHUNK), bp + np.arange(bs)]).astype(np.int32)
    first_np = np.concatenate([[1], (seq_np[1:] != seq_np[:-1])]).astype(np.int32)
    seq_of_chunk, first_of_chunk = jnp.asarray(seq_np), jnp.asarray(first_np)

    c_all = jnp.concatenate([c_prompt, c_sample], axis=0)
    c_pad = jnp.pad(c_all, ((0, -nseq % 8), (0, 0)))
    mod = _ada(c_pad, w_ada[0], b_ada[0])[:nseq].reshape(nseq, 6, d)
    mod_c = mod[seq_of_chunk]
    shift_m, scale_m, gate_m, shift_f, scale_f, gate_f = [mod_c[:, i:i + 1, :] for i in range(6)]

    wi = w_in[0]
    o_z, o_x, o_bc = 0, inner, 2 * inner
    o_dt = inner + inner + bcw
    o_q = o_dt + heads
    o_k, o_v = o_q + att_w, o_q + 2 * att_w
    cols = lambda o, n: wi[:, o:o + n]
    w_main = jnp.concatenate([cols(o_z, inner), cols(o_x, inner), cols(o_q, att_w), cols(o_k, att_w),
                              cols(o_v, att_w), cols(o_bc, bcw)], axis=1).astype(BF16)
    w_dt = jnp.pad(cols(o_dt, heads), ((0, 0), (0, LANES - heads))).astype(BF16)
    proj, dt_raw = _inproj(xp, xs, scale_m, shift_m, norm_pre_mix, w_main, w_dt)
    c_x, c_k, c_v, c_bc = inner, 3 * inner, 4 * inner, 5 * inner

    pad_rows = lambda a: jnp.pad(a, ((0, 0), (8 - (CONV_W - 1), 0), (0, 0)))
    pre = jnp.concatenate([jnp.zeros((bp, CONV_W - 1, inner + bcw), F32), cache_conv[0]], axis=0)
    pre_x, pre_bc = pad_rows(pre[:, :, :inner]), pad_rows(pre[:, :, inner:])
    h0 = jnp.concatenate([jnp.zeros((bp,) + state_ssm.shape[2:], F32), state_ssm[0]], axis=0)
    h0t = h0.transpose(0, 3, 1, 2).reshape(nseq, SSM_STATE, inner)
    lane_pad = lambda v: jnp.pad(v, (0, LANES - heads)).reshape(1, LANES)
    expand = (np.arange(LANES)[:, None] == (np.arange(inner)[None, :] // SSM_HEAD_DIM)).astype(np.float32)
    tri = np.tril(np.ones((CHUNK, CHUNK), np.float32))
    consts = (conv_w[0][:, :inner], conv_w[0][:, inner:],
              conv_b[0][:inner].reshape(1, inner), conv_b[0][inner:].reshape(1, bcw),
              lane_pad(dt_bias[0]), lane_pad(-jnp.exp(a_log[0])),
              jnp.repeat(d_skip[0], SSM_HEAD_DIM).reshape(1, inner), gn_w[0].reshape(1, inner),
              jnp.asarray(expand, BF16), jnp.asarray(tri, BF16))
    y_ssd, st_out = _ssd(proj, dt_raw, seq_of_chunk, first_of_chunk, pre_x, pre_bc, h0t, consts, inner)

    bias2 = _attn_bias(rel_bias[0])
    att_p = _attn_prompt(proj, bias2, tp, att_w)
    att_s = _attn_sample(proj, cache_k[0].reshape(bs * ATT_PAST, att_w), cache_v[0].reshape(bs * ATT_PAST, att_w),
                         bias2, tp, bs, att_w)

    wo = w_out[0].astype(BF16)
    x1, hf, hb, logits = _outproj(y_ssd, att_p, att_s, xp, xs, gate_m, scale_f, shift_f,
                                  norm_post_mix, norm_pre_ffn, wo[:inner], wo[inner:], w_router[0])

    topi_t, topw_t = _route(logits.T, router_bias[0])
    a_sorted, blk_e, blk_r0, blk_first, pos = _dispatch_plan(topi_t, MOE_ROWS, ne)
    eo = _moe(hf, a_sorted, blk_e, blk_r0, blk_first, w1[0], w3[0], w2[0])
    shared = _shared(hb, ws1[0].astype(BF16), ws3[0].astype(BF16), ws2[0].astype(BF16))
    y_p, y_s = _final(eo, pos, topw_t.T, shared, x1, gate_f, norm_post_ffn, tp)

    tail = lambda rows: jnp.concatenate([rows[..., c_x:c_x + inner], rows[..., c_bc:c_bc + bcw]], axis=-1)
    conv_prompt = tail(proj[tp - (CONV_W - 1):tp])[None, None]
    srows = lambda c0, n: proj[tp:, c0:c0 + n].reshape(bs, ls, n)[:, ls - (CONV_W - 1):, :]
    conv_sample = jnp.concatenate([srows(c_x, inner), srows(c_bc, bcw)], axis=-1)[None]
    st = st_out.reshape(nseq, SSM_STATE, heads, SSM_HEAD_DIM).transpose(0, 2, 3, 1)
    keep = min(ATT_PAST, lp)
    hd = (rel_bias.shape[1], ATT_HEAD_DIM)
    kv = lambda c0, r0, r1, b, l: proj[r0:r1, c0:c0 + att_w].reshape(b, l, *hd)[None]
    return (y_p.reshape(bp, lp, d), y_s.reshape(bs, ls, d),
            conv_prompt, st[:bp][None], kv(c_k, tp - keep, tp, bp, keep), kv(c_v, tp - keep, tp, bp, keep),
            conv_sample, st[bp:][None], kv(c_k, tp, tp + ts, bs, ls), kv(c_v, tp, tp + ts, bs, ls))
```

```python
import functools

import numpy as np
import jax
import jax.numpy as jnp
from jax import lax
from jax.experimental import pallas as pl
from jax.experimental.pallas import tpu as pltpu

F32 = jnp.float32
BF16 = jnp.bfloat16
I32 = jnp.int32
HIGHEST = lax.Precision.HIGHEST

CHUNK = 64
SSM_HEAD_DIM = 64
SSM_GROUPS = 2
SSM_STATE = 128
CONV_W = 4
ATT_HEAD_DIM = 64
LEFT_CHUNKS = 8
ATT_PAST = LEFT_CHUNKS * CHUNK
BAND = ATT_PAST + CHUNK
REL_CLIP = 128
TOP_K = 8
N_EXPERT_GROUPS = 8
TOPK_GROUPS = 4
ROUTED_SCALE = 2.5
EPS = 1e-6
NEG_BIG = -1e30

LANES = 128
PAIR = 2 * ATT_HEAD_DIM
QPAIR = 2 * CHUNK
KWIN = ATT_PAST + QPAIR
VMEM_LIMIT = 56 * 1024 * 1024
MOE_ROWS = 512
ATT_UNROLL = 8


def _cparams(sem):
    return pltpu.CompilerParams(dimension_semantics=sem, vmem_limit_bytes=VMEM_LIMIT)


def _pick(ns, cands):
    for c in cands:
        if all(n % c == 0 for n in ns):
            return c
    raise ValueError(f"no tile for {ns} in {cands}")


def _silu(x):
    return x * jax.nn.sigmoid(x)


def _rms(x, g):
    ms = jnp.mean(x * x, axis=-1, keepdims=True)
    return x * lax.rsqrt(ms + EPS) * g


def _bdot(a, b):
    return jnp.dot(a, b, preferred_element_type=F32)


def _split3(x):
    p0 = x.astype(BF16)
    r0 = x - p0.astype(F32)
    p1 = r0.astype(BF16)
    p2 = (r0 - p1.astype(F32)).astype(BF16)
    return p0, p1, p2


def _split_rows(npt):
    first = lambda i, *_: (jnp.minimum(i, npt - 1), 0)
    second = lambda i, *_: (jnp.maximum(i - npt, 0), 0)
    return first, second


def _ada_kernel(c_ref, w_ref, b_ref, o_ref):
    a = _silu(c_ref[...])
    o_ref[...] = jnp.dot(a, w_ref[...], precision=HIGHEST, preferred_element_type=F32) + b_ref[...]


def _ada(c_pad, w_ada, b_ada):
    m, d = c_pad.shape
    n = w_ada.shape[1]
    tn = _pick((n,), (1024, 512, 256, 128))
    return pl.pallas_call(
        _ada_kernel,
        grid=(n // tn,),
        in_specs=[pl.BlockSpec((m, d), lambda j: (0, 0)),
                  pl.BlockSpec((d, tn), lambda j: (0, j)),
                  pl.BlockSpec((1, tn), lambda j: (0, j))],
        out_specs=pl.BlockSpec((m, tn), lambda j: (0, j)),
        out_shape=jax.ShapeDtypeStruct((m, n), F32),
        compiler_params=_cparams(("arbitrary",)),
        name="ada",
    )(c_pad, w_ada, b_ada.reshape(1, n))


def _inproj_kernel(xp_ref, xs_ref, sc_ref, sh_ref, g_ref, w_ref, wdt_ref, o_ref, dt_ref, hm_ref, *, npt):
    i = pl.program_id(0)

    def prep(x_ref):
        x = x_ref[...]
        tm, d = x.shape
        y = _rms(x, g_ref[...]).reshape(tm // CHUNK, CHUNK, d)
        h = (y * (1.0 + sc_ref[...]) + sh_ref[...]).reshape(tm, d).astype(BF16)
        hm_ref[...] = h
        dt_ref[...] = _bdot(h, wdt_ref[...])

    @pl.when(pl.program_id(1) == 0)
    def _():
        @pl.when(i < npt)
        def _():
            prep(xp_ref)

        @pl.when(i >= npt)
        def _():
            prep(xs_ref)

    o_ref[...] = _bdot(hm_ref[...], w_ref[...])


def _inproj(xp, xs, scale, shift, g, w_main, w_dt):
    tp, d = xp.shape
    ts = xs.shape[0]
    t = tp + ts
    n = w_main.shape[1]
    tm = _pick((tp, ts), (1024, 512, 256, 128, 64))
    tn = _pick((n,), (512, 256, 128))
    nc = tm // CHUNK
    first, second = _split_rows(tp // tm)
    return pl.pallas_call(
        functools.partial(_inproj_kernel, npt=tp // tm),
        grid=(t // tm, n // tn),
        in_specs=[pl.BlockSpec((tm, d), first),
                  pl.BlockSpec((tm, d), second),
                  pl.BlockSpec((nc, 1, d), lambda i, j: (i, 0, 0)),
                  pl.BlockSpec((nc, 1, d), lambda i, j: (i, 0, 0)),
                  pl.BlockSpec((1, d), lambda i, j: (0, 0)),
                  pl.BlockSpec((d, tn), lambda i, j: (0, j)),
                  pl.BlockSpec((d, LANES), lambda i, j: (0, 0))],
        out_specs=[pl.BlockSpec((tm, tn), lambda i, j: (i, j)),
                   pl.BlockSpec((tm, LANES), lambda i, j: (i, 0))],
        out_shape=[jax.ShapeDtypeStruct((t, n), F32),
                   jax.ShapeDtypeStruct((t, LANES), F32)],
        scratch_shapes=[pltpu.VMEM((tm, d), BF16)],
        compiler_params=_cparams(("arbitrary", "arbitrary")),
        name="inproj",
    )(xp, xs, scale, shift, g, w_main, w_dt)


def _ssd_kernel(seq_ref, first_ref,
                z_ref, xs_ref, bc_ref, dt_ref, prex_ref, prebc_ref, h0_ref,
                cwx_ref, cwbc_ref, cbx_ref, cbbc_ref, dtb_ref, aneg_ref, dsk_ref, gnw_ref,
                e_ref, tri_ref,
                y_ref, st_out_ref,
                xpx_scr, xpbc_scr, st_scr):
    del seq_ref
    c = pl.program_id(0)
    inner = xs_ref.shape[1]
    gw = inner // SSM_GROUPS
    n = SSM_STATE
    pad = 8

    @pl.when(first_ref[c] == 1)
    def _():
        xpx_scr[0:pad, :] = prex_ref[0]
        xpbc_scr[0:pad, :] = prebc_ref[0]
        st_scr[...] = h0_ref[0]

    xpx_scr[pad:pad + CHUNK, :] = xs_ref[...]
    xpbc_scr[pad:pad + CHUNK, :] = bc_ref[...]

    def conv(xp, w_ref, b_ref):
        base = pad - (CONV_W - 1)
        acc = b_ref[...] + xp[base:base + CHUNK, :] * w_ref[0:1, :]
        for k in range(1, CONV_W):
            acc = acc + xp[base + k:base + k + CHUNK, :] * w_ref[k:k + 1, :]
        return _silu(acc)

    xs = conv(xpx_scr, cwx_ref, cbx_ref)
    bc = conv(xpbc_scr, cwbc_ref, cbbc_ref)
    xpx_scr[0:pad, :] = xpx_scr[CHUNK:CHUNK + pad, :]
    xpbc_scr[0:pad, :] = xpbc_scr[CHUNK:CHUNK + pad, :]

    dtv = dt_ref[...] + dtb_ref[...]
    dt = jnp.maximum(dtv, 0.0) + jnp.log(1.0 + jnp.exp(-jnp.abs(dtv)))
    da = dt * aneg_ref[...]
    acs = sum(_bdot(tri_ref[...], p) for p in _split3(da))
    full = sum(_bdot(p, e_ref[...]) for p in _split3(jnp.concatenate([dt, acs], axis=0)))
    dtf = full[0:CHUNK]
    af = full[CHUNK:2 * CHUNK]

    row = lax.broadcasted_iota(I32, (CHUNK, inner), 0)
    lj = lax.broadcasted_iota(I32, (CHUNK, inner), 1) & (SSM_HEAD_DIM - 1)
    aj = jnp.sum(jnp.where(row == lj, af, 0.0), axis=0, keepdims=True)
    lmat = jnp.exp(jnp.where(row >= lj, af - aj, NEG_BIG))
    alast = af[CHUNK - 1:CHUNK, :]
    xdt = xs * dtf
    xw = xdt * jnp.exp(alast - af)
    cdec = jnp.exp(alast)
    eaf = jnp.exp(af)

    lane = lax.broadcasted_iota(I32, (CHUNK, PAIR), 1)
    st = st_scr[...]
    ydiag, yoff, stn = [], [], []
    for g in range(SSM_GROUPS):
        bg = bc[:, g * n:(g + 1) * n].astype(BF16)
        cg = bc[:, (SSM_GROUPS + g) * n:(SSM_GROUPS + g + 1) * n].astype(BF16)
        bb = jnp.concatenate([bg, bg], axis=0)
        cbb = lax.dot_general(cg, bb, (((1,), (1,)), ((), ())), preferred_element_type=F32)
        stg = st[:, g * gw:(g + 1) * gw]
        yoff.append(_bdot(cg, stg.astype(BF16)))
        for p in range(gw // PAIR):
            lo = g * gw + p * PAIR
            m = (cbb * lmat[:, lo:lo + PAIR]).astype(BF16)
            xd = xdt[:, lo:lo + PAIR]
            w = jnp.concatenate([jnp.where(lane < SSM_HEAD_DIM, xd, 0.0),
                                 jnp.where(lane >= SSM_HEAD_DIM, xd, 0.0)], axis=0).astype(BF16)
            ydiag.append(_bdot(m, w))
        upd = lax.dot_general(bg, xw[:, g * gw:(g + 1) * gw].astype(BF16),
                              (((0,), (0,)), ((), ())), preferred_element_type=F32)
        stn.append(cdec[:, g * gw:(g + 1) * gw] * stg + upd)

    y = jnp.concatenate(ydiag, axis=1) + jnp.concatenate(yoff, axis=1) * eaf + dsk_ref[...] * xs
    y = y * _silu(z_ref[...])
    y_ref[...] = _rms(y, gnw_ref[...])
    st_new = jnp.concatenate(stn, axis=1)
    st_scr[...] = st_new
    st_out_ref[0] = st_new


def _ssd(proj, dt_raw, seq_of_chunk, first_of_chunk, pre_x, pre_bc, h0t, consts, inner):
    t = proj.shape[0]
    nch = t // CHUNK
    nseq = h0t.shape[0]
    bcw = pre_bc.shape[-1]
    assert (5 * inner) % bcw == 0
    cmap = lambda blk: (lambda c, s, f: (c, blk))
    smap3 = lambda c, s, f: (s[c], 0, 0)
    const2 = lambda c, s, f: (0, 0)
    grid_spec = pltpu.PrefetchScalarGridSpec(
        num_scalar_prefetch=2,
        grid=(nch,),
        in_specs=[pl.BlockSpec((CHUNK, inner), cmap(0)),
                  pl.BlockSpec((CHUNK, inner), cmap(1)),
                  pl.BlockSpec((CHUNK, bcw), cmap((5 * inner) // bcw)),
                  pl.BlockSpec((CHUNK, LANES), lambda c, s, f: (c, 0)),
                  pl.BlockSpec((1, 8, inner), smap3),
                  pl.BlockSpec((1, 8, bcw), smap3),
                  pl.BlockSpec((1, SSM_STATE, inner), smap3)]
                 + [pl.BlockSpec(a.shape, const2) for a in consts],
        out_specs=[pl.BlockSpec((CHUNK, inner), lambda c, s, f: (c, 0)),
                   pl.BlockSpec((1, SSM_STATE, inner), smap3)],
        scratch_shapes=[pltpu.VMEM((CHUNK + 8, inner), F32),
                        pltpu.VMEM((CHUNK + 8, bcw), F32),
                        pltpu.VMEM((SSM_STATE, inner), F32)],
    )
    return pl.pallas_call(
        _ssd_kernel,
        grid_spec=grid_spec,
        out_shape=[jax.ShapeDtypeStruct((t, inner), F32),
                   jax.ShapeDtypeStruct((nseq, SSM_STATE, inner), F32)],
        compiler_params=_cparams(("arbitrary",)),
        name="ssd",
    )(seq_of_chunk, first_of_chunk, proj, proj, proj, dt_raw, pre_x, pre_bc, h0t, *consts)


def _attn_pairs(q_ref, kwin, vtwin, bias_ref, o_ref, n_steps, n_masked_fn, out_rows):
    n_pairs = q_ref.shape[1] // PAIR
    rowp = lax.broadcasted_iota(I32, (PAIR, QPAIR), 0)
    krow = lax.broadcasted_iota(I32, (KWIN, 2 * QPAIR), 0)

    for jj in range(n_steps):
        n_masked = n_masked_fn(jj)

        def one_pair(hp, jj=jj, n_masked=n_masked):
            lo = hp * PAIR if isinstance(hp, int) else pl.multiple_of(hp * PAIR, PAIR)
            q = q_ref[jj * QPAIR:(jj + 1) * QPAIR, pl.ds(lo, PAIR)] * (ATT_HEAD_DIM ** -0.5)
            qt = q.T
            w = jnp.concatenate([jnp.where(rowp < ATT_HEAD_DIM, qt, 0.0),
                                 jnp.where(rowp >= ATT_HEAD_DIM, qt, 0.0)], axis=1).astype(BF16)
            kb = kwin[jj * QPAIR:jj * QPAIR + KWIN, pl.ds(lo, PAIR)]
            s = _bdot(kb, w) + bias_ref[hp]
            if n_masked is not None and n_masked > 0:
                s = jnp.where(krow < n_masked, NEG_BIG, s)
            mx = jnp.max(s, axis=0, keepdims=True)
            p = jnp.exp(s - mx)
            den = jnp.sum(p, axis=0, keepdims=True)
            vb = vtwin[pl.ds(lo, PAIR), jj * QPAIR:jj * QPAIR + KWIN]
            o2 = _bdot(vb, p.astype(BF16)) / den
            ot = jnp.where(rowp < ATT_HEAD_DIM, o2[:, 0:QPAIR], o2[:, QPAIR:2 * QPAIR])
            o_ref[jj * out_rows:(jj + 1) * out_rows, pl.ds(lo, PAIR)] = ot.T[0:out_rows]

        def body(i, carry, one_pair=one_pair):
            for u in range(ATT_UNROLL):
                one_pair(i * ATT_UNROLL + u)
            return carry

        if n_pairs == ATT_UNROLL:
            body(0, 0)
        else:
            lax.fori_loop(0, n_pairs // ATT_UNROLL, body, 0)


def _attn_prompt_kernel(q_ref, kp_ref, kc_ref, vp_ref, vc_ref, bias_ref, o_ref, kwin, vtwin):
    i = pl.program_id(0)
    tq = q_ref.shape[0]
    kwin[0:ATT_PAST, :] = kp_ref[...].astype(BF16)
    kwin[ATT_PAST:ATT_PAST + tq, :] = kc_ref[...].astype(BF16)
    vtwin[:, 0:ATT_PAST] = vp_ref[...].T.astype(BF16)
    vtwin[:, ATT_PAST:ATT_PAST + tq] = vc_ref[...].T.astype(BF16)
    @pl.when(i == 0)
    def _():
        _attn_pairs(q_ref, kwin, vtwin, bias_ref, o_ref, tq // QPAIR, lambda jj: ATT_PAST - jj * QPAIR, QPAIR)

    @pl.when(i > 0)
    def _():
        _attn_pairs(q_ref, kwin, vtwin, bias_ref, o_ref, tq // QPAIR, lambda jj: None, QPAIR)


def _attn_prompt(proj, bias2, t_prompt, width):
    tq = ATT_PAST
    assert t_prompt % tq == 0
    qb, kb, vb = 2, 3, 4
    prev = lambda i: jnp.maximum(i - 1, 0)
    return pl.pallas_call(
        _attn_prompt_kernel,
        grid=(t_prompt // tq,),
        in_specs=[pl.BlockSpec((tq, width), lambda i: (i, qb)),
                  pl.BlockSpec((tq, width), lambda i: (prev(i), kb)),
                  pl.BlockSpec((tq, width), lambda i: (i, kb)),
                  pl.BlockSpec((tq, width), lambda i: (prev(i), vb)),
                  pl.BlockSpec((tq, width), lambda i: (i, vb)),
                  pl.BlockSpec(bias2.shape, lambda i: (0, 0, 0))],
        out_specs=pl.BlockSpec((tq, width), lambda i: (i, 0)),
        out_shape=jax.ShapeDtypeStruct((t_prompt, width), F32),
        scratch_shapes=[pltpu.VMEM((ATT_PAST + tq, width), BF16),
                        pltpu.VMEM((width, ATT_PAST + tq), BF16)],
        compiler_params=_cparams(("arbitrary",)),
        name="attn_prompt",
    )(proj, proj, proj, proj, proj, bias2)


def _attn_sample_kernel(q_ref, kc_ref, ks_ref, vc_ref, vs_ref, bias_ref, o_ref, qpad, kwin, vtwin):
    width = q_ref.shape[1]
    qpad[0:CHUNK, :] = q_ref[...]
    qpad[CHUNK:QPAIR, :] = jnp.zeros((CHUNK, width), F32)
    kwin[0:ATT_PAST, :] = kc_ref[...].astype(BF16)
    kwin[ATT_PAST:BAND, :] = ks_ref[...].astype(BF16)
    kwin[BAND:KWIN, :] = jnp.zeros((KWIN - BAND, width), BF16)
    vtwin[:, 0:ATT_PAST] = vc_ref[...].T.astype(BF16)
    vtwin[:, ATT_PAST:KWIN] = jnp.concatenate(
        [vs_ref[...], jnp.zeros((KWIN - BAND, width), F32)], axis=0).T.astype(BF16)
    _attn_pairs(qpad, kwin, vtwin, bias_ref, o_ref, 1, lambda jj: None, CHUNK)


def _attn_sample(proj, cache_k, cache_v, bias2, t_prompt, n_seq, width):
    qb, kb, vb = 2, 3, 4
    c0 = t_prompt // CHUNK
    return pl.pallas_call(
        _attn_sample_kernel,
        grid=(n_seq,),
        in_specs=[pl.BlockSpec((CHUNK, width), lambda b: (c0 + b, qb)),
                  pl.BlockSpec((ATT_PAST, width), lambda b: (b, 0)),
                  pl.BlockSpec((CHUNK, width), lambda b: (c0 + b, kb)),
                  pl.BlockSpec((ATT_PAST, width), lambda b: (b, 0)),
                  pl.BlockSpec((CHUNK, width), lambda b: (c0 + b, vb)),
                  pl.BlockSpec(bias2.shape, lambda b: (0, 0, 0))],
        out_specs=pl.BlockSpec((CHUNK, width), lambda b: (b, 0)),
        out_shape=jax.ShapeDtypeStruct((n_seq * CHUNK, width), F32),
        scratch_shapes=[pltpu.VMEM((QPAIR, width), F32),
                        pltpu.VMEM((KWIN, width), BF16),
                        pltpu.VMEM((width, KWIN), BF16)],
        compiler_params=_cparams(("arbitrary",)),
        name="attn_sample",
    )(proj, cache_k, proj, cache_v, proj, bias2)


def _attn_bias(table):
    h = table.shape[0]
    x = np.arange(BAND + CHUNK - 1)
    rel = np.clip(BAND - 1 - x, -REL_CLIP, REL_CLIP) + REL_CLIP
    u = table[:, rel]
    std = jnp.stack([u[:, CHUNK - 1 - i:CHUNK - 1 - i + BAND] for i in range(CHUNK)], axis=1)
    neg = jnp.full((h, CHUNK, KWIN - BAND), NEG_BIG, F32)
    b = jnp.stack([jnp.concatenate([std, neg], axis=2),
                   jnp.concatenate([neg, std], axis=2)],
                  axis=1)
    b = b.reshape(h // 2, 2, 2, CHUNK, KWIN).transpose(0, 4, 1, 2, 3)
    return b.reshape(h // 2, KWIN, 2 * QPAIR)


def _outproj_kernel(y_ref, ap_ref, as_ref, xp_ref, xs_ref, gm_ref, scf_ref, shf_ref, npost_ref, npre_ref,
                    wo1_ref, wo2_ref, wrh_ref, wrl_ref, x1_ref, hf_ref, hb_ref, lg_ref, *, npt):
    i = pl.program_id(0)

    def body(a_ref, x_ref):
        tm, d = x_ref.shape
        mix = _bdot(y_ref[...].astype(BF16), wo1_ref[...]) + _bdot(a_ref[...].astype(BF16), wo2_ref[...])
        nm = _rms(mix, npost_ref[...]).reshape(tm // CHUNK, CHUNK, d)
        x1 = x_ref[...].reshape(tm // CHUNK, CHUNK, d) + gm_ref[...] * nm
        x1_ref[...] = x1.reshape(tm, d)
        hn = _rms(x1, npre_ref[...])
        hf = (hn * (1.0 + scf_ref[...]) + shf_ref[...]).reshape(tm, d)
        hf_ref[...] = hf
        h_hi = hf.astype(BF16)
        hb_ref[...] = h_hi
        h_lo = (hf - h_hi.astype(F32)).astype(BF16)
        lg_ref[...] = _bdot(h_hi, wrh_ref[...]) + _bdot(h_lo, wrh_ref[...]) + _bdot(h_hi, wrl_ref[...])

    @pl.when(i < npt)
    def _():
        body(ap_ref, xp_ref)

    @pl.when(i >= npt)
    def _():
        body(as_ref, xs_ref)


def _outproj(y_ssd, att_p, att_s, xp, xs, gate_m, scale_f, shift_f, npost, npre, wo1, wo2, wr):
    tp, d = xp.shape
    ts = xs.shape[0]
    t = tp + ts
    inner = y_ssd.shape[1]
    ne = wr.shape[1]
    wr_hi = wr.astype(BF16)
    wr_lo = (wr - wr_hi.astype(F32)).astype(BF16)
    tm = _pick((tp, ts), (256, 128, 64))
    nc = tm // CHUNK
    first, second = _split_rows(tp // tm)
    row = lambda i: (i, 0)
    tab = lambda i: (i, 0, 0)
    const = lambda i: (0, 0)
    return pl.pallas_call(
        functools.partial(_outproj_kernel, npt=tp // tm),
        grid=(t // tm,),
        in_specs=[pl.BlockSpec((tm, inner), row),
                  pl.BlockSpec((tm, att_p.shape[1]), first), pl.BlockSpec((tm, att_s.shape[1]), second),
                  pl.BlockSpec((tm, d), first), pl.BlockSpec((tm, d), second),
                  pl.BlockSpec((nc, 1, d), tab), pl.BlockSpec((nc, 1, d), tab), pl.BlockSpec((nc, 1, d), tab),
                  pl.BlockSpec((1, d), const), pl.BlockSpec((1, d), const),
                  pl.BlockSpec(wo1.shape, const), pl.BlockSpec(wo2.shape, const),
                  pl.BlockSpec(wr.shape, const), pl.BlockSpec(wr.shape, const)],
        out_specs=[pl.BlockSpec((tm, d), row), pl.BlockSpec((tm, d), row),
                   pl.BlockSpec((tm, d), row), pl.BlockSpec((tm, ne), row)],
        out_shape=[jax.ShapeDtypeStruct((t, d), F32), jax.ShapeDtypeStruct((t, d), F32),
                   jax.ShapeDtypeStruct((t, d), BF16), jax.ShapeDtypeStruct((t, ne), F32)],
        compiler_params=_cparams(("arbitrary",)),
        name="outproj",
    )(y_ssd, att_p, att_s, xp, xs, gate_m, scale_f, shift_f, npost, npre, wo1, wo2, wr_hi, wr_lo)


def _route_kernel(lg_ref, rb_ref, ti_ref, tw_ref):
    ne, tt = lg_ref.shape
    gs = ne // N_EXPERT_GROUPS
    scores = jax.nn.sigmoid(lg_ref[...])
    sel = scores + rb_ref[...]
    g3 = sel.reshape(N_EXPERT_GROUPS, gs, tt)
    i3 = lax.broadcasted_iota(I32, g3.shape, 1)
    m1 = jnp.max(g3, axis=1, keepdims=True)
    first = jnp.min(jnp.where(g3 == m1, i3, gs), axis=1, keepdims=True)
    m2 = jnp.max(jnp.where(i3 == first, -jnp.inf, g3), axis=1, keepdims=True)
    gscore = (m1 + m2).reshape(N_EXPERT_GROUPS, tt)
    gi = lax.broadcasted_iota(I32, gscore.shape, 0)
    gmask = jnp.zeros(gscore.shape, jnp.bool_)
    rem = gscore
    for _ in range(TOPK_GROUPS):
        mg = jnp.max(rem, axis=0, keepdims=True)
        pick = jnp.min(jnp.where(rem == mg, gi, N_EXPERT_GROUPS), axis=0, keepdims=True)
        hit = gi == pick
        gmask = gmask | hit
        rem = jnp.where(hit, -jnp.inf, rem)
    emask = jnp.broadcast_to(gmask.reshape(N_EXPERT_GROUPS, 1, tt), g3.shape).reshape(ne, tt)
    rem = jnp.where(emask, sel, -jnp.inf)
    ei = lax.broadcasted_iota(I32, (ne, tt), 0)
    idx, wts = [], []
    for _ in range(TOP_K):
        me = jnp.max(rem, axis=0, keepdims=True)
        pick = jnp.min(jnp.where(rem == me, ei, ne), axis=0, keepdims=True)
        hit = ei == pick
        idx.append(pick)
        wts.append(jnp.sum(jnp.where(hit, scores, 0.0), axis=0, keepdims=True))
        rem = jnp.where(hit, -jnp.inf, rem)
    w = jnp.concatenate(wts, axis=0)
    ti_ref[...] = jnp.concatenate(idx, axis=0)
    tw_ref[...] = w / jnp.sum(w, axis=0, keepdims=True) * ROUTED_SCALE


def _route(logits_t, router_bias):
    ne, t = logits_t.shape
    tt = _pick((t,), (2176, 2048, 1024, 512, 256, 128))
    return pl.pallas_call(
        _route_kernel,
        grid=(t // tt,),
        in_specs=[pl.BlockSpec((ne, tt), lambda i: (0, i)),
                  pl.BlockSpec((ne, 1), lambda i: (0, 0))],
        out_specs=[pl.BlockSpec((TOP_K, tt), lambda i: (0, i)),
                   pl.BlockSpec((TOP_K, tt), lambda i: (0, i))],
        out_shape=[jax.ShapeDtypeStruct((TOP_K, t), I32),
                   jax.ShapeDtypeStruct((TOP_K, t), F32)],
        compiler_params=_cparams(("arbitrary",)),
        name="route",
    )(logits_t, router_bias.reshape(ne, 1))


def _moe_kernel(be_ref, r0_ref, first_ref, a_ref,
                hf_hbm, w1_ref, w3_ref, w2_ref, o_ref,
                xbuf_a, xbuf_b, w1b, w3b, w2b, gsem):
    del be_ref
    b = pl.program_id(0)
    nb = pl.num_programs(0)
    tme = xbuf_a.shape[0]

    def gather_all(xb, s):
        return pltpu.make_async_copy(hf_hbm.at[pl.ds(0, tme)], xb, gsem.at[s])

    def start_gather(blk, xb, s):
        r0 = r0_ref[blk]
        for r in range(tme):
            tok = a_ref[r0 + r] >> 3
            pltpu.make_async_copy(hf_hbm.at[pl.ds(tok, 1)], xb.at[pl.ds(r, 1)], gsem.at[s]).start()

    @pl.when(b == 0)
    def _():
        start_gather(0, xbuf_a, 0)

    @pl.when(first_ref[b] == 1)
    def _():
        w1b[...] = w1_ref[0].astype(BF16)
        w3b[...] = w3_ref[0].astype(BF16)
        w2b[...] = w2_ref[0].astype(BF16)

    nxt = jnp.minimum(b + 1, nb - 1)

    def step(s, xb_cur, xb_nxt):
        gather_all(xb_cur, s).wait()
        start_gather(nxt, xb_nxt, 1 - s)
        x = xb_cur[...].astype(BF16)
        h = (_silu(_bdot(x, w1b[...])) * _bdot(x, w3b[...])).astype(BF16)
        o_ref[...] = _bdot(h, w2b[...])

        @pl.when(b == nb - 1)
        def _():
            gather_all(xb_nxt, 1 - s).wait()

    @pl.when(b % 2 == 0)
    def _():
        step(0, xbuf_a, xbuf_b)

    @pl.when(b % 2 == 1)
    def _():
        step(1, xbuf_b, xbuf_a)


def _moe(hf, a_sorted, blk_e, blk_r0, blk_first, w1, w3, w2):
    ne, d, de = w1.shape
    nb = blk_e.shape[0]
    tme = MOE_ROWS
    wmap = lambda b, be, r0, f, a: (be[b], 0, 0)
    grid_spec = pltpu.PrefetchScalarGridSpec(
        num_scalar_prefetch=4,
        grid=(nb,),
        in_specs=[pl.BlockSpec(memory_space=pl.ANY),
                  pl.BlockSpec((1, d, de), wmap),
                  pl.BlockSpec((1, d, de), wmap),
                  pl.BlockSpec((1, de, d), wmap)],
        out_specs=pl.BlockSpec((tme, d), lambda b, be, r0, f, a: (b, 0)),
        scratch_shapes=[pltpu.VMEM((tme, d), F32), pltpu.VMEM((tme, d), F32),
                        pltpu.VMEM((d, de), BF16), pltpu.VMEM((d, de), BF16), pltpu.VMEM((de, d), BF16),
                        pltpu.SemaphoreType.DMA((2,))],
    )
    return pl.pallas_call(
        _moe_kernel,
        grid_spec=grid_spec,
        out_shape=jax.ShapeDtypeStruct((nb * tme, d), F32),
        compiler_params=_cparams(("arbitrary",)),
        name="moe",
    )(blk_e, blk_r0, blk_first, a_sorted, hf, w1, w3, w2)


def _dispatch_plan(topi_t, tme, ne):
    k, t = topi_t.shape
    assert k == TOP_K
    a_cnt = k * t
    shift = int(np.ceil(np.log2(a_cnt)))
    assert ne << shift < 2 ** 31
    a_id = jnp.arange(t, dtype=I32)[None, :] * k + jnp.arange(k, dtype=I32)[:, None]
    keys = (topi_t << shift) + a_id
    a_sorted = jnp.sort(keys.reshape(-1)) & ((1 << shift) - 1)
    a_sorted = jnp.concatenate([a_sorted, jnp.zeros((tme,), I32)])
    is_e = topi_t[:, :, None] == jnp.arange(ne, dtype=I32)[None, None, :]
    chose = jnp.sum(is_e, axis=0, dtype=I32)
    rank = jnp.cumsum(chose, axis=0) - chose
    counts = jnp.sum(chose, axis=0)
    starts = jnp.cumsum(counts) - counts
    nblk = (counts + tme - 1) // tme
    blk_end = jnp.cumsum(nblk)
    nb = a_cnt // tme + ne
    b = jnp.arange(nb, dtype=I32)
    valid = b < blk_end[-1]
    e_raw = jnp.minimum(jnp.sum(blk_end[None, :] <= b[:, None], axis=1, dtype=I32), ne - 1)
    onehot = (e_raw[:, None] == jnp.arange(ne, dtype=I32)[None, :]).astype(I32)
    pick = lambda v: jnp.sum(onehot * v[None, :], axis=1)
    j = b - pick(blk_end - nblk)
    blk_r0 = jnp.where(valid, pick(starts) + j * tme, 0)
    blk_first = (valid & (j == 0)).astype(I32)
    e_last = jnp.max(jnp.where(valid, e_raw, 0))
    blk_e = jnp.where(valid, e_raw, e_last)
    out_row = ((blk_end - nblk) * tme)[None, :] + rank
    pos = jnp.sum(jnp.where(is_e, out_row[None], 0), axis=2).T.reshape(-1)
    return a_sorted, blk_e, blk_r0, blk_first, pos


def _final_kernel(pos_ref, eo_hbm, tw_ref, hb_ref, x1_ref, gf_ref, npost_ref, ws1_ref, ws3_ref, ws2_ref,
                  op_ref, os_ref, rows_a, rows_b, sem, *, npt):
    i = pl.program_id(0)
    nt = pl.num_programs(0)
    tm, d = x1_ref.shape

    def gather_all(buf, s):
        return pltpu.make_async_copy(eo_hbm.at[pl.ds(0, TOP_K * tm)], buf, sem.at[s])

    def start_gather(tile, buf, s):
        base = tile * (TOP_K * tm)
        for t in range(tm):
            for k in range(TOP_K):
                src = eo_hbm.at[pl.ds(pos_ref[base + t * TOP_K + k], 1)]
                pltpu.async_copy(src, buf.at[pl.ds(k * tm + t, 1)], sem.at[s], priority=k % 2)

    @pl.when(i == 0)
    def _():
        start_gather(0, rows_a, 0)

    nxt = jnp.minimum(i + 1, nt - 1)

    def step(s, cur, other):
        gather_all(cur, s).wait()
        start_gather(nxt, other, 1 - s)
        tw = tw_ref[...]
        routed = cur[0:tm, :] * tw[:, 0:1]
        for k in range(1, TOP_K):
            routed = routed + cur[k * tm:(k + 1) * tm, :] * tw[:, k:k + 1]
        hb = hb_ref[...]
        shared = _bdot((_silu(_bdot(hb, ws1_ref[...])) * _bdot(hb, ws3_ref[...])).astype(BF16), ws2_ref[...])
        nm = _rms(routed + shared, npost_ref[...]).reshape(tm // CHUNK, CHUNK, d)
        y = (x1_ref[...].reshape(tm // CHUNK, CHUNK, d) + gf_ref[...] * nm).reshape(tm, d)

        @pl.when(i < npt)
        def _():
            op_ref[...] = y

        @pl.when(i >= npt)
        def _():
            os_ref[...] = y

        @pl.when(i == nt - 1)
        def _():
            gather_all(other, 1 - s).wait()

    @pl.when(i % 2 == 0)
    def _():
        step(0, rows_a, rows_b)

    @pl.when(i % 2 == 1)
    def _():
        step(1, rows_b, rows_a)


def _final(eo, pos, topw, hb, x1, gate_f, npost, ws1, ws3, ws2, tp):
    t, d = x1.shape
    ts = t - tp
    tm = _pick((tp, ts), (128, 64))
    nc = tm // CHUNK
    nt = t // tm
    first, second = _split_rows(tp // tm)
    row = lambda i, *_: (i, 0)
    const = lambda i, *_: (0, 0)
    grid_spec = pltpu.PrefetchScalarGridSpec(
        num_scalar_prefetch=1,
        grid=(nt,),
        in_specs=[pl.BlockSpec(memory_space=pl.ANY),
                  pl.BlockSpec((tm, TOP_K), row),
                  pl.BlockSpec((tm, d), row), pl.BlockSpec((tm, d), row),
                  pl.BlockSpec((nc, 1, d), lambda i, *_: (i, 0, 0)),
                  pl.BlockSpec((1, d), const),
                  pl.BlockSpec(ws1.shape, const), pl.BlockSpec(ws3.shape, const),
                  pl.BlockSpec(ws2.shape, const)],
        out_specs=[pl.BlockSpec((tm, d), first), pl.BlockSpec((tm, d), second)],
        scratch_shapes=[pltpu.VMEM((TOP_K * tm, d), F32), pltpu.VMEM((TOP_K * tm, d), F32),
                        pltpu.SemaphoreType.DMA((2,))],
    )
    return pl.pallas_call(
        functools.partial(_final_kernel, npt=tp // tm),
        grid_spec=grid_spec,
        out_shape=[jax.ShapeDtypeStruct((tp, d), F32), jax.ShapeDtypeStruct((ts, d), F32)],
        compiler_params=_cparams(("arbitrary",)),
        name="final",
    )(pos, eo, topw, hb, x1, gate_f, npost, ws1, ws3, ws2)


def kernel(x_prompt, x_sample, cache_conv, state_ssm, cache_k, cache_v, c_prompt, c_sample,
           w_ada, b_ada, norm_pre_mix, norm_post_mix, norm_pre_ffn, norm_post_ffn,
           w_in, conv_w, conv_b, dt_bias, a_log, d_skip, gn_w, rel_bias, w_out,
           w_router, router_bias, w1, w3, w2, ws1, ws3, ws2):
    assert w_ada.shape[0] == 1, "single layer"
    bp, lp, d = x_prompt.shape
    bs, ls, _ = x_sample.shape
    assert bp == 1 and ls == CHUNK and lp % ATT_PAST == 0
    assert cache_k.shape[2] == ATT_PAST
    heads = a_log.shape[1]
    inner = heads * SSM_HEAD_DIM
    att_w = rel_bias.shape[1] * ATT_HEAD_DIM
    assert att_w == inner
    bcw = 2 * SSM_GROUPS * SSM_STATE
    ne = w_router.shape[2]
    tp, ts = bp * lp, bs * ls
    nseq = bp + bs

    xp, xs = x_prompt.reshape(tp, d), x_sample.reshape(ts, d)
    seq_np = np.concatenate([np.repeat(np.arange(bp), lp // CHUNK), bp + np.arange(bs)]).astype(np.int32)
    first_np = np.concatenate([[1], (seq_np[1:] != seq_np[:-1])]).astype(np.int32)
    seq_of_chunk, first_of_chunk = jnp.asarray(seq_np), jnp.asarray(first_np)

    c_all = jnp.concatenate([c_prompt, c_sample], axis=0)
    c_pad = jnp.pad(c_all, ((0, -nseq % 8), (0, 0)))
    mod = _ada(c_pad, w_ada[0], b_ada[0])[:nseq].reshape(nseq, 6, d)
    mod_c = mod[seq_of_chunk]
    shift_m, scale_m, gate_m, shift_f, scale_f, gate_f = [mod_c[:, i:i + 1, :] for i in range(6)]

    wi = w_in[0]
    o_z, o_x, o_bc = 0, inner, 2 * inner
    o_dt = inner + inner + bcw
    o_q = o_dt + heads
    o_k, o_v = o_q + att_w, o_q + 2 * att_w
    cols = lambda o, n: wi[:, o:o + n]
    w_main = jnp.concatenate([cols(o_z, inner), cols(o_x, inner), cols(o_q, att_w), cols(o_k, att_w),
                              cols(o_v, att_w), cols(o_bc, bcw)], axis=1).astype(BF16)
    w_dt = jnp.pad(cols(o_dt, heads), ((0, 0), (0, LANES - heads))).astype(BF16)
    proj, dt_raw = _inproj(xp, xs, scale_m, shift_m, norm_pre_mix, w_main, w_dt)
    c_x, c_k, c_v, c_bc = inner, 3 * inner, 4 * inner, 5 * inner

    pad_rows = lambda a: jnp.pad(a, ((0, 0), (8 - (CONV_W - 1), 0), (0, 0)))
    pre = jnp.concatenate([jnp.zeros((bp, CONV_W - 1, inner + bcw), F32), cache_conv[0]], axis=0)
    pre_x, pre_bc = pad_rows(pre[:, :, :inner]), pad_rows(pre[:, :, inner:])
    h0 = jnp.concatenate([jnp.zeros((bp,) + state_ssm.shape[2:], F32), state_ssm[0]], axis=0)
    h0t = h0.transpose(0, 3, 1, 2).reshape(nseq, SSM_STATE, inner)
    lane_pad = lambda v: jnp.pad(v, (0, LANES - heads)).reshape(1, LANES)
    expand = (np.arange(LANES)[:, None] == (np.arange(inner)[None, :] // SSM_HEAD_DIM)).astype(np.float32)
    tri = np.tril(np.ones((CHUNK, CHUNK), np.float32))
    consts = (conv_w[0][:, :inner], conv_w[0][:, inner:],
              conv_b[0][:inner].reshape(1, inner), conv_b[0][inner:].reshape(1, bcw),
              lane_pad(dt_bias[0]), lane_pad(-jnp.exp(a_log[0])),
              jnp.repeat(d_skip[0], SSM_HEAD_DIM).reshape(1, inner), gn_w[0].reshape(1, inner),
              jnp.asarray(expand, BF16), jnp.asarray(tri, BF16))
    y_ssd, st_out = _ssd(proj, dt_raw, seq_of_chunk, first_of_chunk, pre_x, pre_bc, h0t, consts, inner)

    bias2 = _attn_bias(rel_bias[0])
    att_p = _attn_prompt(proj, bias2, tp, att_w)
    att_s = _attn_sample(proj, cache_k[0].reshape(bs * ATT_PAST, att_w), cache_v[0].reshape(bs * ATT_PAST, att_w),
                         bias2, tp, bs, att_w)

    wo = w_out[0].astype(BF16)
    x1, hf, hb, logits = _outproj(y_ssd, att_p, att_s, xp, xs, gate_m, scale_f, shift_f,
                                  norm_post_mix, norm_pre_ffn, wo[:inner], wo[inner:], w_router[0])

    topi_t, topw_t = _route(logits.T, router_bias[0])
    a_sorted, blk_e, blk_r0, blk_first, pos = _dispatch_plan(topi_t, MOE_ROWS, ne)
    eo = _moe(hf, a_sorted, blk_e, blk_r0, blk_first, w1[0], w3[0], w2[0])
    y_p, y_s = _final(eo, pos, topw_t.T, hb, x1, gate_f, norm_post_ffn,
                      ws1[0].astype(BF16), ws3[0].astype(BF16), ws2[0].astype(BF16), tp)

    tail = lambda rows: jnp.concatenate([rows[..., c_x:c_x + inner], rows[..., c_bc:c_bc + bcw]], axis=-1)
    conv_prompt = tail(proj[tp - (CONV_W - 1):tp])[None, None]
    srows = lambda c0, n: proj[tp:, c0:c0 + n].reshape(bs, ls, n)[:, ls - (CONV_W - 1):, :]
    conv_sample = jnp.concatenate([srows(c_x, inner), srows(c_bc, bcw)], axis=-1)[None]
    st = st_out.reshape(nseq, SSM_STATE, heads, SSM_HEAD_DIM).transpose(0, 2, 3, 1)
    keep = min(ATT_PAST, lp)
    hd = (rel_bias.shape[1], ATT_HEAD_DIM)
    kv = lambda c0, r0, r1, b, l: proj[r0:r1, c0:c0 + att_w].reshape(b, l, *hd)[None]
    return (y_p.reshape(bp, lp, d), y_s.reshape(bs, ls, d),
            conv_prompt, st[:bp][None], kv(c_k, tp - keep, tp, bp, keep), kv(c_v, tp - keep, tp, bp, keep),
            conv_sample, st[bp:][None], kv(c_k, tp, tp + ts, bs, ls), kv(c_v, tp, tp + ts, bs, ls))
```

```python
import functools

import numpy as np
import jax
import jax.numpy as jnp
from jax import lax
from jax.experimental import pallas as pl
from jax.experimental.pallas import tpu as pltpu

F32 = jnp.float32
BF16 = jnp.bfloat16
I32 = jnp.int32
HIGHEST = lax.Precision.HIGHEST

CHUNK = 64
SSM_HEAD_DIM = 64
SSM_GROUPS = 2
SSM_STATE = 128
CONV_W = 4
ATT_HEAD_DIM = 64
LEFT_CHUNKS = 8
ATT_PAST = LEFT_CHUNKS * CHUNK
BAND = ATT_PAST + CHUNK
REL_CLIP = 128
TOP_K = 8
N_EXPERT_GROUPS = 8
TOPK_GROUPS = 4
ROUTED_SCALE = 2.5
EPS = 1e-6
NEG_BIG = -1e30

LANES = 128
PAIR = 2 * ATT_HEAD_DIM
QPAIR = 2 * CHUNK
KWIN = ATT_PAST + QPAIR
VMEM_LIMIT = 56 * 1024 * 1024
MOE_ROWS = 512
ATT_UNROLL = 8


def _cparams(sem):
    return pltpu.CompilerParams(dimension_semantics=sem, vmem_limit_bytes=VMEM_LIMIT)


def _pick(ns, cands):
    for c in cands:
        if all(n % c == 0 for n in ns):
            return c
    raise ValueError(f"no tile for {ns} in {cands}")


def _silu(x):
    return x * jax.nn.sigmoid(x)


def _rms(x, g):
    ms = jnp.mean(x * x, axis=-1, keepdims=True)
    return x * lax.rsqrt(ms + EPS) * g


def _bdot(a, b):
    return jnp.dot(a, b, preferred_element_type=F32)


def _split3(x):
    p0 = x.astype(BF16)
    r0 = x - p0.astype(F32)
    p1 = r0.astype(BF16)
    p2 = (r0 - p1.astype(F32)).astype(BF16)
    return p0, p1, p2


def _split_rows(npt):
    first = lambda i, *_: (jnp.minimum(i, npt - 1), 0)
    second = lambda i, *_: (jnp.maximum(i - npt, 0), 0)
    return first, second


def _ada_kernel(c_ref, w_ref, b_ref, o_ref):
    a = _silu(c_ref[...])
    o_ref[...] = jnp.dot(a, w_ref[...], precision=HIGHEST, preferred_element_type=F32) + b_ref[...]


def _ada(c_pad, w_ada, b_ada):
    m, d = c_pad.shape
    n = w_ada.shape[1]
    tn = _pick((n,), (1024, 512, 256, 128))
    return pl.pallas_call(
        _ada_kernel,
        grid=(n // tn,),
        in_specs=[pl.BlockSpec((m, d), lambda j: (0, 0)),
                  pl.BlockSpec((d, tn), lambda j: (0, j)),
                  pl.BlockSpec((1, tn), lambda j: (0, j))],
        out_specs=pl.BlockSpec((m, tn), lambda j: (0, j)),
        out_shape=jax.ShapeDtypeStruct((m, n), F32),
        compiler_params=_cparams(("arbitrary",)),
        name="ada",
    )(c_pad, w_ada, b_ada.reshape(1, n))


def _inproj_kernel(xp_ref, xs_ref, sc_ref, sh_ref, g_ref, w_ref, wdt_ref, o_ref, dt_ref, hm_ref, *, npt):
    i = pl.program_id(0)

    def prep(x_ref):
        x = x_ref[...]
        tm, d = x.shape
        y = _rms(x, g_ref[...]).reshape(tm // CHUNK, CHUNK, d)
        h = (y * (1.0 + sc_ref[...]) + sh_ref[...]).reshape(tm, d).astype(BF16)
        hm_ref[...] = h
        dt_ref[...] = _bdot(h, wdt_ref[...])

    @pl.when(pl.program_id(1) == 0)
    def _():
        @pl.when(i < npt)
        def _():
            prep(xp_ref)

        @pl.when(i >= npt)
        def _():
            prep(xs_ref)

    o_ref[...] = _bdot(hm_ref[...], w_ref[...])


def _inproj(xp, xs, scale, shift, g, w_main, w_dt):
    tp, d = xp.shape
    ts = xs.shape[0]
    t = tp + ts
    n = w_main.shape[1]
    tm = _pick((tp, ts), (1024, 512, 256, 128, 64))
    tn = _pick((n,), (512, 256, 128))
    nc = tm // CHUNK
    first, second = _split_rows(tp // tm)
    return pl.pallas_call(
        functools.partial(_inproj_kernel, npt=tp // tm),
        grid=(t // tm, n // tn),
        in_specs=[pl.BlockSpec((tm, d), first),
                  pl.BlockSpec((tm, d), second),
                  pl.BlockSpec((nc, 1, d), lambda i, j: (i, 0, 0)),
                  pl.BlockSpec((nc, 1, d), lambda i, j: (i, 0, 0)),
                  pl.BlockSpec((1, d), lambda i, j: (0, 0)),
                  pl.BlockSpec((d, tn), lambda i, j: (0, j)),
                  pl.BlockSpec((d, LANES), lambda i, j: (0, 0))],
        out_specs=[pl.BlockSpec((tm, tn), lambda i, j: (i, j)),
                   pl.BlockSpec((tm, LANES), lambda i, j: (i, 0))],
        out_shape=[jax.ShapeDtypeStruct((t, n), F32),
                   jax.ShapeDtypeStruct((t, LANES), F32)],
        scratch_shapes=[pltpu.VMEM((tm, d), BF16)],
        compiler_params=_cparams(("arbitrary", "arbitrary")),
        name="inproj",
    )(xp, xs, scale, shift, g, w_main, w_dt)


def _ssd_kernel(seq_ref, first_ref,
                z_ref, xs_ref, bc_ref, dt_ref, prex_ref, prebc_ref, h0_ref,
                cwx_ref, cwbc_ref, cbx_ref, cbbc_ref, dtb_ref, aneg_ref, dsk_ref, gnw_ref,
                e_ref, tri_ref,
                y_ref, st_out_ref,
                xpx_scr, xpbc_scr, st_scr):
    del seq_ref
    c = pl.program_id(0)
    inner = xs_ref.shape[1]
    gw = inner // SSM_GROUPS
    n = SSM_STATE
    pad = 8

    @pl.when(first_ref[c] == 1)
    def _():
        xpx_scr[0:pad, :] = prex_ref[0]
        xpbc_scr[0:pad, :] = prebc_ref[0]
        st_scr[...] = h0_ref[0]

    xpx_scr[pad:pad + CHUNK, :] = xs_ref[...]
    xpbc_scr[pad:pad + CHUNK, :] = bc_ref[...]

    def conv(xp, w_ref, b_ref):
        base = pad - (CONV_W - 1)
        acc = b_ref[...] + xp[base:base + CHUNK, :] * w_ref[0:1, :]
        for k in range(1, CONV_W):
            acc = acc + xp[base + k:base + k + CHUNK, :] * w_ref[k:k + 1, :]
        return _silu(acc)

    xs = conv(xpx_scr, cwx_ref, cbx_ref)
    bc = conv(xpbc_scr, cwbc_ref, cbbc_ref)
    xpx_scr[0:pad, :] = xpx_scr[CHUNK:CHUNK + pad, :]
    xpbc_scr[0:pad, :] = xpbc_scr[CHUNK:CHUNK + pad, :]

    dtv = dt_ref[...] + dtb_ref[...]
    dt = jnp.maximum(dtv, 0.0) + jnp.log(1.0 + jnp.exp(-jnp.abs(dtv)))
    da = dt * aneg_ref[...]
    acs = sum(_bdot(tri_ref[...], p) for p in _split3(da))
    full = sum(_bdot(p, e_ref[...]) for p in _split3(jnp.concatenate([dt, acs], axis=0)))
    dtf = full[0:CHUNK]
    af = full[CHUNK:2 * CHUNK]

    row = lax.broadcasted_iota(I32, (CHUNK, inner), 0)
    lj = lax.broadcasted_iota(I32, (CHUNK, inner), 1) & (SSM_HEAD_DIM - 1)
    aj = jnp.sum(jnp.where(row == lj, af, 0.0), axis=0, keepdims=True)
    lmat = jnp.exp(jnp.where(row >= lj, af - aj, NEG_BIG))
    alast = af[CHUNK - 1:CHUNK, :]
    xdt = xs * dtf
    xw = xdt * jnp.exp(alast - af)
    cdec = jnp.exp(alast)
    eaf = jnp.exp(af)

    lane = lax.broadcasted_iota(I32, (CHUNK, PAIR), 1)
    st = st_scr[...]
    ydiag, yoff, stn = [], [], []
    for g in range(SSM_GROUPS):
        bg = bc[:, g * n:(g + 1) * n].astype(BF16)
        cg = bc[:, (SSM_GROUPS + g) * n:(SSM_GROUPS + g + 1) * n].astype(BF16)
        bb = jnp.concatenate([bg, bg], axis=0)
        cbb = lax.dot_general(cg, bb, (((1,), (1,)), ((), ())), preferred_element_type=F32)
        stg = st[:, g * gw:(g + 1) * gw]
        yoff.append(_bdot(cg, stg.astype(BF16)))
        for p in range(gw // PAIR):
            lo = g * gw + p * PAIR
            m = (cbb * lmat[:, lo:lo + PAIR]).astype(BF16)
            xd = xdt[:, lo:lo + PAIR]
            w = jnp.concatenate([jnp.where(lane < SSM_HEAD_DIM, xd, 0.0),
                                 jnp.where(lane >= SSM_HEAD_DIM, xd, 0.0)], axis=0).astype(BF16)
            ydiag.append(_bdot(m, w))
        upd = lax.dot_general(bg, xw[:, g * gw:(g + 1) * gw].astype(BF16),
                              (((0,), (0,)), ((), ())), preferred_element_type=F32)
        stn.append(cdec[:, g * gw:(g + 1) * gw] * stg + upd)

    y = jnp.concatenate(ydiag, axis=1) + jnp.concatenate(yoff, axis=1) * eaf + dsk_ref[...] * xs
    y = y * _silu(z_ref[...])
    y_ref[...] = _rms(y, gnw_ref[...])
    st_new = jnp.concatenate(stn, axis=1)
    st_scr[...] = st_new
    st_out_ref[0] = st_new


def _ssd(proj, dt_raw, seq_of_chunk, first_of_chunk, pre_x, pre_bc, h0t, consts, inner):
    t = proj.shape[0]
    nch = t // CHUNK
    nseq = h0t.shape[0]
    bcw = pre_bc.shape[-1]
    assert (5 * inner) % bcw == 0
    cmap = lambda blk: (lambda c, s, f: (c, blk))
    smap3 = lambda c, s, f: (s[c], 0, 0)
    const2 = lambda c, s, f: (0, 0)
    grid_spec = pltpu.PrefetchScalarGridSpec(
        num_scalar_prefetch=2,
        grid=(nch,),
        in_specs=[pl.BlockSpec((CHUNK, inner), cmap(0)),
                  pl.BlockSpec((CHUNK, inner), cmap(1)),
                  pl.BlockSpec((CHUNK, bcw), cmap((5 * inner) // bcw)),
                  pl.BlockSpec((CHUNK, LANES), lambda c, s, f: (c, 0)),
                  pl.BlockSpec((1, 8, inner), smap3),
                  pl.BlockSpec((1, 8, bcw), smap3),
                  pl.BlockSpec((1, SSM_STATE, inner), smap3)]
                 + [pl.BlockSpec(a.shape, const2) for a in consts],
        out_specs=[pl.BlockSpec((CHUNK, inner), lambda c, s, f: (c, 0)),
                   pl.BlockSpec((1, SSM_STATE, inner), smap3)],
        scratch_shapes=[pltpu.VMEM((CHUNK + 8, inner), F32),
                        pltpu.VMEM((CHUNK + 8, bcw), F32),
                        pltpu.VMEM((SSM_STATE, inner), F32)],
    )
    return pl.pallas_call(
        _ssd_kernel,
        grid_spec=grid_spec,
        out_shape=[jax.ShapeDtypeStruct((t, inner), F32),
                   jax.ShapeDtypeStruct((nseq, SSM_STATE, inner), F32)],
        compiler_params=_cparams(("arbitrary",)),
        name="ssd",
    )(seq_of_chunk, first_of_chunk, proj, proj, proj, dt_raw, pre_x, pre_bc, h0t, *consts)


def _attn_pairs(q_ref, kwin, vtwin, bias_ref, o_ref, n_steps, n_masked_fn, out_rows):
    n_pairs = q_ref.shape[1] // PAIR
    rowp = lax.broadcasted_iota(I32, (PAIR, QPAIR), 0)
    krow = lax.broadcasted_iota(I32, (KWIN, 2 * QPAIR), 0)

    for jj in range(n_steps):
        n_masked = n_masked_fn(jj)

        def one_pair(hp, jj=jj, n_masked=n_masked):
            lo = hp * PAIR if isinstance(hp, int) else pl.multiple_of(hp * PAIR, PAIR)
            q = q_ref[jj * QPAIR:(jj + 1) * QPAIR, pl.ds(lo, PAIR)] * (ATT_HEAD_DIM ** -0.5)
            qt = q.T
            w = jnp.concatenate([jnp.where(rowp < ATT_HEAD_DIM, qt, 0.0),
                                 jnp.where(rowp >= ATT_HEAD_DIM, qt, 0.0)], axis=1).astype(BF16)
            kb = kwin[jj * QPAIR:jj * QPAIR + KWIN, pl.ds(lo, PAIR)]
            s = _bdot(kb, w) + bias_ref[hp]
            if n_masked is not None and n_masked > 0:
                s = jnp.where(krow < n_masked, NEG_BIG, s)
            mx = jnp.max(s, axis=0, keepdims=True)
            p = jnp.exp(s - mx)
            den = jnp.sum(p, axis=0, keepdims=True)
            vb = vtwin[pl.ds(lo, PAIR), jj * QPAIR:jj * QPAIR + KWIN]
            o2 = _bdot(vb, p.astype(BF16)) / den
            ot = jnp.where(rowp < ATT_HEAD_DIM, o2[:, 0:QPAIR], o2[:, QPAIR:2 * QPAIR])
            o_ref[jj * out_rows:(jj + 1) * out_rows, pl.ds(lo, PAIR)] = ot.T[0:out_rows]

        def body(i, carry, one_pair=one_pair):
            for u in range(ATT_UNROLL):
                one_pair(i * ATT_UNROLL + u)
            return carry

        if n_pairs == ATT_UNROLL:
            body(0, 0)
        else:
            lax.fori_loop(0, n_pairs // ATT_UNROLL, body, 0)


def _attn_prompt_kernel(q_ref, kp_ref, kc_ref, vp_ref, vc_ref, bias_ref, o_ref, kwin, vtwin):
    i = pl.program_id(0)
    tq = q_ref.shape[0]
    kwin[0:ATT_PAST, :] = kp_ref[...].astype(BF16)
    kwin[ATT_PAST:ATT_PAST + tq, :] = kc_ref[...].astype(BF16)
    vtwin[:, 0:ATT_PAST] = vp_ref[...].T.astype(BF16)
    vtwin[:, ATT_PAST:ATT_PAST + tq] = vc_ref[...].T.astype(BF16)
    @pl.when(i == 0)
    def _():
        _attn_pairs(q_ref, kwin, vtwin, bias_ref, o_ref, tq // QPAIR, lambda jj: ATT_PAST - jj * QPAIR, QPAIR)

    @pl.when(i > 0)
    def _():
        _attn_pairs(q_ref, kwin, vtwin, bias_ref, o_ref, tq // QPAIR, lambda jj: None, QPAIR)


def _attn_prompt(proj, bias2, t_prompt, width):
    tq = ATT_PAST
    assert t_prompt % tq == 0
    qb, kb, vb = 2, 3, 4
    prev = lambda i: jnp.maximum(i - 1, 0)
    return pl.pallas_call(
        _attn_prompt_kernel,
        grid=(t_prompt // tq,),
        in_specs=[pl.BlockSpec((tq, width), lambda i: (i, qb)),
                  pl.BlockSpec((tq, width), lambda i: (prev(i), kb)),
                  pl.BlockSpec((tq, width), lambda i: (i, kb)),
                  pl.BlockSpec((tq, width), lambda i: (prev(i), vb)),
                  pl.BlockSpec((tq, width), lambda i: (i, vb)),
                  pl.BlockSpec(bias2.shape, lambda i: (0, 0, 0))],
        out_specs=pl.BlockSpec((tq, width), lambda i: (i, 0)),
        out_shape=jax.ShapeDtypeStruct((t_prompt, width), F32),
        scratch_shapes=[pltpu.VMEM((ATT_PAST + tq, width), BF16),
                        pltpu.VMEM((width, ATT_PAST + tq), BF16)],
        compiler_params=_cparams(("arbitrary",)),
        name="attn_prompt",
    )(proj, proj, proj, proj, proj, bias2)


def _attn_sample_kernel(q_ref, kc_ref, ks_ref, vc_ref, vs_ref, bias_ref, o_ref, qpad, kwin, vtwin):
    width = q_ref.shape[1]
    qpad[0:CHUNK, :] = q_ref[...]
    qpad[CHUNK:QPAIR, :] = jnp.zeros((CHUNK, width), F32)
    kwin[0:ATT_PAST, :] = kc_ref[...].astype(BF16)
    kwin[ATT_PAST:BAND, :] = ks_ref[...].astype(BF16)
    kwin[BAND:KWIN, :] = jnp.zeros((KWIN - BAND, width), BF16)
    vtwin[:, 0:ATT_PAST] = vc_ref[...].T.astype(BF16)
    vtwin[:, ATT_PAST:KWIN] = jnp.concatenate(
        [vs_ref[...], jnp.zeros((KWIN - BAND, width), F32)], axis=0).T.astype(BF16)
    _attn_pairs(qpad, kwin, vtwin, bias_ref, o_ref, 1, lambda jj: None, CHUNK)


def _attn_sample(proj, cache_k, cache_v, bias2, t_prompt, n_seq, width):
    qb, kb, vb = 2, 3, 4
    c0 = t_prompt // CHUNK
    return pl.pallas_call(
        _attn_sample_kernel,
        grid=(n_seq,),
        in_specs=[pl.BlockSpec((CHUNK, width), lambda b: (c0 + b, qb)),
                  pl.BlockSpec((ATT_PAST, width), lambda b: (b, 0)),
                  pl.BlockSpec((CHUNK, width), lambda b: (c0 + b, kb)),
                  pl.BlockSpec((ATT_PAST, width), lambda b: (b, 0)),
                  pl.BlockSpec((CHUNK, width), lambda b: (c0 + b, vb)),
                  pl.BlockSpec(bias2.shape, lambda b: (0, 0, 0))],
        out_specs=pl.BlockSpec((CHUNK, width), lambda b: (b, 0)),
        out_shape=jax.ShapeDtypeStruct((n_seq * CHUNK, width), F32),
        scratch_shapes=[pltpu.VMEM((QPAIR, width), F32),
                        pltpu.VMEM((KWIN, width), BF16),
                        pltpu.VMEM((width, KWIN), BF16)],
        compiler_params=_cparams(("arbitrary",)),
        name="attn_sample",
    )(proj, cache_k, proj, cache_v, proj, bias2)


def _attn_bias(table):
    h = table.shape[0]
    x = np.arange(BAND + CHUNK - 1)
    rel = np.clip(BAND - 1 - x, -REL_CLIP, REL_CLIP) + REL_CLIP
    u = table[:, rel]
    std = jnp.stack([u[:, CHUNK - 1 - i:CHUNK - 1 - i + BAND] for i in range(CHUNK)], axis=1)
    neg = jnp.full((h, CHUNK, KWIN - BAND), NEG_BIG, F32)
    b = jnp.stack([jnp.concatenate([std, neg], axis=2),
                   jnp.concatenate([neg, std], axis=2)],
                  axis=1)
    b = b.reshape(h // 2, 2, 2, CHUNK, KWIN).transpose(0, 4, 1, 2, 3)
    return b.reshape(h // 2, KWIN, 2 * QPAIR)


def _outproj_kernel(y_ref, ap_ref, as_ref, xp_ref, xs_ref, gm_ref, scf_ref, shf_ref, npost_ref, npre_ref,
                    wo1_ref, wo2_ref, wrh_ref, wrl_ref, x1_ref, hf_ref, hb_ref, lg_ref, *, npt):
    i = pl.program_id(0)

    def body(a_ref, x_ref):
        tm, d = x_ref.shape
        mix = _bdot(y_ref[...].astype(BF16), wo1_ref[...]) + _bdot(a_ref[...].astype(BF16), wo2_ref[...])
        nm = _rms(mix, npost_ref[...]).reshape(tm // CHUNK, CHUNK, d)
        x1 = x_ref[...].reshape(tm // CHUNK, CHUNK, d) + gm_ref[...] * nm
        x1_ref[...] = x1.reshape(tm, d)
        hn = _rms(x1, npre_ref[...])
        hf = (hn * (1.0 + scf_ref[...]) + shf_ref[...]).reshape(tm, d)
        hf_ref[...] = hf
        h_hi = hf.astype(BF16)
        hb_ref[...] = h_hi
        h_lo = (hf - h_hi.astype(F32)).astype(BF16)
        lg_ref[...] = _bdot(h_hi, wrh_ref[...]) + _bdot(h_lo, wrh_ref[...]) + _bdot(h_hi, wrl_ref[...])

    @pl.when(i < npt)
    def _():
        body(ap_ref, xp_ref)

    @pl.when(i >= npt)
    def _():
        body(as_ref, xs_ref)


def _outproj(y_ssd, att_p, att_s, xp, xs, gate_m, scale_f, shift_f, npost, npre, wo1, wo2, wr):
    tp, d = xp.shape
    ts = xs.shape[0]
    t = tp + ts
    inner = y_ssd.shape[1]
    ne = wr.shape[1]
    wr_hi = wr.astype(BF16)
    wr_lo = (wr - wr_hi.astype(F32)).astype(BF16)
    tm = _pick((tp, ts), (256, 128, 64))
    nc = tm // CHUNK
    first, second = _split_rows(tp // tm)
    row = lambda i: (i, 0)
    tab = lambda i: (i, 0, 0)
    const = lambda i: (0, 0)
    return pl.pallas_call(
        functools.partial(_outproj_kernel, npt=tp // tm),
        grid=(t // tm,),
        in_specs=[pl.BlockSpec((tm, inner), row),
                  pl.BlockSpec((tm, att_p.shape[1]), first), pl.BlockSpec((tm, att_s.shape[1]), second),
                  pl.BlockSpec((tm, d), first), pl.BlockSpec((tm, d), second),
                  pl.BlockSpec((nc, 1, d), tab), pl.BlockSpec((nc, 1, d), tab), pl.BlockSpec((nc, 1, d), tab),
                  pl.BlockSpec((1, d), const), pl.BlockSpec((1, d), const),
                  pl.BlockSpec(wo1.shape, const), pl.BlockSpec(wo2.shape, const),
                  pl.BlockSpec(wr.shape, const), pl.BlockSpec(wr.shape, const)],
        out_specs=[pl.BlockSpec((tm, d), row), pl.BlockSpec((tm, d), row),
                   pl.BlockSpec((tm, d), row), pl.BlockSpec((tm, ne), row)],
        out_shape=[jax.ShapeDtypeStruct((t, d), F32), jax.ShapeDtypeStruct((t, d), F32),
                   jax.ShapeDtypeStruct((t, d), BF16), jax.ShapeDtypeStruct((t, ne), F32)],
        compiler_params=_cparams(("arbitrary",)),
        name="outproj",
    )(y_ssd, att_p, att_s, xp, xs, gate_m, scale_f, shift_f, npost, npre, wo1, wo2, wr_hi, wr_lo)


def _route_kernel(lg_ref, rb_ref, ti_ref, tw_ref):
    ne, tt = lg_ref.shape
    gs = ne // N_EXPERT_GROUPS
    scores = jax.nn.sigmoid(lg_ref[...])
    sel = scores + rb_ref[...]
    g3 = sel.reshape(N_EXPERT_GROUPS, gs, tt)
    i3 = lax.broadcasted_iota(I32, g3.shape, 1)
    m1 = jnp.max(g3, axis=1, keepdims=True)
    first = jnp.min(jnp.where(g3 == m1, i3, gs), axis=1, keepdims=True)
    m2 = jnp.max(jnp.where(i3 == first, -jnp.inf, g3), axis=1, keepdims=True)
    gscore = (m1 + m2).reshape(N_EXPERT_GROUPS, tt)
    gi = lax.broadcasted_iota(I32, gscore.shape, 0)
    gmask = jnp.zeros(gscore.shape, jnp.bool_)
    rem = gscore
    for _ in range(TOPK_GROUPS):
        mg = jnp.max(rem, axis=0, keepdims=True)
        pick = jnp.min(jnp.where(rem == mg, gi, N_EXPERT_GROUPS), axis=0, keepdims=True)
        hit = gi == pick
        gmask = gmask | hit
        rem = jnp.where(hit, -jnp.inf, rem)
    emask = jnp.broadcast_to(gmask.reshape(N_EXPERT_GROUPS, 1, tt), g3.shape).reshape(ne, tt)
    rem = jnp.where(emask, sel, -jnp.inf)
    ei = lax.broadcasted_iota(I32, (ne, tt), 0)
    idx, wts = [], []
    for _ in range(TOP_K):
        me = jnp.max(rem, axis=0, keepdims=True)
        pick = jnp.min(jnp.where(rem == me, ei, ne), axis=0, keepdims=True)
        hit = ei == pick
        idx.append(pick)
        wts.append(jnp.sum(jnp.where(hit, scores, 0.0), axis=0, keepdims=True))
        rem = jnp.where(hit, -jnp.inf, rem)
    w = jnp.concatenate(wts, axis=0)
    ti_ref[...] = jnp.concatenate(idx, axis=0)
    tw_ref[...] = w / jnp.sum(w, axis=0, keepdims=True) * ROUTED_SCALE


def _route(logits_t, router_bias):
    ne, t = logits_t.shape
    tt = _pick((t,), (2176, 2048, 1024, 512, 256, 128))
    return pl.pallas_call(
        _route_kernel,
        grid=(t // tt,),
        in_specs=[pl.BlockSpec((ne, tt), lambda i: (0, i)),
                  pl.BlockSpec((ne, 1), lambda i: (0, 0))],
        out_specs=[pl.BlockSpec((TOP_K, tt), lambda i: (0, i)),
                   pl.BlockSpec((TOP_K, tt), lambda i: (0, i))],
        out_shape=[jax.ShapeDtypeStruct((TOP_K, t), I32),
                   jax.ShapeDtypeStruct((TOP_K, t), F32)],
        compiler_params=_cparams(("arbitrary",)),
        name="route",
    )(logits_t, router_bias.reshape(ne, 1))


def _moe_kernel(be_ref, r0_ref, first_ref, a_ref,
                hf_hbm, w1_ref, w3_ref, w2_ref, o_ref,
                xbuf_a, xbuf_b, w1b, w3b, w2b, gsem):
    del be_ref
    b = pl.program_id(0)
    nb = pl.num_programs(0)
    tme = xbuf_a.shape[0]

    def gather_all(xb, s):
        return pltpu.make_async_copy(hf_hbm.at[pl.ds(0, tme)], xb, gsem.at[s])

    def start_gather(blk, xb, s):
        r0 = r0_ref[blk]
        for r in range(tme):
            tok = a_ref[r0 + r] >> 3
            pltpu.async_copy(hf_hbm.at[pl.ds(tok, 1)], xb.at[pl.ds(r, 1)], gsem.at[s], priority=r % 2)

    @pl.when(b == 0)
    def _():
        start_gather(0, xbuf_a, 0)

    @pl.when(first_ref[b] == 1)
    def _():
        w1b[...] = w1_ref[0].astype(BF16)
        w3b[...] = w3_ref[0].astype(BF16)
        w2b[...] = w2_ref[0].astype(BF16)

    nxt = jnp.minimum(b + 1, nb - 1)

    def step(s, xb_cur, xb_nxt):
        gather_all(xb_cur, s).wait()
        start_gather(nxt, xb_nxt, 1 - s)
        x = xb_cur[...].astype(BF16)
        h = (_silu(_bdot(x, w1b[...])) * _bdot(x, w3b[...])).astype(BF16)
        o_ref[...] = _bdot(h, w2b[...])

        @pl.when(b == nb - 1)
        def _():
            gather_all(xb_nxt, 1 - s).wait()

    @pl.when(b % 2 == 0)
    def _():
        step(0, xbuf_a, xbuf_b)

    @pl.when(b % 2 == 1)
    def _():
        step(1, xbuf_b, xbuf_a)


def _moe(hf, a_sorted, blk_e, blk_r0, blk_first, w1, w3, w2):
    ne, d, de = w1.shape
    nb = blk_e.shape[0]
    tme = MOE_ROWS
    wmap = lambda b, be, r0, f, a: (be[b], 0, 0)
    grid_spec = pltpu.PrefetchScalarGridSpec(
        num_scalar_prefetch=4,
        grid=(nb,),
        in_specs=[pl.BlockSpec(memory_space=pl.ANY),
                  pl.BlockSpec((1, d, de), wmap),
                  pl.BlockSpec((1, d, de), wmap),
                  pl.BlockSpec((1, de, d), wmap)],
        out_specs=pl.BlockSpec((tme, d), lambda b, be, r0, f, a: (b, 0)),
        scratch_shapes=[pltpu.VMEM((tme, d), F32), pltpu.VMEM((tme, d), F32),
                        pltpu.VMEM((d, de), BF16), pltpu.VMEM((d, de), BF16), pltpu.VMEM((de, d), BF16),
                        pltpu.SemaphoreType.DMA((2,))],
    )
    return pl.pallas_call(
        _moe_kernel,
        grid_spec=grid_spec,
        out_shape=jax.ShapeDtypeStruct((nb * tme, d), F32),
        compiler_params=_cparams(("arbitrary",)),
        name="moe",
    )(blk_e, blk_r0, blk_first, a_sorted, hf, w1, w3, w2)


def _dispatch_plan(topi_t, tme, ne):
    k, t = topi_t.shape
    assert k == TOP_K
    a_cnt = k * t
    shift = int(np.ceil(np.log2(a_cnt)))
    assert ne << shift < 2 ** 31
    a_id = jnp.arange(t, dtype=I32)[None, :] * k + jnp.arange(k, dtype=I32)[:, None]
    keys = (topi_t << shift) + a_id
    a_sorted = jnp.sort(keys.reshape(-1)) & ((1 << shift) - 1)
    a_sorted = jnp.concatenate([a_sorted, jnp.zeros((tme,), I32)])
    is_e = topi_t[:, :, None] == jnp.arange(ne, dtype=I32)[None, None, :]
    chose = jnp.sum(is_e, axis=0, dtype=I32)
    rank = jnp.cumsum(chose, axis=0) - chose
    counts = jnp.sum(chose, axis=0)
    starts = jnp.cumsum(counts) - counts
    nblk = (counts + tme - 1) // tme
    blk_end = jnp.cumsum(nblk)
    nb = a_cnt // tme + ne
    b = jnp.arange(nb, dtype=I32)
    valid = b < blk_end[-1]
    e_raw = jnp.minimum(jnp.sum(blk_end[None, :] <= b[:, None], axis=1, dtype=I32), ne - 1)
    onehot = (e_raw[:, None] == jnp.arange(ne, dtype=I32)[None, :]).astype(I32)
    pick = lambda v: jnp.sum(onehot * v[None, :], axis=1)
    j = b - pick(blk_end - nblk)
    blk_r0 = jnp.where(valid, pick(starts) + j * tme, 0)
    blk_first = (valid & (j == 0)).astype(I32)
    e_last = jnp.max(jnp.where(valid, e_raw, 0))
    blk_e = jnp.where(valid, e_raw, e_last)
    out_row = ((blk_end - nblk) * tme)[None, :] + rank
    pos = jnp.sum(jnp.where(is_e, out_row[None], 0), axis=2).T.reshape(-1)
    return a_sorted, blk_e, blk_r0, blk_first, pos


def _final_kernel(pos_ref, eo_hbm, tw_ref, hb_ref, x1_ref, gf_ref, npost_ref, ws1_ref, ws3_ref, ws2_ref,
                  op_ref, os_ref, rows_a, rows_b, sem, *, npt):
    i = pl.program_id(0)
    nt = pl.num_programs(0)
    tm, d = x1_ref.shape

    def gather_all(buf, s):
        return pltpu.make_async_copy(eo_hbm.at[pl.ds(0, TOP_K * tm)], buf, sem.at[s])

    def start_gather(tile, buf, s):
        base = tile * (TOP_K * tm)
        for t in range(tm):
            for k in range(TOP_K):
                src = eo_hbm.at[pl.ds(pos_ref[base + t * TOP_K + k], 1)]
                pltpu.async_copy(src, buf.at[pl.ds(k * tm + t, 1)], sem.at[s], priority=k % 2)

    @pl.when(i == 0)
    def _():
        start_gather(0, rows_a, 0)

    nxt = jnp.minimum(i + 1, nt - 1)

    def step(s, cur, other):
        gather_all(cur, s).wait()
        start_gather(nxt, other, 1 - s)
        tw = tw_ref[...]
        routed = cur[0:tm, :] * tw[:, 0:1]
        for k in range(1, TOP_K):
            routed = routed + cur[k * tm:(k + 1) * tm, :] * tw[:, k:k + 1]
        hb = hb_ref[...]
        shared = _bdot((_silu(_bdot(hb, ws1_ref[...])) * _bdot(hb, ws3_ref[...])).astype(BF16), ws2_ref[...])
        nm = _rms(routed + shared, npost_ref[...]).reshape(tm // CHUNK, CHUNK, d)
        y = (x1_ref[...].reshape(tm // CHUNK, CHUNK, d) + gf_ref[...] * nm).reshape(tm, d)

        @pl.when(i < npt)
        def _():
            op_ref[...] = y

        @pl.when(i >= npt)
        def _():
            os_ref[...] = y

        @pl.when(i == nt - 1)
        def _():
            gather_all(other, 1 - s).wait()

    @pl.when(i % 2 == 0)
    def _():
        step(0, rows_a, rows_b)

    @pl.when(i % 2 == 1)
    def _():
        step(1, rows_b, rows_a)


def _final(eo, pos, topw, hb, x1, gate_f, npost, ws1, ws3, ws2, tp):
    t, d = x1.shape
    ts = t - tp
    tm = _pick((tp, ts), (128, 64))
    nc = tm // CHUNK
    nt = t // tm
    first, second = _split_rows(tp // tm)
    row = lambda i, *_: (i, 0)
    const = lambda i, *_: (0, 0)
    grid_spec = pltpu.PrefetchScalarGridSpec(
        num_scalar_prefetch=1,
        grid=(nt,),
        in_specs=[pl.BlockSpec(memory_space=pl.ANY),
                  pl.BlockSpec((tm, TOP_K), row),
                  pl.BlockSpec((tm, d), row), pl.BlockSpec((tm, d), row),
                  pl.BlockSpec((nc, 1, d), lambda i, *_: (i, 0, 0)),
                  pl.BlockSpec((1, d), const),
                  pl.BlockSpec(ws1.shape, const), pl.BlockSpec(ws3.shape, const),
                  pl.BlockSpec(ws2.shape, const)],
        out_specs=[pl.BlockSpec((tm, d), first), pl.BlockSpec((tm, d), second)],
        scratch_shapes=[pltpu.VMEM((TOP_K * tm, d), F32), pltpu.VMEM((TOP_K * tm, d), F32),
                        pltpu.SemaphoreType.DMA((2,))],
    )
    return pl.pallas_call(
        functools.partial(_final_kernel, npt=tp // tm),
        grid_spec=grid_spec,
        out_shape=[jax.ShapeDtypeStruct((tp, d), F32), jax.ShapeDtypeStruct((ts, d), F32)],
        compiler_params=_cparams(("arbitrary",)),
        name="final",
    )(pos, eo, topw, hb, x1, gate_f, npost, ws1, ws3, ws2)


def kernel(x_prompt, x_sample, cache_conv, state_ssm, cache_k, cache_v, c_prompt, c_sample,
           w_ada, b_ada, norm_pre_mix, norm_post_mix, norm_pre_ffn, norm_post_ffn,
           w_in, conv_w, conv_b, dt_bias, a_log, d_skip, gn_w, rel_bias, w_out,
           w_router, router_bias, w1, w3, w2, ws1, ws3, ws2):
    assert w_ada.shape[0] == 1, "single layer"
    bp, lp, d = x_prompt.shape
    bs, ls, _ = x_sample.shape
    assert bp == 1 and ls == CHUNK and lp % ATT_PAST == 0
    assert cache_k.shape[2] == ATT_PAST
    heads = a_log.shape[1]
    inner = heads * SSM_HEAD_DIM
    att_w = rel_bias.shape[1] * ATT_HEAD_DIM
    assert att_w == inner
    bcw = 2 * SSM_GROUPS * SSM_STATE
    ne = w_router.shape[2]
    tp, ts = bp * lp, bs * ls
    nseq = bp + bs

    xp, xs = x_prompt.reshape(tp, d), x_sample.reshape(ts, d)
    seq_np = np.concatenate([np.repeat(np.arange(bp), lp // CHUNK), bp + np.arange(bs)]).astype(np.int32)
    first_np = np.concatenate([[1], (seq_np[1:] != seq_np[:-1])]).astype(np.int32)
    seq_of_chunk, first_of_chunk = jnp.asarray(seq_np), jnp.asarray(first_np)

    c_all = jnp.concatenate([c_prompt, c_sample], axis=0)
    c_pad = jnp.pad(c_all, ((0, -nseq % 8), (0, 0)))
    mod = _ada(c_pad, w_ada[0], b_ada[0])[:nseq].reshape(nseq, 6, d)
    mod_c = mod[seq_of_chunk]
    shift_m, scale_m, gate_m, shift_f, scale_f, gate_f = [mod_c[:, i:i + 1, :] for i in range(6)]

    wi = w_in[0]
    o_z, o_x, o_bc = 0, inner, 2 * inner
    o_dt = inner + inner + bcw
    o_q = o_dt + heads
    o_k, o_v = o_q + att_w, o_q + 2 * att_w
    cols = lambda o, n: wi[:, o:o + n]
    w_main = jnp.concatenate([cols(o_z, inner), cols(o_x, inner), cols(o_q, att_w), cols(o_k, att_w),
                              cols(o_v, att_w), cols(o_bc, bcw)], axis=1).astype(BF16)
    w_dt = jnp.pad(cols(o_dt, heads), ((0, 0), (0, LANES - heads))).astype(BF16)
    proj, dt_raw = _inproj(xp, xs, scale_m, shift_m, norm_pre_mix, w_main, w_dt)
    c_x, c_k, c_v, c_bc = inner, 3 * inner, 4 * inner, 5 * inner

    pad_rows = lambda a: jnp.pad(a, ((0, 0), (8 - (CONV_W - 1), 0), (0, 0)))
    pre = jnp.concatenate([jnp.zeros((bp, CONV_W - 1, inner + bcw), F32), cache_conv[0]], axis=0)
    pre_x, pre_bc = pad_rows(pre[:, :, :inner]), pad_rows(pre[:, :, inner:])
    h0 = jnp.concatenate([jnp.zeros((bp,) + state_ssm.shape[2:], F32), state_ssm[0]], axis=0)
    h0t = h0.transpose(0, 3, 1, 2).reshape(nseq, SSM_STATE, inner)
    lane_pad = lambda v: jnp.pad(v, (0, LANES - heads)).reshape(1, LANES)
    expand = (np.arange(LANES)[:, None] == (np.arange(inner)[None, :] // SSM_HEAD_DIM)).astype(np.float32)
    tri = np.tril(np.ones((CHUNK, CHUNK), np.float32))
    consts = (conv_w[0][:, :inner], conv_w[0][:, inner:],
              conv_b[0][:inner].reshape(1, inner), conv_b[0][inner:].reshape(1, bcw),
              lane_pad(dt_bias[0]), lane_pad(-jnp.exp(a_log[0])),
              jnp.repeat(d_skip[0], SSM_HEAD_DIM).reshape(1, inner), gn_w[0].reshape(1, inner),
              jnp.asarray(expand, BF16), jnp.asarray(tri, BF16))
    y_ssd, st_out = _ssd(proj, dt_raw, seq_of_chunk, first_of_chunk, pre_x, pre_bc, h0t, consts, inner)

    bias2 = _attn_bias(rel_bias[0])
    att_p = _attn_prompt(proj, bias2, tp, att_w)
    att_s = _attn_sample(proj, cache_k[0].reshape(bs * ATT_PAST, att_w), cache_v[0].reshape(bs * ATT_PAST, att_w),
                         bias2, tp, bs, att_w)

    wo = w_out[0].astype(BF16)
    x1, hf, hb, logits = _outproj(y_ssd, att_p, att_s, xp, xs, gate_m, scale_f, shift_f,
                                  norm_post_mix, norm_pre_ffn, wo[:inner], wo[inner:], w_router[0])

    topi_t, topw_t = _route(logits.T, router_bias[0])
    a_sorted, blk_e, blk_r0, blk_first, pos = _dispatch_plan(topi_t, MOE_ROWS, ne)
    eo = _moe(hf, a_sorted, blk_e, blk_r0, blk_first, w1[0], w3[0], w2[0])
    y_p, y_s = _final(eo, pos, topw_t.T, hb, x1, gate_f, norm_post_ffn,
                      ws1[0].astype(BF16), ws3[0].astype(BF16), ws2[0].astype(BF16), tp)

    tail = lambda rows: jnp.concatenate([rows[..., c_x:c_x + inner], rows[..., c_bc:c_bc + bcw]], axis=-1)
    conv_prompt = tail(proj[tp - (CONV_W - 1):tp])[None, None]
    srows = lambda c0, n: proj[tp:, c0:c0 + n].reshape(bs, ls, n)[:, ls - (CONV_W - 1):, :]
    conv_sample = jnp.concatenate([srows(c_x, inner), srows(c_bc, bcw)], axis=-1)[None]
    st = st_out.reshape(nseq, SSM_STATE, heads, SSM_HEAD_DIM).transpose(0, 2, 3, 1)
    keep = min(ATT_PAST, lp)
    hd = (rel_bias.shape[1], ATT_HEAD_DIM)
    kv = lambda c0, r0, r1, b, l: proj[r0:r1, c0:c0 + att_w].reshape(b, l, *hd)[None]
    return (y_p.reshape(bp, lp, d), y_s.reshape(bs, ls, d),
            conv_prompt, st[:bp][None], kv(c_k, tp - keep, tp, bp, keep), kv(c_v, tp - keep, tp, bp, keep),
            conv_sample, st[bp:][None], kv(c_k, tp, tp + ts, bs, ls), kv(c_v, tp, tp + ts, bs, ls))
```
